```python
import math
import jax
import jax.numpy as jnp
from jax import lax
import numpy as np

D_MODEL = 1024
BATCH = 16
SEQ = 2048
DEPTH = 2
DEC_BATCH = 128
DEC_SEQ = 1
PAST_LEN = 16384
PAGE_SIZE = 128

SWA_HEADS = 8
SWA_KV_HEADS = 2
SWA_GROUP = SWA_HEADS // SWA_KV_HEADS
HEAD_DIM = 64
WINDOW = 128
SWA_BLOCK = 128
SWA_Q = SWA_HEADS * HEAD_DIM
SWA_KV = SWA_KV_HEADS * HEAD_DIM
SSM_HEADS = 8
SSM_HEAD_DIM = 64
SSM_GROUPS = 2
SSM_STATE = 128
SSM_INNER = SSM_HEADS * SSM_HEAD_DIM
SSM_CHUNK = 128
SSM_CONV_CH = SSM_INNER + 2 * SSM_GROUPS * SSM_STATE
CONV_WIDTH = 4
DT_MIN = 0.001
DT_MAX = 0.1
DN_HEADS = 8
DN_DK = 128
DN_DV = 128
DN_CHUNK = 64
DN_QK = DN_HEADS * DN_DK
DN_VW = DN_HEADS * DN_DV
DN_CONV_CH = 2 * DN_QK + DN_VW
EVEN_IN = SWA_Q + 2 * SWA_KV + SSM_INNER + SSM_CONV_CH + SSM_HEADS
EVEN_MIX = SWA_Q + SSM_INNER
ODD_IN = DN_CONV_CH + DN_VW + 2 * DN_HEADS
ODD_MIX = DN_VW
N_GROUPS = 4
EXPERTS_PER_GROUP = 8
N_EXPERTS = N_GROUPS * EXPERTS_PER_GROUP
TOP_K = 2
EXPERT_FF = 256
N_EVEN = (DEPTH + 1) // 2
N_ODD = DEPTH // 2
EPS = 1e-6
F32 = jnp.float32

kernel_name = 'hybrid_swa_ssd_deltanet_hmoe_step'


def rmsnorm(x, g):
    xf = x.astype(F32)
    y = xf * lax.rsqrt(jnp.mean(xf * xf, axis=-1, keepdims=True) + EPS)
    return (y * g.astype(F32)).astype(x.dtype)


def l2norm(x):
    xf = x.astype(F32)
    return (xf * lax.rsqrt(jnp.sum(xf * xf, axis=-1, keepdims=True) + EPS)).astype(x.dtype)


def causal_conv(x, buf, w):
    xx = jnp.concatenate([buf.astype(x.dtype), x], axis=1)
    y = lax.conv_general_dilated(xx, w[:, None, :].astype(x.dtype), window_strides=(1,), padding='VALID',
                                 dimension_numbers=('NWC', 'WIO', 'NWC'), feature_group_count=x.shape[-1])
    return y, xx[:, -(CONV_WIDTH - 1):]


def sink_attend(s, sink, v, eq):
    sink = sink.astype(F32)
    m = jnp.maximum(jnp.max(s, axis=-1, keepdims=True), sink)
    p = jnp.exp(s - m)
    p = p / (jnp.sum(p, axis=-1, keepdims=True) + jnp.exp(sink - m))
    return jnp.einsum(eq, p.astype(v.dtype), v)


def swa_prompt(q, k, v, sinks):
    bsz, L = q.shape[0], q.shape[1]
    nb = L // SWA_BLOCK
    qb = q.reshape(bsz, nb, SWA_BLOCK, SWA_KV_HEADS, SWA_GROUP, HEAD_DIM)

    def band(t):
        tb = t.reshape(bsz, nb, SWA_BLOCK, SWA_KV_HEADS, HEAD_DIM)
        prev = jnp.concatenate([jnp.zeros_like(tb[:, :1]), tb[:, :-1]], axis=1)
        return jnp.concatenate([prev, tb], axis=2)

    kb, vb = band(k), band(v)
    kj = jnp.arange(2 * SWA_BLOCK)
    rel = (jnp.arange(SWA_BLOCK)[:, None] + SWA_BLOCK) - kj[None, :]
    has_prev = (jnp.arange(nb)[:, None, None] > 0) | (kj[None, None, :] >= SWA_BLOCK)
    mask = (rel >= 0) & (rel <= WINDOW) & has_prev
    s = jnp.einsum('bnqkgd,bnskd->bnkgqs', qb, kb).astype(F32) * (HEAD_DIM ** -0.5)
    s = jnp.where(mask[None, :, None, None], s, -jnp.inf)
    o = sink_attend(s, sinks.reshape(SWA_KV_HEADS, SWA_GROUP, 1, 1), vb, 'bnkgqs,bnskd->bnqkgd')
    return o.reshape(bsz, L, SWA_Q)


def swa_decode(q, k, v, k_buf, v_buf, sinks):
    bsz, L = q.shape[0], q.shape[1]
    w = k_buf.shape[1]
    kk = jnp.concatenate([k_buf.astype(k.dtype), k], axis=1)
    vv = jnp.concatenate([v_buf.astype(v.dtype), v], axis=1)
    qg = q.reshape(bsz, L, SWA_KV_HEADS, SWA_GROUP, HEAD_DIM)
    s = jnp.einsum('blkgd,bskd->bkgls', qg, kk).astype(F32) * (HEAD_DIM ** -0.5)
    rel = jnp.arange(L)[:, None] - (jnp.arange(w + L)[None, :] - w)
    mask = (rel >= 0) & (rel <= WINDOW)
    s = jnp.where(mask, s, -jnp.inf)
    o = sink_attend(s, sinks.reshape(SWA_KV_HEADS, SWA_GROUP, 1, 1), vv, 'bkgls,bskd->blkgd')
    return o.reshape(bsz, L, SWA_Q), kk[:, -w:], vv[:, -w:]


def ssd_scan(x, dt, a, bm, cm, h0):
    bsz, L, H, P = x.shape
    G, N = bm.shape[2], bm.shape[3]
    hg = H // G
    Q = SSM_CHUNK if L % SSM_CHUNK == 0 else L
    nc = L // Q

    def chunks(t):
        t = t.astype(F32).reshape((bsz, nc, Q) + t.shape[2:])
        return jnp.moveaxis(t, 1, 0)

    xg = x.reshape(bsz, L, G, hg, P)
    dtg = dt.reshape(bsz, L, G, hg)
    da = dtg * a.reshape(G, hg)
    causal = jnp.tril(jnp.ones((Q, Q), dtype=bool))[None, :, :, None, None]

    def step(h, inp):
        xc, dtc, dac, bc, cc = inp
        cum = jnp.cumsum(dac, axis=1)
        seg = jnp.exp(jnp.where(causal, cum[:, :, None] - cum[:, None, :], -jnp.inf))
        cb = jnp.einsum('bign,bjgn->bijg', cc, bc)
        wgt = cb[..., None] * seg * dtc[:, None]
        y = jnp.einsum('bijgh,bjghp->bighp', wgt, xc)
        y = y + jnp.einsum('bign,bghpn->bighp', cc, h) * jnp.exp(cum)[..., None]
        tail = jnp.exp(cum[:, -1:] - cum) * dtc
        h = h * jnp.exp(cum[:, -1])[..., None, None] + jnp.einsum('bjgh,bjghp,bjgn->bghpn', tail, xc, bc)
        return h, y

    h_fin, y = lax.scan(step, h0.astype(F32).reshape(bsz, G, hg, P, N),
                        (chunks(xg), chunks(dtg), chunks(da), chunks(bm), chunks(cm)))
    y = jnp.moveaxis(y, 0, 1).reshape(bsz, L, H, P)
    return y, h_fin.reshape(bsz, H, P, N)


def gated_delta_rule(q, k, v, g, beta, s0):
    bsz, L, H = q.shape[0], q.shape[1], q.shape[2]
    dv = v.shape[-1]
    C = DN_CHUNK if L % DN_CHUNK == 0 else L
    nc = L // C

    def chunks(t):
        t = t.astype(F32).reshape((bsz, nc, C, H) + t.shape[3:])
        return jnp.moveaxis(t, (1, 3), (0, 2))

    incl = jnp.tril(jnp.ones((C, C), dtype=bool))
    strict = jnp.tril(jnp.ones((C, C), dtype=bool), -1)
    eye = jnp.eye(C, dtype=F32)

    def step(S, inp):
        qc, kc, vc, gc, bc = inp
        cum = jnp.cumsum(gc, axis=-1)
        decay = jnp.exp(jnp.where(incl, cum[..., :, None] - cum[..., None, :], -jnp.inf))
        kb = kc * bc[..., None]
        lmat = jnp.where(strict, jnp.einsum('bhid,bhjd->bhij', kb, kc) * decay, 0.0)
        tinv = lax.linalg.triangular_solve(eye + lmat, jnp.broadcast_to(eye, lmat.shape),
                                           left_side=True, lower=True, unit_diagonal=True)
        u = tinv @ (vc * bc[..., None])
        w = tinv @ (kb * jnp.exp(cum)[..., None])
        v_new = u - w @ S
        attn = jnp.einsum('bhid,bhjd->bhij', qc, kc) * decay
        o = (qc * jnp.exp(cum)[..., None]) @ S + attn @ v_new
        S = S * jnp.exp(cum[..., -1])[..., None, None] + jnp.einsum(
            'bhcd,bhce->bhde', kc * jnp.exp(cum[..., -1:] - cum)[..., None], v_new)
        return S, o

    s_fin, o = lax.scan(step, s0.astype(F32),
                        (chunks(q), chunks(k), chunks(v), chunks(g), chunks(beta)))
    o = jnp.moveaxis(o, (0, 2), (1, 3)).reshape(bsz, L, H, dv)
    return o, s_fin


def even_mixer(h, p, i, k_buf, v_buf, ssm_h0, conv_buf):
    bsz, L = h.shape[0], h.shape[1]
    u = h @ p['w_in_even'][i]
    o1 = SWA_Q
    o2 = o1 + SWA_KV
    o3 = o2 + SWA_KV
    o4 = o3 + SSM_INNER
    o5 = o4 + SSM_CONV_CH
    q, k, v, z, xbc, dt = jnp.split(u, [o1, o2, o3, o4, o5], axis=-1)
    q = rmsnorm(q.reshape(bsz, L, SWA_HEADS, HEAD_DIM), p['q_norm'][i])
    k = rmsnorm(k.reshape(bsz, L, SWA_KV_HEADS, HEAD_DIM), p['k_norm'][i])
    v = v.reshape(bsz, L, SWA_KV_HEADS, HEAD_DIM)
    sinks = p['attn_sinks'][i]
    if k_buf is None:
        att = swa_prompt(q, k, v, sinks)
        new_k, new_v = k[:, -WINDOW:], v[:, -WINDOW:]
    else:
        att, new_k, new_v = swa_decode(q, k, v, k_buf, v_buf, sinks)
    xbc, new_conv = causal_conv(xbc, conv_buf, p['ssm_conv_w'][i])
    xbc = jax.nn.silu(xbc + p['ssm_conv_b'][i])
    xs, bm, cm = jnp.split(xbc, [SSM_INNER, SSM_INNER + SSM_GROUPS * SSM_STATE], axis=-1)
    xs = xs.reshape(bsz, L, SSM_HEADS, SSM_HEAD_DIM)
    dt = jax.nn.softplus(dt.astype(F32) + p['ssm_dt_bias'][i].astype(F32))
    a = -jnp.exp(p['ssm_A_log'][i].astype(F32))
    y, new_h = ssd_scan(xs, dt, a, bm.reshape(bsz, L, SSM_GROUPS, SSM_STATE),
                        cm.reshape(bsz, L, SSM_GROUPS, SSM_STATE), ssm_h0)
    y = y.astype(h.dtype) + p['ssm_D'][i][:, None] * xs
    y = y.reshape(bsz, L, SSM_INNER) * jax.nn.silu(z)
    y = rmsnorm(y.reshape(bsz, L, SSM_GROUPS, SSM_INNER // SSM_GROUPS),
                p['ssm_norm'][i].reshape(SSM_GROUPS, SSM_INNER // SSM_GROUPS))
    mix = jnp.concatenate([att, y.reshape(bsz, L, SSM_INNER)], axis=-1) @ p['w_out_even'][i]
    return mix, new_k, new_v, new_h.astype(h.dtype), new_conv


def odd_mixer(h, p, j, s0, conv_buf):
    bsz, L = h.shape[0], h.shape[1]
    u = h @ p['w_in_odd'][j]
    qkv, z, b, a = jnp.split(u, [DN_CONV_CH, DN_CONV_CH + DN_VW, DN_CONV_CH + DN_VW + DN_HEADS], axis=-1)
    qkv, new_conv = causal_conv(qkv, conv_buf, p['dn_conv_w'][j])
    qkv = jax.nn.silu(qkv)
    q, k, v = jnp.split(qkv, [DN_QK, 2 * DN_QK], axis=-1)
    q = l2norm(q.reshape(bsz, L, DN_HEADS, DN_DK)) * (DN_DK ** -0.5)
    k = l2norm(k.reshape(bsz, L, DN_HEADS, DN_DK))
    v = v.reshape(bsz, L, DN_HEADS, DN_DV)
    beta = jax.nn.sigmoid(b.astype(F32))
    g = -jnp.exp(p['dn_A_log'][j].astype(F32)) * jax.nn.softplus(a.astype(F32) + p['dn_dt_bias'][j].astype(F32))
    o, s_new = gated_delta_rule(q, k, v, g, beta, s0)
    o = rmsnorm(o.astype(h.dtype), p['dn_norm'][j]) * jax.nn.silu(z.reshape(bsz, L, DN_HEADS, DN_DV))
    return o.reshape(bsz, L, DN_VW) @ p['w_out_odd'][j], s_new.astype(h.dtype), new_conv


def hier_moe(h, p, l):
    shp = h.shape
    t = h.reshape(-1, D_MODEL)
    grp_prob = jax.nn.softmax((t @ p['moe_w_group'][l]).astype(F32) + p['moe_b_group'][l].astype(F32), axis=-1)
    g_p, g_i = lax.top_k(grp_prob, 1)
    e_logits = ((t @ p['moe_w_router'][l]).astype(F32) + p['moe_b_router'][l].astype(F32))
    e_logits = e_logits.reshape(-1, N_GROUPS, EXPERTS_PER_GROUP)
    sel = jnp.take_along_axis(e_logits, g_i[:, :, None], axis=1)[:, 0]
    e_p, e_i = lax.top_k(jax.nn.softmax(sel, axis=-1), TOP_K)
    gates = g_p * e_p / jnp.sum(e_p, axis=-1, keepdims=True)
    eid = g_i * EXPERTS_PER_GROUP + e_i
    combine = jnp.sum(jax.nn.one_hot(eid, N_EXPERTS, dtype=F32) * gates[..., None], axis=1)
    out = jnp.zeros(t.shape, F32)
    for e in range(N_EXPERTS):
        act = jax.nn.silu(t @ p['moe_w_gate'][l, e]) * (t @ p['moe_w_up'][l, e])
        out = out + combine[:, e:e + 1] * (act @ p['moe_w_down'][l, e]).astype(F32)
    return out.astype(h.dtype).reshape(shp)


def trunk(x, p, k_win, v_win, ssm, ssm_conv, dn, dn_conv):
    nk, nv, nssm, nsc, ndn, ndc = [], [], [], [], [], []
    for l in range(DEPTH):
        h = rmsnorm(x, p['ln_mix'][l])
        if l % 2 == 0:
            i = l // 2
            kb = None if k_win is None else k_win[i]
            vb = None if v_win is None else v_win[i]
            mix, k_i, v_i, s_i, c_i = even_mixer(h, p, i, kb, vb, ssm[i], ssm_conv[i])
            nk.append(k_i)
            nv.append(v_i)
            nssm.append(s_i)
            nsc.append(c_i)
        else:
            j = l // 2
            mix, s_j, c_j = odd_mixer(h, p, j, dn[j], dn_conv[j])
            ndn.append(s_j)
            ndc.append(c_j)
        x = x + mix
        x = x + hier_moe(rmsnorm(x, p['ln_ffn'][l]), p, l)
    return x, jnp.stack(nk), jnp.stack(nv), jnp.stack(nssm), jnp.stack(nsc), jnp.stack(ndn), jnp.stack(ndc)


def setup_inputs(seed: int = 0) -> dict:
    key = jax.random.key(seed)
    k = jax.random.split(key, 34)

    def nrm(kk, shape, scale):
        return scale * jax.random.normal(kk, shape, F32)

    def gain(kk, shape):
        return 1.0 + 0.02 * jax.random.normal(kk, shape, F32)

    def dt_bias(kk, shape):
        u = jax.random.uniform(kk, shape, F32)
        dt = jnp.exp(u * (math.log(DT_MAX) - math.log(DT_MIN)) + math.log(DT_MIN))
        return dt + jnp.log(-jnp.expm1(-dt))

    def a_log(kk, shape):
        return jnp.log(jax.random.uniform(kk, shape, F32, 1.0, 16.0))

    win = min(WINDOW, PAST_LEN)
    return {
        'x_prompt': nrm(k[0], (BATCH, SEQ, D_MODEL), 1.0),
        'x_sample': nrm(k[1], (DEC_BATCH, DEC_SEQ, D_MODEL), 1.0),
        'cache_k_win': nrm(k[2], (N_EVEN, DEC_BATCH, win, SWA_KV_HEADS, HEAD_DIM), 1.0),
        'cache_v_win': nrm(k[3], (N_EVEN, DEC_BATCH, win, SWA_KV_HEADS, HEAD_DIM), 1.0),
        'state_ssm': nrm(k[4], (N_EVEN, DEC_BATCH, SSM_HEADS, SSM_HEAD_DIM, SSM_STATE), 0.1),
        'state_ssm_conv': nrm(k[5], (N_EVEN, DEC_BATCH, CONV_WIDTH - 1, SSM_CONV_CH), 1.0),
        'state_dn': nrm(k[6], (N_ODD, DEC_BATCH, DN_HEADS, DN_DK, DN_DV), 0.1),
        'state_dn_conv': nrm(k[7], (N_ODD, DEC_BATCH, CONV_WIDTH - 1, DN_CONV_CH), 1.0),
        'ln_mix': gain(k[8], (DEPTH, D_MODEL)),
        'ln_ffn': gain(k[9], (DEPTH, D_MODEL)),
        'w_in_even': nrm(k[10], (N_EVEN, D_MODEL, EVEN_IN), D_MODEL ** -0.5),
        'q_norm': gain(k[11], (N_EVEN, HEAD_DIM)),
        'k_norm': gain(k[12], (N_EVEN, HEAD_DIM)),
        'attn_sinks': nrm(k[13], (N_EVEN, SWA_HEADS), 0.5),
        'ssm_conv_w': nrm(k[14], (N_EVEN, CONV_WIDTH, SSM_CONV_CH), CONV_WIDTH ** -0.5),
        'ssm_conv_b': nrm(k[15], (N_EVEN, SSM_CONV_CH), 0.01),
        'ssm_dt_bias': dt_bias(k[16], (N_EVEN, SSM_HEADS)),
        'ssm_A_log': a_log(k[17], (N_EVEN, SSM_HEADS)),
        'ssm_D': gain(k[18], (N_EVEN, SSM_HEADS)),
        'ssm_norm': gain(k[19], (N_EVEN, SSM_INNER)),
        'w_out_even': nrm(k[20], (N_EVEN, EVEN_MIX, D_MODEL), EVEN_MIX ** -0.5),
        'w_in_odd': nrm(k[21], (N_ODD, D_MODEL, ODD_IN), D_MODEL ** -0.5),
        'dn_conv_w': nrm(k[22], (N_ODD, CONV_WIDTH, DN_CONV_CH), CONV_WIDTH ** -0.5),
        'dn_dt_bias': dt_bias(k[23], (N_ODD, DN_HEADS)),
        'dn_A_log': a_log(k[24], (N_ODD, DN_HEADS)),
        'dn_norm': gain(k[25], (N_ODD, DN_DV)),
        'w_out_odd': nrm(k[26], (N_ODD, ODD_MIX, D_MODEL), ODD_MIX ** -0.5),
        'moe_w_group': nrm(k[27], (DEPTH, D_MODEL, N_GROUPS), D_MODEL ** -0.5),
        'moe_b_group': nrm(k[28], (DEPTH, N_GROUPS), 0.01),
        'moe_w_router': nrm(k[29], (DEPTH, D_MODEL, N_EXPERTS), D_MODEL ** -0.5),
        'moe_b_router': nrm(k[30], (DEPTH, N_EXPERTS), 0.01),
        'moe_w_gate': nrm(k[31], (DEPTH, N_EXPERTS, D_MODEL, EXPERT_FF), D_MODEL ** -0.5),
        'moe_w_up': nrm(k[32], (DEPTH, N_EXPERTS, D_MODEL, EXPERT_FF), D_MODEL ** -0.5),
        'moe_w_down': nrm(k[33], (DEPTH, N_EXPERTS, EXPERT_FF, D_MODEL), EXPERT_FF ** -0.5),
    }


def reference(x_prompt, x_sample, cache_k_win, cache_v_win, state_ssm, state_ssm_conv, state_dn, state_dn_conv,
              ln_mix, ln_ffn, w_in_even, q_norm, k_norm, attn_sinks, ssm_conv_w, ssm_conv_b, ssm_dt_bias,
              ssm_A_log, ssm_D, ssm_norm, w_out_even, w_in_odd, dn_conv_w, dn_dt_bias, dn_A_log, dn_norm,
              w_out_odd, moe_w_group, moe_b_group, moe_w_router, moe_b_router, moe_w_gate, moe_w_up, moe_w_down):
    p = {
        'ln_mix': ln_mix, 'ln_ffn': ln_ffn, 'w_in_even': w_in_even, 'q_norm': q_norm, 'k_norm': k_norm,
        'attn_sinks': attn_sinks, 'ssm_conv_w': ssm_conv_w, 'ssm_conv_b': ssm_conv_b, 'ssm_dt_bias': ssm_dt_bias,
        'ssm_A_log': ssm_A_log, 'ssm_D': ssm_D, 'ssm_norm': ssm_norm, 'w_out_even': w_out_even,
        'w_in_odd': w_in_odd, 'dn_conv_w': dn_conv_w, 'dn_dt_bias': dn_dt_bias, 'dn_A_log': dn_A_log,
        'dn_norm': dn_norm, 'w_out_odd': w_out_odd, 'moe_w_group': moe_w_group, 'moe_b_group': moe_b_group,
        'moe_w_router': moe_w_router, 'moe_b_router': moe_b_router, 'moe_w_gate': moe_w_gate,
        'moe_w_up': moe_w_up, 'moe_w_down': moe_w_down,
    }
    bp = x_prompt.shape[0]
    dtp = x_prompt.dtype
    ssm0 = jnp.zeros((N_EVEN, bp, SSM_HEADS, SSM_HEAD_DIM, SSM_STATE), dtp)
    ssmc0 = jnp.zeros((N_EVEN, bp, CONV_WIDTH - 1, SSM_CONV_CH), dtp)
    dn0 = jnp.zeros((N_ODD, bp, DN_HEADS, DN_DK, DN_DV), dtp)
    dnc0 = jnp.zeros((N_ODD, bp, CONV_WIDTH - 1, DN_CONV_CH), dtp)
    y_p, kp, vp, sp, scp, dnp, dncp = trunk(x_prompt, p, None, None, ssm0, ssmc0, dn0, dnc0)
    y_s, ks, vs, ss, scs, dns, dncs = trunk(x_sample, p, cache_k_win, cache_v_win, state_ssm, state_ssm_conv,
                                            state_dn, state_dn_conv)
    return (y_p, y_s, kp, ks, vp, vs, sp, ss, scp, scs, dnp, dns, dncp, dncs)
```

```python
import functools

import jax
import jax.numpy as jnp
from jax import lax
from jax.experimental import pallas as pl
from jax.experimental.pallas import tpu as pltpu

F32 = jnp.float32
BF16 = jnp.bfloat16
EPS = 1e-6

D_MODEL = 1024
SWA_HEADS = 8
SWA_KV_HEADS = 2
SWA_GROUP = SWA_HEADS // SWA_KV_HEADS
HEAD_DIM = 64
WINDOW = 128
SWA_Q = SWA_HEADS * HEAD_DIM
SWA_KV = SWA_KV_HEADS * HEAD_DIM
SSM_HEADS = 8
SSM_HEAD_DIM = 64
SSM_GROUPS = 2
SSM_STATE = 128
SSM_INNER = SSM_HEADS * SSM_HEAD_DIM
SSM_CHUNK = 128
SSM_CONV_CH = SSM_INNER + 2 * SSM_GROUPS * SSM_STATE
CONV_WIDTH = 4
DN_HEADS = 8
DN_DK = 128
DN_DV = 128
DN_CHUNK = 64
DN_QK = DN_HEADS * DN_DK
DN_VW = DN_HEADS * DN_DV
DN_CONV_CH = 2 * DN_QK + DN_VW
N_GROUPS = 4
EXPERTS_PER_GROUP = 8
N_EXPERTS = N_GROUPS * EXPERTS_PER_GROUP
EXPERT_FF = 256

LANES = 128
SUBLANES = 8
VMEM_LIMIT = 56 * 1024 * 1024

NT_DIMS = (((1,), (1,)), ((), ()))
TN_DIMS = (((0,), (0,)), ((), ()))


def _cparams(sem):
    return pltpu.CompilerParams(dimension_semantics=sem, vmem_limit_bytes=VMEM_LIMIT)


def _const_spec(shape):
    nd = len(shape)
    return pl.BlockSpec(shape, lambda *_: (0,) * nd)


def _sigmoid(x):
    return 1.0 / (1.0 + jnp.exp(-x))


def _silu(x):
    return x * _sigmoid(x)


def _softplus(x):
    return jnp.maximum(x, 0.0) + jnp.log(1.0 + jnp.exp(-jnp.abs(x)))


def _rms(x, gain):
    return x * lax.rsqrt(jnp.mean(x * x, axis=-1, keepdims=True) + EPS) * gain


def _prenorm_proj_kernel(x_ref, g_ref, w_ref, *out_refs, segs):
    h = _rms(x_ref[...], g_ref[...]).astype(BF16)
    for o_ref, (a, b) in zip(out_refs, segs):
        o_ref[...] = jnp.dot(h, w_ref[:, a:b], preferred_element_type=F32)


def prenorm_proj(x, gain, w_bf16, segs, tm):
    t = x.shape[0]
    n_pad = w_bf16.shape[1]
    return pl.pallas_call(
        functools.partial(_prenorm_proj_kernel, segs=segs),
        grid=(t // tm,),
        in_specs=[pl.BlockSpec((tm, D_MODEL), lambda i: (i, 0)),
                  _const_spec((1, D_MODEL)),
                  _const_spec((D_MODEL, n_pad))],
        out_specs=[pl.BlockSpec((tm, b - a), lambda i: (i, 0)) for a, b in segs],
        out_shape=[jax.ShapeDtypeStruct((t, b - a), F32) for a, b in segs],
        compiler_params=_cparams(("parallel",)),
        name="prenorm_proj",
    )(x, gain.reshape(1, D_MODEL), w_bf16)


def _resid_proj_kernel(x_ref, *refs, n_in):
    acc = x_ref[...]
    for a_ref, w_ref in zip(refs[:n_in], refs[n_in:2 * n_in]):
        acc = acc + jnp.dot(a_ref[...].astype(BF16), w_ref[...], preferred_element_type=F32)
    refs[2 * n_in][...] = acc


def resid_proj(x, acts, ws, tm):
    t = x.shape[0]
    n_in = len(acts)
    return pl.pallas_call(
        functools.partial(_resid_proj_kernel, n_in=n_in),
        grid=(t // tm,),
        in_specs=([pl.BlockSpec((tm, D_MODEL), lambda i: (i, 0))]
                  + [pl.BlockSpec((tm, a.shape[1]), lambda i: (i, 0)) for a in acts]
                  + [_const_spec(w.shape) for w in ws]),
        out_specs=pl.BlockSpec((tm, D_MODEL), lambda i: (i, 0)),
        out_shape=jax.ShapeDtypeStruct((t, D_MODEL), F32),
        compiler_params=_cparams(("parallel",)),
        name="resid_proj",
    )(x, *acts, *ws)


def _swa_kernel(sink_ref, q_ref, kc_ref, kp_ref, vc_ref, vp_ref, qn_ref, kn_ref, o_ref, kout_ref, *,
                norm_prev, first_has_prev):
    n = pl.program_id(1)
    blk = q_ref.shape[0]
    q = q_ref[...]
    kc, kp, vc, vp = kc_ref[...], kp_ref[...], vc_ref[...], vp_ref[...]
    qn, kn = qn_ref[...], kn_ref[...]
    row = lax.broadcasted_iota(jnp.int32, (blk, 2 * blk), 0)
    col = lax.broadcasted_iota(jnp.int32, (blk, 2 * blk), 1)
    rel = row + blk - col
    mask = (rel >= 0) & (rel <= WINDOW)
    if not first_has_prev:
        mask = mask & ((n > 0) | (col >= blk))
    outs, kouts = [], []
    for j in range(SWA_KV_HEADS):
        sl = slice(j * HEAD_DIM, (j + 1) * HEAD_DIM)
        kcj = _rms(kc[:, sl], kn)
        kpj = _rms(kp[:, sl], kn) if norm_prev else kp[:, sl]
        kouts.append(kcj)
        kcat = jnp.concatenate([kpj, kcj], axis=0).astype(BF16)
        vcat = jnp.concatenate([vp[:, sl], vc[:, sl]], axis=0).astype(BF16)
        for g in range(SWA_GROUP):
            h = j * SWA_GROUP + g
            qh = _rms(q[:, h * HEAD_DIM:(h + 1) * HEAD_DIM], qn) * (HEAD_DIM ** -0.5)
            s = lax.dot_general(qh.astype(BF16), kcat, NT_DIMS, preferred_element_type=F32)
            s = jnp.where(mask, s, -jnp.inf)
            sink = sink_ref[h]
            m = jnp.maximum(jnp.max(s, axis=-1, keepdims=True), sink)
            p = jnp.exp(s - m)
            p = p / (jnp.sum(p, axis=-1, keepdims=True) + jnp.exp(sink - m))
            outs.append(jnp.dot(p.astype(BF16), vcat, preferred_element_type=F32))
    o_ref[...] = jnp.concatenate(outs, axis=1)
    kout_ref[...] = jnp.concatenate(kouts, axis=1)


def swa_attention(q, k, v, k_prev, v_prev, q_norm, k_norm, sinks, *, prompt):
    bsz, length = q.shape[0], q.shape[1]
    nb = length // WINDOW
    if prompt:
        prev_map = lambda b, n: (b, jnp.maximum(n - 1, 0), 0)
    else:
        prev_map = lambda b, n: (b, 0, 0)
    cur = lambda b, n: (b, n, 0)
    kv_blk = (None, WINDOW, SWA_KV)
    return pl.pallas_call(
        functools.partial(_swa_kernel, norm_prev=prompt, first_has_prev=not prompt),
        grid=(bsz, nb),
        in_specs=[pl.BlockSpec(memory_space=pltpu.SMEM),
                  pl.BlockSpec((None, WINDOW, SWA_Q), cur),
                  pl.BlockSpec(kv_blk, cur), pl.BlockSpec(kv_blk, prev_map),
                  pl.BlockSpec(kv_blk, cur), pl.BlockSpec(kv_blk, prev_map),
                  _const_spec((1, HEAD_DIM)), _const_spec((1, HEAD_DIM))],
        out_specs=[pl.BlockSpec((None, WINDOW, SWA_Q), cur), pl.BlockSpec(kv_blk, cur)],
        out_shape=[jax.ShapeDtypeStruct((bsz, length, SWA_Q), F32),
                   jax.ShapeDtypeStruct((bsz, length, SWA_KV), F32)],
        compiler_params=_cparams(("parallel", "arbitrary")),
        name="swa_attention",
    )(sinks, q, k, k_prev, v, v_prev, q_norm.reshape(1, HEAD_DIM), k_norm.reshape(1, HEAD_DIM))


def _chunk_conv(x_ref, xx_scr, w_ref, rows):
    xx_scr[SUBLANES:SUBLANES + rows, :] = x_ref[...]
    acc = None
    for tap in range(CONV_WIDTH):
        off = SUBLANES - (CONV_WIDTH - 1) + tap
        term = w_ref[tap:tap + 1, :] * xx_scr[off:off + rows, :]
        acc = term if acc is None else acc + term
    return acc


def _carry_conv_tail(xx_scr, rows):
    xx_scr[0:SUBLANES, :] = xx_scr[rows:rows + SUBLANES, :]


def _ssd_kernel(xbc_ref, z_ref, dt_ref, cw_ref, cb_ref, dtb_ref, alog_ref, dd_ref, nrm_ref, tail0_ref, h0_ref,
                y_ref, hout_ref, xx_scr, h_scr, *, n_valid):
    q_len = xbc_ref.shape[0]

    @pl.when(pl.program_id(1) == 0)
    def _():
        xx_scr[0:SUBLANES, :] = tail0_ref[...]
        h_scr[...] = h0_ref[...]

    act = _silu(_chunk_conv(xbc_ref, xx_scr, cw_ref, q_len) + cb_ref[...])
    _carry_conv_tail(xx_scr, q_len)
    xs = act[:, :SSM_INNER]
    bm = act[:, SSM_INNER:SSM_INNER + SSM_GROUPS * SSM_STATE]
    cm = act[:, SSM_INNER + SSM_GROUPS * SSM_STATE:]

    row = lax.broadcasted_iota(jnp.int32, (q_len, q_len), 0)
    col = lax.broadcasted_iota(jnp.int32, (q_len, q_len), 1)
    causal = row >= col
    dt = _softplus(dt_ref[...] + dtb_ref[...])
    if n_valid < q_len:
        dt = jnp.where(lax.broadcasted_iota(jnp.int32, dt.shape, 0) < n_valid, dt, 0.0)
    da = dt * (-jnp.exp(alog_ref[...]))
    cum = jnp.dot(causal.astype(F32), da, preferred_element_type=F32, precision=lax.Precision.HIGHEST)
    cum_t = cum.T
    dt_t = dt.T
    e_cum = jnp.exp(cum)
    hpg = SSM_HEADS // SSM_GROUPS
    gw = hpg * SSM_HEAD_DIM
    ys = []
    for g in range(SSM_GROUPS):
        bm_g = bm[:, g * SSM_STATE:(g + 1) * SSM_STATE].astype(BF16)
        cm_g = cm[:, g * SSM_STATE:(g + 1) * SSM_STATE].astype(BF16)
        cb = lax.dot_general(cm_g, bm_g, NT_DIMS, preferred_element_type=F32)
        h_g = h_scr[g * gw:(g + 1) * gw, :]
        y_state = lax.dot_general(cm_g, h_g.astype(BF16), NT_DIMS, preferred_element_type=F32)
        xt_parts, dec_parts = [], []
        for hh in range(hpg):
            h = g * hpg + hh
            x_h = xs[:, h * SSM_HEAD_DIM:(h + 1) * SSM_HEAD_DIM]
            cum_c = cum[:, h:h + 1]
            seg = jnp.exp(jnp.where(causal, cum_c - cum_t[h:h + 1, :], -jnp.inf))
            wgt = cb * seg * dt_t[h:h + 1, :]
            y = jnp.dot(wgt.astype(BF16), x_h.astype(BF16), preferred_element_type=F32)
            y = y + y_state[:, hh * SSM_HEAD_DIM:(hh + 1) * SSM_HEAD_DIM] * e_cum[:, h:h + 1]
            ys.append(y + dd_ref[0, h] * x_h)
            c_last = cum[q_len - 1:q_len, h:h + 1]
            xt_parts.append(x_h * (jnp.exp(c_last - cum_c) * dt[:, h:h + 1]))
            dec_parts.append(jnp.broadcast_to(jnp.exp(c_last), (SSM_HEAD_DIM, SSM_STATE)))
        xt = jnp.concatenate(xt_parts, axis=1).astype(BF16)
        upd = lax.dot_general(xt, bm_g, TN_DIMS, preferred_element_type=F32)
        h_scr[g * gw:(g + 1) * gw, :] = h_g * jnp.concatenate(dec_parts, axis=0) + upd
    y_all = jnp.concatenate(ys, axis=1) * _silu(z_ref[...])
    nrm = nrm_ref[...]
    y_ref[...] = jnp.concatenate(
        [_rms(y_all[:, g * gw:(g + 1) * gw], nrm[:, g * gw:(g + 1) * gw]) for g in range(SSM_GROUPS)], axis=1)
    hout_ref[...] = h_scr[...]


def ssd_mixer(xbc, z, dt, conv_w, conv_b, dt_bias, a_log, d_skip, norm_g, tail0, h0, n_valid):
    bsz, length = xbc.shape[0], xbc.shape[1]
    nc = length // SSM_CHUNK
    cur = lambda b, c: (b, c, 0)
    per_b = lambda b, c: (b, 0, 0)
    pad8 = lambda v: jnp.pad(v.reshape(1, SSM_HEADS), ((0, 0), (0, LANES - SSM_HEADS)))
    return pl.pallas_call(
        functools.partial(_ssd_kernel, n_valid=n_valid),
        grid=(bsz, nc),
        in_specs=[pl.BlockSpec((None, SSM_CHUNK, SSM_CONV_CH), cur),
                  pl.BlockSpec((None, SSM_CHUNK, SSM_INNER), cur),
                  pl.BlockSpec((None, SSM_CHUNK, LANES), cur),
                  _const_spec((CONV_WIDTH, SSM_CONV_CH)), _const_spec((1, SSM_CONV_CH)),
                  _const_spec((1, LANES)), _const_spec((1, LANES)),
                  pl.BlockSpec(memory_space=pltpu.SMEM),
                  _const_spec((1, SSM_INNER)),
                  pl.BlockSpec((None, SUBLANES, SSM_CONV_CH), per_b),
                  pl.BlockSpec((None, SSM_INNER, SSM_STATE), per_b)],
        out_specs=[pl.BlockSpec((None, SSM_CHUNK, SSM_INNER), cur),
                   pl.BlockSpec((None, SSM_INNER, SSM_STATE), per_b)],
        out_shape=[jax.ShapeDtypeStruct((bsz, length, SSM_INNER), F32),
                   jax.ShapeDtypeStruct((bsz, SSM_INNER, SSM_STATE), F32)],
        scratch_shapes=[pltpu.VMEM((SSM_CHUNK + SUBLANES, SSM_CONV_CH), F32),
                        pltpu.VMEM((SSM_INNER, SSM_STATE), F32)],
        compiler_params=_cparams(("parallel", "arbitrary")),
        name="ssd_mixer",
    )(xbc, z, dt, conv_w, conv_b.reshape(1, SSM_CONV_CH), pad8(dt_bias), pad8(a_log),
      d_skip.reshape(1, SSM_HEADS), norm_g.reshape(1, SSM_INNER), tail0, h0)


def _unit_lower_inverse(lmat, row, col):
    c = lmat.shape[0]
    hi = lax.Precision.HIGHEST
    mm = lambda a, b: jnp.dot(a, b, preferred_element_type=F32, precision=hi)
    eye = (row == col).astype(F32)
    blk = SUBLANES
    same = (row // blk) == (col // blk)
    x = jnp.where(same, -lmat, 0.0)
    inv = eye + x
    p = blk
    while p > 2:
        x = mm(x, x)
        inv = inv + mm(inv, x)
        p //= 2
    while blk < c:
        outer = ((row // (2 * blk)) == (col // (2 * blk))) & ((row // blk) != (col // blk))
        inv = inv - mm(mm(inv, jnp.where(outer, lmat, 0.0)), inv)
        blk *= 2
    return inv


def _dn_kernel(qkv_ref, z_ref, ba_ref, cw_ref, dtb_ref, alog_ref, nrm_ref, tail0_ref, s0_ref,
               o_ref, sout_ref, xx_scr, s_scr, *, n_valid):
    c_len = qkv_ref.shape[0]

    @pl.when(pl.program_id(1) == 0)
    def _():
        xx_scr[0:SUBLANES, :] = tail0_ref[...]
        s_scr[...] = s0_ref[...]

    act = _silu(_chunk_conv(qkv_ref, xx_scr, cw_ref, c_len))
    _carry_conv_tail(xx_scr, c_len)

    row = lax.broadcasted_iota(jnp.int32, (c_len, c_len), 0)
    col = lax.broadcasted_iota(jnp.int32, (c_len, c_len), 1)
    incl = row >= col
    strict = row > col
    ba = ba_ref[...]
    beta = _sigmoid(ba)
    gate = -jnp.exp(alog_ref[...]) * _softplus(ba + dtb_ref[...])
    if n_valid < c_len:
        valid = lax.broadcasted_iota(jnp.int32, ba.shape, 0) < n_valid
        beta = jnp.where(valid, beta, 0.0)
        gate = jnp.where(valid, gate, 0.0)
    cum = jnp.dot(incl.astype(F32), gate, preferred_element_type=F32, precision=lax.Precision.HIGHEST)
    cum_t = cum.T
    e_cum = jnp.exp(cum)
    z = z_ref[...]
    nrm = nrm_ref[...]
    outs = []
    for h in range(DN_HEADS):
        gl = DN_HEADS + h
        q_h = act[:, h * DN_DK:(h + 1) * DN_DK]
        k_h = act[:, DN_QK + h * DN_DK:DN_QK + (h + 1) * DN_DK]
        v_h = act[:, 2 * DN_QK + h * DN_DV:2 * DN_QK + (h + 1) * DN_DV]
        q_h = q_h * lax.rsqrt(jnp.sum(q_h * q_h, axis=-1, keepdims=True) + EPS) * (DN_DK ** -0.5)
        k_h = k_h * lax.rsqrt(jnp.sum(k_h * k_h, axis=-1, keepdims=True) + EPS)
        beta_c = beta[:, h:h + 1]
        cum_c = cum[:, gl:gl + 1]
        ecum_c = e_cum[:, gl:gl + 1]
        decay = jnp.exp(jnp.where(incl, cum_c - cum_t[gl:gl + 1, :], -jnp.inf))
        kb = k_h * beta_c
        kq = lax.dot_general(jnp.concatenate([kb, q_h], axis=0).astype(BF16), k_h.astype(BF16), NT_DIMS,
                             preferred_element_type=F32)
        lmat = jnp.where(strict, kq[:c_len] * decay, 0.0)
        attn = kq[c_len:] * decay
        tinv = _unit_lower_inverse(lmat, row, col)
        rhs = jnp.concatenate([v_h * beta_c, kb * ecum_c], axis=1)
        uw = jnp.dot(tinv.astype(BF16), rhs.astype(BF16), preferred_element_type=F32)
        s_h = s_scr[h * DN_DK:(h + 1) * DN_DK, :]
        wq = jnp.dot(jnp.concatenate([uw[:, DN_DV:], q_h * ecum_c], axis=0).astype(BF16), s_h.astype(BF16),
                     preferred_element_type=F32)
        v_new = uw[:, :DN_DV] - wq[:c_len]
        o_h = wq[c_len:] + jnp.dot(attn.astype(BF16), v_new.astype(BF16), preferred_element_type=F32)
        c_last = cum[c_len - 1:c_len, gl:gl + 1]
        kd = k_h * jnp.exp(c_last - cum_c)
        s_scr[h * DN_DK:(h + 1) * DN_DK, :] = s_h * jnp.exp(c_last) + lax.dot_general(
            kd.astype(BF16), v_new.astype(BF16), TN_DIMS, preferred_element_type=F32)
        outs.append(_rms(o_h, nrm) * _silu(z[:, h * DN_DV:(h + 1) * DN_DV]))
    o_ref[...] = jnp.concatenate(outs, axis=1)
    sout_ref[...] = s_scr[...]


def dn_mixer(qkv, z, ba, conv_w, dt_bias, a_log, norm_g, tail0, s0, n_valid):
    bsz, length = qkv.shape[0], qkv.shape[1]
    nc = length // DN_CHUNK
    cur = lambda b, c: (b, c, 0)
    per_b = lambda b, c: (b, 0, 0)
    pad_a = lambda v: jnp.pad(v.reshape(1, DN_HEADS), ((0, 0), (DN_HEADS, LANES - 2 * DN_HEADS)))
    return pl.pallas_call(
        functools.partial(_dn_kernel, n_valid=n_valid),
        grid=(bsz, nc),
        in_specs=[pl.BlockSpec((None, DN_CHUNK, DN_CONV_CH), cur),
                  pl.BlockSpec((None, DN_CHUNK, DN_VW), cur),
                  pl.BlockSpec((None, DN_CHUNK, LANES), cur),
                  _const_spec((CONV_WIDTH, DN_CONV_CH)),
                  _const_spec((1, LANES)), _const_spec((1, LANES)), _const_spec((1, DN_DV)),
                  pl.BlockSpec((None, SUBLANES, DN_CONV_CH), per_b),
                  pl.BlockSpec((None, DN_QK, DN_DV), per_b)],
        out_specs=[pl.BlockSpec((None, DN_CHUNK, DN_VW), cur),
                   pl.BlockSpec((None, DN_QK, DN_DV), per_b)],
        out_shape=[jax.ShapeDtypeStruct((bsz, length, DN_VW), F32),
                   jax.ShapeDtypeStruct((bsz, DN_QK, DN_DV), F32)],
        scratch_shapes=[pltpu.VMEM((DN_CHUNK + SUBLANES, DN_CONV_CH), F32),
                        pltpu.VMEM((DN_QK, DN_DV), F32)],
        compiler_params=_cparams(("parallel", "arbitrary")),
        name="dn_mixer",
    )(qkv, z, ba, conv_w, pad_a(dt_bias), pad_a(a_log), norm_g.reshape(1, DN_DV), tail0, s0)


def _router_kernel(x_ref, g_ref, wr_ref, br_ref, h_ref, comb_ref):
    h = _rms(x_ref[...], g_ref[...])
    h_ref[...] = h.astype(BF16)
    logits = jnp.dot(h, wr_ref[...], preferred_element_type=F32, precision=lax.Precision.HIGHEST) + br_ref[...]
    lane = lax.broadcasted_iota(jnp.int32, logits.shape, 1)
    big = jnp.int32(LANES)
    neg = -jnp.inf
    glog = jnp.where((lane >= N_EXPERTS) & (lane < N_EXPERTS + N_GROUPS), logits, neg)
    gmax = jnp.max(glog, axis=-1, keepdims=True)
    g_p = 1.0 / jnp.sum(jnp.exp(glog - gmax), axis=-1, keepdims=True)
    g_i = jnp.min(jnp.where(glog == gmax, lane - N_EXPERTS, big), axis=-1, keepdims=True)
    elog = jnp.where((lane < N_EXPERTS) & ((lane // EXPERTS_PER_GROUP) == g_i), logits, neg)
    m1 = jnp.max(elog, axis=-1, keepdims=True)
    zsum = jnp.sum(jnp.exp(elog - m1), axis=-1, keepdims=True)
    i1 = jnp.min(jnp.where(elog == m1, lane, big), axis=-1, keepdims=True)
    elog2 = jnp.where(lane == i1, neg, elog)
    m2 = jnp.max(elog2, axis=-1, keepdims=True)
    i2 = jnp.min(jnp.where(elog2 == m2, lane, big), axis=-1, keepdims=True)
    p1 = 1.0 / zsum
    p2 = jnp.exp(m2 - m1) / zsum
    gate1 = g_p * p1 / (p1 + p2)
    gate2 = g_p * p2 / (p1 + p2)
    comb_ref[...] = jnp.where(lane == i1, gate1, 0.0) + jnp.where(lane == i2, gate2, 0.0)


def moe_router(x, gain, w_router_pad, b_router_pad, tm):
    t = x.shape[0]
    return pl.pallas_call(
        _router_kernel,
        grid=(t // tm,),
        in_specs=[pl.BlockSpec((tm, D_MODEL), lambda i: (i, 0)),
                  _const_spec((1, D_MODEL)), _const_spec((D_MODEL, LANES)), _const_spec((1, LANES))],
        out_specs=[pl.BlockSpec((tm, D_MODEL), lambda i: (i, 0)),
                   pl.BlockSpec((tm, LANES), lambda i: (i, 0))],
        out_shape=[jax.ShapeDtypeStruct((t, D_MODEL), BF16),
                   jax.ShapeDtypeStruct((t, LANES), F32)],
        compiler_params=_cparams(("parallel",)),
        name="moe_router",
    )(x, gain.reshape(1, D_MODEL), w_router_pad, b_router_pad)


def _experts_kernel(x_ref, h_ref, comb_ref, wg_ref, wu_ref, wd_ref, o_ref, acc_ref):
    e = pl.program_id(1)

    @pl.when(e == 0)
    def _():
        acc_ref[...] = jnp.zeros_like(acc_ref)

    h = h_ref[...]
    comb = comb_ref[...]
    lane = lax.broadcasted_iota(jnp.int32, comb.shape, 1)
    c_e = jnp.sum(jnp.where(lane == e, comb, 0.0), axis=-1, keepdims=True)
    gate = jnp.dot(h, wg_ref[...], preferred_element_type=F32)
    up = jnp.dot(h, wu_ref[...], preferred_element_type=F32)
    act = (_silu(gate) * up).astype(BF16)
    acc_ref[...] += c_e * jnp.dot(act, wd_ref[...], preferred_element_type=F32)

    @pl.when(e == pl.num_programs(1) - 1)
    def _():
        o_ref[...] = x_ref[...] + acc_ref[...]


def moe_experts(x, h, comb, wg, wu, wd, tm):
    t = x.shape[0]
    tok = lambda i, e: (i, 0)
    return pl.pallas_call(
        _experts_kernel,
        grid=(t // tm, N_EXPERTS),
        in_specs=[pl.BlockSpec((tm, D_MODEL), tok), pl.BlockSpec((tm, D_MODEL), tok),
                  pl.BlockSpec((tm, LANES), tok),
                  pl.BlockSpec((None, D_MODEL, EXPERT_FF), lambda i, e: (e, 0, 0)),
                  pl.BlockSpec((None, D_MODEL, EXPERT_FF), lambda i, e: (e, 0, 0)),
                  pl.BlockSpec((None, EXPERT_FF, D_MODEL), lambda i, e: (e, 0, 0))],
        out_specs=pl.BlockSpec((tm, D_MODEL), tok),
        out_shape=jax.ShapeDtypeStruct((t, D_MODEL), F32),
        scratch_shapes=[pltpu.VMEM((tm, D_MODEL), F32)],
        compiler_params=_cparams(("parallel", "arbitrary")),
        name="moe_experts",
    )(x, h, comb, wg, wu, wd)


def _pad_cols(w, n_pad):
    return jnp.pad(w, ((0, 0), (0, n_pad - w.shape[1])))


def _pad_seq(u, rows):
    return jnp.pad(u[:, None, :], ((0, 0), (0, rows - 1), (0, 0)))


def _conv_tail(buf):
    return jnp.pad(buf, ((0, 0), (SUBLANES - (CONV_WIDTH - 1), 0), (0, 0)))


EVEN_SEGS = ((0, 512), (512, 640), (640, 768), (768, 1280), (1280, 2304), (2304, 2432))
ODD_SEGS = ((0, 3072), (3072, 4096), (4096, 4224))


def _trunk(x_seq, p, states, prompt):
    bsz, length = x_seq.shape[0], x_seq.shape[1]
    t = bsz * length
    tm = 512 if t % 512 == 0 else t
    tm_moe = 1024 if t % 1024 == 0 else t
    x = x_seq.reshape(t, D_MODEL)
    k_win, v_win, ssm_h, ssm_conv, dn_s, dn_conv = states

    q, k, v, z, xbc, dt = prenorm_proj(x, p['ln_mix'][0], p['w_in_even'], EVEN_SEGS, tm)
    if prompt:
        seq = lambda u: u.reshape(bsz, length, u.shape[-1])
        q3, k3, v3, z3, xbc3, dt3 = seq(q), seq(k), seq(v), seq(z), seq(xbc), seq(dt)
        att, k_normed = swa_attention(q3, k3, v3, k3, v3, p['q_norm'], p['k_norm'], p['attn_sinks'], prompt=True)
        new_k = k_normed[:, -WINDOW:].reshape(bsz, WINDOW, SWA_KV_HEADS, HEAD_DIM)
        new_v = v3[:, -WINDOW:].reshape(bsz, WINDOW, SWA_KV_HEADS, HEAD_DIM)
        new_ssm_conv = xbc3[:, -(CONV_WIDTH - 1):]
        n_valid_ssm, n_valid_dn = SSM_CHUNK, DN_CHUNK
    else:
        q3, k3, v3 = _pad_seq(q, WINDOW), _pad_seq(k, WINDOW), _pad_seq(v, WINDOW)
        z3, xbc3, dt3 = _pad_seq(z, SSM_CHUNK), _pad_seq(xbc, SSM_CHUNK), _pad_seq(dt, SSM_CHUNK)
        kb = k_win.reshape(bsz, WINDOW, SWA_KV)
        vb = v_win.reshape(bsz, WINDOW, SWA_KV)
        att, k_normed = swa_attention(q3, k3, v3, kb, vb, p['q_norm'], p['k_norm'], p['attn_sinks'], prompt=False)
        att = att[:, :1]
        new_k = jnp.concatenate([kb[:, 1:], k_normed[:, :1]], axis=1).reshape(bsz, WINDOW, SWA_KV_HEADS, HEAD_DIM)
        new_v = jnp.concatenate([vb[:, 1:], v3[:, :1]], axis=1).reshape(bsz, WINDOW, SWA_KV_HEADS, HEAD_DIM)
        new_ssm_conv = jnp.concatenate([ssm_conv[:, 1:], xbc[:, None, :]], axis=1)
        n_valid_ssm, n_valid_dn = 1, 1
    y_ssm, new_h = ssd_mixer(xbc3, z3, dt3, p['ssm_conv_w'], p['ssm_conv_b'], p['ssm_dt_bias'], p['ssm_A_log'],
                             p['ssm_D'], p['ssm_norm'], _conv_tail(ssm_conv),
                             ssm_h.reshape(bsz, SSM_INNER, SSM_STATE), n_valid_ssm)
    if not prompt:
        y_ssm = y_ssm[:, :1]
    x = resid_proj(x, [att.reshape(t, SWA_Q), y_ssm.reshape(t, SSM_INNER)],
                   [p['w_out_even'][:SWA_Q], p['w_out_even'][SWA_Q:]], tm)
    h_ffn, comb = moe_router(x, p['ln_ffn'][0], p['w_route'][0], p['b_route'][0], tm)
    x = moe_experts(x, h_ffn, comb, p['moe_w_gate'][0], p['moe_w_up'][0], p['moe_w_down'][0], tm_moe)

    qkv, zz, ba = prenorm_proj(x, p['ln_mix'][1], p['w_in_odd'], ODD_SEGS, tm // 2 if tm >= 512 else tm)
    if prompt:
        qkv3 = qkv.reshape(bsz, length, DN_CONV_CH)
        zz3 = zz.reshape(bsz, length, DN_VW)
        ba3 = ba.reshape(bsz, length, LANES)
        new_dn_conv = qkv3[:, -(CONV_WIDTH - 1):]
    else:
        qkv3, zz3, ba3 = _pad_seq(qkv, DN_CHUNK), _pad_seq(zz, DN_CHUNK), _pad_seq(ba, DN_CHUNK)
        new_dn_conv = jnp.concatenate([dn_conv[:, 1:], qkv[:, None, :]], axis=1)
    o_dn, new_s = dn_mixer(qkv3, zz3, ba3, p['dn_conv_w'], p['dn_dt_bias'], p['dn_A_log'], p['dn_norm'],
                           _conv_tail(dn_conv), dn_s.reshape(bsz, DN_QK, DN_DV), n_valid_dn)
    if not prompt:
        o_dn = o_dn[:, :1]
    x = resid_proj(x, [o_dn.reshape(t, DN_VW)], [p['w_out_odd']], tm)
    h_ffn, comb = moe_router(x, p['ln_ffn'][1], p['w_route'][1], p['b_route'][1], tm)
    x = moe_experts(x, h_ffn, comb, p['moe_w_gate'][1], p['moe_w_up'][1], p['moe_w_down'][1], tm_moe)

    return (x.reshape(bsz, length, D_MODEL), new_k[None], new_v[None],
            new_h.reshape(1, bsz, SSM_HEADS, SSM_HEAD_DIM, SSM_STATE), new_ssm_conv[None],
            new_s.reshape(1, bsz, DN_HEADS, DN_DK, DN_DV), new_dn_conv[None])


def kernel(x_prompt, x_sample, cache_k_win, cache_v_win, state_ssm, state_ssm_conv, state_dn, state_dn_conv,
           ln_mix, ln_ffn, w_in_even, q_norm, k_norm, attn_sinks, ssm_conv_w, ssm_conv_b, ssm_dt_bias,
           ssm_A_log, ssm_D, ssm_norm, w_out_even, w_in_odd, dn_conv_w, dn_dt_bias, dn_A_log, dn_norm,
           w_out_odd, moe_w_group, moe_b_group, moe_w_router, moe_b_router, moe_w_gate, moe_w_up, moe_w_down):
    w_route = _pad_cols(jnp.concatenate([moe_w_router, moe_w_group], axis=-1).reshape(-1, N_EXPERTS + N_GROUPS),
                        LANES).reshape(2, D_MODEL, LANES)
    b_route = _pad_cols(jnp.concatenate([moe_b_router, moe_b_group], axis=-1), LANES).reshape(2, 1, LANES)
    p = {
        'ln_mix': ln_mix, 'ln_ffn': ln_ffn,
        'w_in_even': _pad_cols(w_in_even[0], EVEN_SEGS[-1][1]).astype(BF16),
        'q_norm': q_norm[0], 'k_norm': k_norm[0], 'attn_sinks': attn_sinks[0],
        'ssm_conv_w': ssm_conv_w[0], 'ssm_conv_b': ssm_conv_b[0], 'ssm_dt_bias': ssm_dt_bias[0],
        'ssm_A_log': ssm_A_log[0], 'ssm_D': ssm_D[0], 'ssm_norm': ssm_norm[0],
        'w_out_even': w_out_even[0].astype(BF16),
        'w_in_odd': _pad_cols(w_in_odd[0], ODD_SEGS[-1][1]).astype(BF16),
        'dn_conv_w': dn_conv_w[0], 'dn_dt_bias': dn_dt_bias[0], 'dn_A_log': dn_A_log[0], 'dn_norm': dn_norm[0],
        'w_out_odd': w_out_odd[0].astype(BF16),
        'w_route': w_route, 'b_route': b_route,
        'moe_w_gate': moe_w_gate.astype(BF16), 'moe_w_up': moe_w_up.astype(BF16),
        'moe_w_down': moe_w_down.astype(BF16),
    }
    bp = x_prompt.shape[0]
    zeros = lambda *s: jnp.zeros(s, F32)
    prompt_states = (None, None, zeros(bp, SSM_INNER, SSM_STATE), zeros(bp, CONV_WIDTH - 1, SSM_CONV_CH),
                     zeros(bp, DN_QK, DN_DV), zeros(bp, CONV_WIDTH - 1, DN_CONV_CH))
    y_p, kp, vp, sp, scp, dnp, dncp = _trunk(x_prompt, p, prompt_states, True)
    sample_states = (cache_k_win[0], cache_v_win[0], state_ssm[0], state_ssm_conv[0], state_dn[0],
                     state_dn_conv[0])
    y_s, ks, vs, ss, scs, dns, dncs = _trunk(x_sample, p, sample_states, False)
    return (y_p, y_s, kp, ks, vp, vs, sp, ss, scp, scs, dnp, dns, dncp, dncs)
```

```python
import functools

import jax
import jax.numpy as jnp
from jax import lax
from jax.experimental import pallas as pl
from jax.experimental.pallas import tpu as pltpu

F32 = jnp.float32
BF16 = jnp.bfloat16
EPS = 1e-6

D_MODEL = 1024
SWA_HEADS = 8
SWA_KV_HEADS = 2
SWA_GROUP = SWA_HEADS // SWA_KV_HEADS
HEAD_DIM = 64
WINDOW = 128
SWA_Q = SWA_HEADS * HEAD_DIM
SWA_KV = SWA_KV_HEADS * HEAD_DIM
SSM_HEADS = 8
SSM_HEAD_DIM = 64
SSM_GROUPS = 2
SSM_STATE = 128
SSM_INNER = SSM_HEADS * SSM_HEAD_DIM
SSM_CHUNK = 128
SSM_CONV_CH = SSM_INNER + 2 * SSM_GROUPS * SSM_STATE
CONV_WIDTH = 4
DN_HEADS = 8
DN_DK = 128
DN_DV = 128
DN_CHUNK = 64
DN_QK = DN_HEADS * DN_DK
DN_VW = DN_HEADS * DN_DV
DN_CONV_CH = 2 * DN_QK + DN_VW
N_GROUPS = 4
EXPERTS_PER_GROUP = 8
N_EXPERTS = N_GROUPS * EXPERTS_PER_GROUP
EXPERT_FF = 256

LANES = 128
SUBLANES = 8
VMEM_LIMIT = 56 * 1024 * 1024

DN_SEQS_PER_STEP = 2

NT_DIMS = (((1,), (1,)), ((), ()))
TN_DIMS = (((0,), (0,)), ((), ()))


def _cparams(sem):
    return pltpu.CompilerParams(dimension_semantics=sem, vmem_limit_bytes=VMEM_LIMIT)


def _const_spec(shape):
    nd = len(shape)
    return pl.BlockSpec(shape, lambda *_: (0,) * nd)


def _sigmoid(x):
    return 1.0 / (1.0 + jnp.exp(-x))


def _silu(x):
    return x * _sigmoid(x)


def _softplus(x):
    return jnp.maximum(x, 0.0) + jnp.log(1.0 + jnp.exp(-jnp.abs(x)))


def _rms(x, gain):
    return x * lax.rsqrt(jnp.mean(x * x, axis=-1, keepdims=True) + EPS) * gain


def _prenorm_proj_kernel(x_ref, g_ref, w_ref, *out_refs, segs):
    h = _rms(x_ref[...], g_ref[...]).astype(BF16)
    for o_ref, (a, b) in zip(out_refs, segs):
        o_ref[...] = jnp.dot(h, w_ref[:, a:b], preferred_element_type=F32)


def prenorm_proj(x, gain, w_bf16, segs, tm):
    t = x.shape[0]
    n_pad = w_bf16.shape[1]
    return pl.pallas_call(
        functools.partial(_prenorm_proj_kernel, segs=segs),
        grid=(t // tm,),
        in_specs=[pl.BlockSpec((tm, D_MODEL), lambda i: (i, 0)),
                  _const_spec((1, D_MODEL)),
                  _const_spec((D_MODEL, n_pad))],
        out_specs=[pl.BlockSpec((tm, b - a), lambda i: (i, 0)) for a, b in segs],
        out_shape=[jax.ShapeDtypeStruct((t, b - a), F32) for a, b in segs],
        compiler_params=_cparams(("parallel",)),
        name="prenorm_proj",
    )(x, gain.reshape(1, D_MODEL), w_bf16)


def _resid_proj_kernel(x_ref, *refs, n_in):
    acc = x_ref[...]
    for a_ref, w_ref in zip(refs[:n_in], refs[n_in:2 * n_in]):
        acc = acc + jnp.dot(a_ref[...].astype(BF16), w_ref[...], preferred_element_type=F32)
    refs[2 * n_in][...] = acc


def resid_proj(x, acts, ws, tm):
    t = x.shape[0]
    n_in = len(acts)
    return pl.pallas_call(
        functools.partial(_resid_proj_kernel, n_in=n_in),
        grid=(t // tm,),
        in_specs=([pl.BlockSpec((tm, D_MODEL), lambda i: (i, 0))]
                  + [pl.BlockSpec((tm, a.shape[1]), lambda i: (i, 0)) for a in acts]
                  + [_const_spec(w.shape) for w in ws]),
        out_specs=pl.BlockSpec((tm, D_MODEL), lambda i: (i, 0)),
        out_shape=jax.ShapeDtypeStruct((t, D_MODEL), F32),
        compiler_params=_cparams(("parallel",)),
        name="resid_proj",
    )(x, *acts, *ws)


def _swa_kernel(sink_ref, q_ref, kc_ref, kp_ref, vc_ref, vp_ref, qn_ref, kn_ref, o_ref, kout_ref, *,
                norm_prev, first_has_prev):
    n = pl.program_id(1)
    blk = q_ref.shape[0]
    q = q_ref[...]
    kc, kp, vc, vp = kc_ref[...], kp_ref[...], vc_ref[...], vp_ref[...]
    qn, kn = qn_ref[...], kn_ref[...]
    row = lax.broadcasted_iota(jnp.int32, (blk, 2 * blk), 0)
    col = lax.broadcasted_iota(jnp.int32, (blk, 2 * blk), 1)
    rel = row + blk - col
    mask = (rel >= 0) & (rel <= WINDOW)
    if not first_has_prev:
        mask = mask & ((n > 0) | (col >= blk))
    outs, kouts = [], []
    for j in range(SWA_KV_HEADS):
        sl = slice(j * HEAD_DIM, (j + 1) * HEAD_DIM)
        kcj = _rms(kc[:, sl], kn)
        kpj = _rms(kp[:, sl], kn) if norm_prev else kp[:, sl]
        kouts.append(kcj)
        kcat = jnp.concatenate([kpj, kcj], axis=0).astype(BF16)
        vcat = jnp.concatenate([vp[:, sl], vc[:, sl]], axis=0).astype(BF16)
        for g in range(SWA_GROUP):
            h = j * SWA_GROUP + g
            qh = _rms(q[:, h * HEAD_DIM:(h + 1) * HEAD_DIM], qn) * (HEAD_DIM ** -0.5)
            s = lax.dot_general(qh.astype(BF16), kcat, NT_DIMS, preferred_element_type=F32)
            s = jnp.where(mask, s, -jnp.inf)
            sink = sink_ref[h]
            m = jnp.maximum(jnp.max(s, axis=-1, keepdims=True), sink)
            p = jnp.exp(s - m)
            p = p / (jnp.sum(p, axis=-1, keepdims=True) + jnp.exp(sink - m))
            outs.append(jnp.dot(p.astype(BF16), vcat, preferred_element_type=F32))
    o_ref[...] = jnp.concatenate(outs, axis=1)
    kout_ref[...] = jnp.concatenate(kouts, axis=1)


def swa_attention(q, k, v, k_prev, v_prev, q_norm, k_norm, sinks, *, prompt):
    bsz, length = q.shape[0], q.shape[1]
    nb = length // WINDOW
    if prompt:
        prev_map = lambda b, n: (b, jnp.maximum(n - 1, 0), 0)
    else:
        prev_map = lambda b, n: (b, 0, 0)
    cur = lambda b, n: (b, n, 0)
    kv_blk = (None, WINDOW, SWA_KV)
    return pl.pallas_call(
        functools.partial(_swa_kernel, norm_prev=prompt, first_has_prev=not prompt),
        grid=(bsz, nb),
        in_specs=[pl.BlockSpec(memory_space=pltpu.SMEM),
                  pl.BlockSpec((None, WINDOW, SWA_Q), cur),
                  pl.BlockSpec(kv_blk, cur), pl.BlockSpec(kv_blk, prev_map),
                  pl.BlockSpec(kv_blk, cur), pl.BlockSpec(kv_blk, prev_map),
                  _const_spec((1, HEAD_DIM)), _const_spec((1, HEAD_DIM))],
        out_specs=[pl.BlockSpec((None, WINDOW, SWA_Q), cur), pl.BlockSpec(kv_blk, cur)],
        out_shape=[jax.ShapeDtypeStruct((bsz, length, SWA_Q), F32),
                   jax.ShapeDtypeStruct((bsz, length, SWA_KV), F32)],
        compiler_params=_cparams(("parallel", "arbitrary")),
        name="swa_attention",
    )(sinks, q, k, k_prev, v, v_prev, q_norm.reshape(1, HEAD_DIM), k_norm.reshape(1, HEAD_DIM))


def _chunk_conv(x_ref, xx_scr, w_ref, rows):
    xx_scr[SUBLANES:SUBLANES + rows, :] = x_ref[...]
    acc = None
    for tap in range(CONV_WIDTH):
        off = SUBLANES - (CONV_WIDTH - 1) + tap
        term = w_ref[tap:tap + 1, :] * xx_scr[off:off + rows, :]
        acc = term if acc is None else acc + term
    return acc


def _carry_conv_tail(xx_scr, rows):
    xx_scr[0:SUBLANES, :] = xx_scr[rows:rows + SUBLANES, :]


def _ssd_kernel(xbc_ref, z_ref, dt_ref, cw_ref, cb_ref, dtb_ref, alog_ref, dd_ref, nrm_ref, tail0_ref, h0_ref,
                y_ref, hout_ref, xx_scr, h_scr, *, n_valid):
    q_len = xbc_ref.shape[0]

    @pl.when(pl.program_id(1) == 0)
    def _():
        xx_scr[0:SUBLANES, :] = tail0_ref[...]
        h_scr[...] = h0_ref[...]

    act = _silu(_chunk_conv(xbc_ref, xx_scr, cw_ref, q_len) + cb_ref[...])
    _carry_conv_tail(xx_scr, q_len)
    xs = act[:, :SSM_INNER]
    bm = act[:, SSM_INNER:SSM_INNER + SSM_GROUPS * SSM_STATE]
    cm = act[:, SSM_INNER + SSM_GROUPS * SSM_STATE:]

    row = lax.broadcasted_iota(jnp.int32, (q_len, q_len), 0)
    col = lax.broadcasted_iota(jnp.int32, (q_len, q_len), 1)
    causal = row >= col
    dt = _softplus(dt_ref[...] + dtb_ref[...])
    if n_valid < q_len:
        dt = jnp.where(lax.broadcasted_iota(jnp.int32, dt.shape, 0) < n_valid, dt, 0.0)
    da = dt * (-jnp.exp(alog_ref[...]))
    cum = jnp.dot(causal.astype(F32), da, preferred_element_type=F32, precision=lax.Precision.HIGHEST)
    cum_t = cum.T
    dt_t = dt.T
    e_cum = jnp.exp(cum)
    hpg = SSM_HEADS // SSM_GROUPS
    gw = hpg * SSM_HEAD_DIM
    ys = []
    for g in range(SSM_GROUPS):
        bm_g = bm[:, g * SSM_STATE:(g + 1) * SSM_STATE].astype(BF16)
        cm_g = cm[:, g * SSM_STATE:(g + 1) * SSM_STATE].astype(BF16)
        cb = lax.dot_general(cm_g, bm_g, NT_DIMS, preferred_element_type=F32)
        h_g = h_scr[g * gw:(g + 1) * gw, :]
        y_state = lax.dot_general(cm_g, h_g.astype(BF16), NT_DIMS, preferred_element_type=F32)
        xt_parts, dec_parts = [], []
        for hh in range(hpg):
            h = g * hpg + hh
            x_h = xs[:, h * SSM_HEAD_DIM:(h + 1) * SSM_HEAD_DIM]
            cum_c = cum[:, h:h + 1]
            seg = jnp.exp(jnp.where(causal, cum_c - cum_t[h:h + 1, :], -jnp.inf))
            wgt = cb * seg * dt_t[h:h + 1, :]
            y = jnp.dot(wgt.astype(BF16), x_h.astype(BF16), preferred_element_type=F32)
            y = y + y_state[:, hh * SSM_HEAD_DIM:(hh + 1) * SSM_HEAD_DIM] * e_cum[:, h:h + 1]
            ys.append(y + dd_ref[0, h] * x_h)
            c_last = cum[q_len - 1:q_len, h:h + 1]
            xt_parts.append(x_h * (jnp.exp(c_last - cum_c) * dt[:, h:h + 1]))
            dec_parts.append(jnp.broadcast_to(jnp.exp(c_last), (SSM_HEAD_DIM, SSM_STATE)))
        xt = jnp.concatenate(xt_parts, axis=1).astype(BF16)
        upd = lax.dot_general(xt, bm_g, TN_DIMS, preferred_element_type=F32)
        h_scr[g * gw:(g + 1) * gw, :] = h_g * jnp.concatenate(dec_parts, axis=0) + upd
    y_all = jnp.concatenate(ys, axis=1) * _silu(z_ref[...])
    nrm = nrm_ref[...]
    y_ref[...] = jnp.concatenate(
        [_rms(y_all[:, g * gw:(g + 1) * gw], nrm[:, g * gw:(g + 1) * gw]) for g in range(SSM_GROUPS)], axis=1)
    hout_ref[...] = h_scr[...]


def ssd_mixer(xbc, z, dt, conv_w, conv_b, dt_bias, a_log, d_skip, norm_g, tail0, h0, n_valid):
    bsz, length = xbc.shape[0], xbc.shape[1]
    nc = length // SSM_CHUNK
    cur = lambda b, c: (b, c, 0)
    per_b = lambda b, c: (b, 0, 0)
    pad8 = lambda v: jnp.pad(v.reshape(1, SSM_HEADS), ((0, 0), (0, LANES - SSM_HEADS)))
    return pl.pallas_call(
        functools.partial(_ssd_kernel, n_valid=n_valid),
        grid=(bsz, nc),
        in_specs=[pl.BlockSpec((None, SSM_CHUNK, SSM_CONV_CH), cur),
                  pl.BlockSpec((None, SSM_CHUNK, SSM_INNER), cur),
                  pl.BlockSpec((None, SSM_CHUNK, LANES), cur),
                  _const_spec((CONV_WIDTH, SSM_CONV_CH)), _const_spec((1, SSM_CONV_CH)),
                  _const_spec((1, LANES)), _const_spec((1, LANES)),
                  pl.BlockSpec(memory_space=pltpu.SMEM),
                  _const_spec((1, SSM_INNER)),
                  pl.BlockSpec((None, SUBLANES, SSM_CONV_CH), per_b),
                  pl.BlockSpec((None, SSM_INNER, SSM_STATE), per_b)],
        out_specs=[pl.BlockSpec((None, SSM_CHUNK, SSM_INNER), cur),
                   pl.BlockSpec((None, SSM_INNER, SSM_STATE), per_b)],
        out_shape=[jax.ShapeDtypeStruct((bsz, length, SSM_INNER), F32),
                   jax.ShapeDtypeStruct((bsz, SSM_INNER, SSM_STATE), F32)],
        scratch_shapes=[pltpu.VMEM((SSM_CHUNK + SUBLANES, SSM_CONV_CH), F32),
                        pltpu.VMEM((SSM_INNER, SSM_STATE), F32)],
        compiler_params=_cparams(("parallel", "arbitrary")),
        name="ssd_mixer",
    )(xbc, z, dt, conv_w, conv_b.reshape(1, SSM_CONV_CH), pad8(dt_bias), pad8(a_log),
      d_skip.reshape(1, SSM_HEADS), norm_g.reshape(1, SSM_INNER), tail0, h0)


def _unit_lower_inverses(lmats, row, col):
    c = lmats[0].shape[0]
    mm = lambda a, b: jnp.dot(a.astype(BF16), b.astype(BF16), preferred_element_type=F32)
    eye = (row == col).astype(F32)
    blk = SUBLANES
    same = (row // blk) == (col // blk)
    xs = [jnp.where(same, -l, 0.0) for l in lmats]
    invs = [eye + x for x in xs]
    p = blk
    while p > 2:
        xs = [mm(x, x) for x in xs]
        invs = [i + mm(i, x) for i, x in zip(invs, xs)]
        p //= 2
    while blk < c:
        outer = ((row // (2 * blk)) == (col // (2 * blk))) & ((row // blk) != (col // blk))
        ts = [mm(i, jnp.where(outer, l, 0.0)) for i, l in zip(invs, lmats)]
        invs = [i - mm(t, i) for i, t in zip(invs, ts)]
        blk *= 2
    return invs


def _dn_kernel(qkv_ref, z_ref, ba_ref, cw_ref, dtb_ref, alog_ref, nrm_ref, tail0_ref, s0_ref,
               o_ref, sout_ref, xx_scr, s_scr, *, n_valid):
    nseq, c_len = qkv_ref.shape[0], qkv_ref.shape[1]
    seqs = range(nseq)
    chains = [(s, h) for s in seqs for h in range(DN_HEADS)]
    ids = range(len(chains))

    @pl.when(pl.program_id(1) == 0)
    def _():
        xx_scr[:, 0:SUBLANES, :] = tail0_ref[...]
        s_scr[...] = s0_ref[...]

    row = lax.broadcasted_iota(jnp.int32, (c_len, c_len), 0)
    col = lax.broadcasted_iota(jnp.int32, (c_len, c_len), 1)
    incl = row >= col
    strict = row > col
    tri = incl.astype(F32)
    nrm = nrm_ref[...]
    bf = lambda t: t.astype(BF16)
    mm = lambda a, b: jnp.dot(bf(a), bf(b), preferred_element_type=F32)

    act, beta, cum, cum_t, e_cum, e_rest, e_last = [], [], [], [], [], [], []
    for s in seqs:
        act.append(_silu(_chunk_conv(qkv_ref.at[s], xx_scr.at[s], cw_ref, c_len)))
        _carry_conv_tail(xx_scr.at[s], c_len)
        ba = ba_ref[s]
        beta_s = _sigmoid(ba)
        gate = -jnp.exp(alog_ref[...]) * _softplus(ba + dtb_ref[...])
        if n_valid < c_len:
            valid = lax.broadcasted_iota(jnp.int32, ba.shape, 0) < n_valid
            beta_s = jnp.where(valid, beta_s, 0.0)
            gate = jnp.where(valid, gate, 0.0)
        beta.append(beta_s)
        cum_s = jnp.dot(tri, gate, preferred_element_type=F32, precision=lax.Precision.HIGHEST)
        c_last = cum_s[c_len - 1:c_len, :]
        cum.append(cum_s)
        cum_t.append(cum_s.T)
        e_cum.append(jnp.exp(cum_s))
        e_rest.append(jnp.exp(c_last - cum_s))
        e_last.append(jnp.exp(c_last))

    q, k, kb, rhs, decay = [], [], [], [], []
    for i in ids:
        s, h = chains[i]
        gl = DN_HEADS + h
        q_h = act[s][:, h * DN_DK:(h + 1) * DN_DK]
        k_h = act[s][:, DN_QK + h * DN_DK:DN_QK + (h + 1) * DN_DK]
        v_h = act[s][:, 2 * DN_QK + h * DN_DV:2 * DN_QK + (h + 1) * DN_DV]
        q.append(q_h * lax.rsqrt(jnp.sum(q_h * q_h, axis=-1, keepdims=True) + EPS) * (DN_DK ** -0.5))
        k.append(k_h * lax.rsqrt(jnp.sum(k_h * k_h, axis=-1, keepdims=True) + EPS))
        beta_c = beta[s][:, h:h + 1]
        kb.append(k[i] * beta_c)
        rhs.append(jnp.concatenate([v_h * beta_c, kb[i] * e_cum[s][:, gl:gl + 1]], axis=1))
        decay.append(jnp.exp(jnp.where(incl, cum[s][:, gl:gl + 1] - cum_t[s][gl:gl + 1, :], -jnp.inf)))
    kq = [lax.dot_general(bf(jnp.concatenate([kb[i], q[i]], axis=0)), bf(k[i]), NT_DIMS,
                          preferred_element_type=F32) for i in ids]
    lmat = [jnp.where(strict, kq[i][:c_len] * decay[i], 0.0) for i in ids]
    attn = [kq[i][c_len:] * decay[i] for i in ids]
    tinv = _unit_lower_inverses(lmat, row, col)
    uw = [mm(tinv[i], rhs[i]) for i in ids]
    s_old = [s_scr[s, h * DN_DK:(h + 1) * DN_DK, :] for s, h in chains]
    qe = [q[i] * e_cum[s][:, DN_HEADS + h:DN_HEADS + h + 1] for i, (s, h) in enumerate(chains)]
    wq = [mm(jnp.concatenate([uw[i][:, DN_DV:], qe[i]], axis=0), s_old[i]) for i in ids]
    v_new = [uw[i][:, :DN_DV] - wq[i][:c_len] for i in ids]
    o = [wq[i][c_len:] + mm(attn[i], v_new[i]) for i in ids]
    for i in ids:
        s, h = chains[i]
        gl = DN_HEADS + h
        s_scr[s, h * DN_DK:(h + 1) * DN_DK, :] = s_old[i] * e_last[s][:, gl:gl + 1] + lax.dot_general(
            bf(k[i] * e_rest[s][:, gl:gl + 1]), bf(v_new[i]), TN_DIMS, preferred_element_type=F32)
    for s in seqs:
        z = z_ref[s]
        o_ref[s] = jnp.concatenate(
            [_rms(o[s * DN_HEADS + h], nrm) * _silu(z[:, h * DN_DV:(h + 1) * DN_DV]) for h in range(DN_HEADS)],
            axis=1)
    sout_ref[...] = s_scr[...]


def dn_mixer(qkv, z, ba, conv_w, dt_bias, a_log, norm_g, tail0, s0, n_valid, nseq):
    bsz, length = qkv.shape[0], qkv.shape[1]
    nc = length // DN_CHUNK
    cur = lambda b, c: (b, c, 0)
    per_b = lambda b, c: (b, 0, 0)
    pad_a = lambda v: jnp.pad(v.reshape(1, DN_HEADS), ((0, 0), (DN_HEADS, LANES - 2 * DN_HEADS)))
    return pl.pallas_call(
        functools.partial(_dn_kernel, n_valid=n_valid),
        grid=(bsz // nseq, nc),
        in_specs=[pl.BlockSpec((nseq, DN_CHUNK, DN_CONV_CH), cur),
                  pl.BlockSpec((nseq, DN_CHUNK, DN_VW), cur),
                  pl.BlockSpec((nseq, DN_CHUNK, LANES), cur),
                  _const_spec((CONV_WIDTH, DN_CONV_CH)),
                  _const_spec((1, LANES)), _const_spec((1, LANES)), _const_spec((1, DN_DV)),
                  pl.BlockSpec((nseq, SUBLANES, DN_CONV_CH), per_b),
                  pl.BlockSpec((nseq, DN_QK, DN_DV), per_b)],
        out_specs=[pl.BlockSpec((nseq, DN_CHUNK, DN_VW), cur),
                   pl.BlockSpec((nseq, DN_QK, DN_DV), per_b)],
        out_shape=[jax.ShapeDtypeStruct((bsz, length, DN_VW), F32),
                   jax.ShapeDtypeStruct((bsz, DN_QK, DN_DV), F32)],
        scratch_shapes=[pltpu.VMEM((nseq, DN_CHUNK + SUBLANES, DN_CONV_CH), F32),
                        pltpu.VMEM((nseq, DN_QK, DN_DV), F32)],
        compiler_params=_cparams(("parallel", "arbitrary")),
        name="dn_mixer",
    )(qkv, z, ba, conv_w, pad_a(dt_bias), pad_a(a_log), norm_g.reshape(1, DN_DV), tail0, s0)


def _router_kernel(x_ref, g_ref, wr_ref, br_ref, h_ref, comb_ref):
    h = _rms(x_ref[...], g_ref[...])
    h_ref[...] = h.astype(BF16)
    logits = jnp.dot(h, wr_ref[...], preferred_element_type=F32, precision=lax.Precision.HIGHEST) + br_ref[...]
    lane = lax.broadcasted_iota(jnp.int32, logits.shape, 1)
    big = jnp.int32(LANES)
    neg = -jnp.inf
    glog = jnp.where((lane >= N_EXPERTS) & (lane < N_EXPERTS + N_GROUPS), logits, neg)
    gmax = jnp.max(glog, axis=-1, keepdims=True)
    g_p = 1.0 / jnp.sum(jnp.exp(glog - gmax), axis=-1, keepdims=True)
    g_i = jnp.min(jnp.where(glog == gmax, lane - N_EXPERTS, big), axis=-1, keepdims=True)
    elog = jnp.where((lane < N_EXPERTS) & ((lane // EXPERTS_PER_GROUP) == g_i), logits, neg)
    m1 = jnp.max(elog, axis=-1, keepdims=True)
    zsum = jnp.sum(jnp.exp(elog - m1), axis=-1, keepdims=True)
    i1 = jnp.min(jnp.where(elog == m1, lane, big), axis=-1, keepdims=True)
    elog2 = jnp.where(lane == i1, neg, elog)
    m2 = jnp.max(elog2, axis=-1, keepdims=True)
    i2 = jnp.min(jnp.where(elog2 == m2, lane, big), axis=-1, keepdims=True)
    p1 = 1.0 / zsum
    p2 = jnp.exp(m2 - m1) / zsum
    gate1 = g_p * p1 / (p1 + p2)
    gate2 = g_p * p2 / (p1 + p2)
    comb_ref[...] = jnp.where(lane == i1, gate1, 0.0) + jnp.where(lane == i2, gate2, 0.0)


def moe_router(x, gain, w_router_pad, b_router_pad, tm):
    t = x.shape[0]
    return pl.pallas_call(
        _router_kernel,
        grid=(t // tm,),
        in_specs=[pl.BlockSpec((tm, D_MODEL), lambda i: (i, 0)),
                  _const_spec((1, D_MODEL)), _const_spec((D_MODEL, LANES)), _const_spec((1, LANES))],
        out_specs=[pl.BlockSpec((tm, D_MODEL), lambda i: (i, 0)),
                   pl.BlockSpec((tm, LANES), lambda i: (i, 0))],
        out_shape=[jax.ShapeDtypeStruct((t, D_MODEL), BF16),
                   jax.ShapeDtypeStruct((t, LANES), F32)],
        compiler_params=_cparams(("parallel",)),
        name="moe_router",
    )(x, gain.reshape(1, D_MODEL), w_router_pad, b_router_pad)


def _experts_kernel(x_ref, h_ref, comb_ref, wg_ref, wu_ref, wd_ref, o_ref, acc_ref):
    e = pl.program_id(1)

    @pl.when(e == 0)
    def _():
        acc_ref[...] = jnp.zeros_like(acc_ref)

    h = h_ref[...]
    comb = comb_ref[...]
    lane = lax.broadcasted_iota(jnp.int32, comb.shape, 1)
    c_e = jnp.sum(jnp.where(lane == e, comb, 0.0), axis=-1, keepdims=True)
    gate = jnp.dot(h, wg_ref[...], preferred_element_type=F32)
    up = jnp.dot(h, wu_ref[...], preferred_element_type=F32)
    act = (_silu(gate) * up).astype(BF16)
    acc_ref[...] += c_e * jnp.dot(act, wd_ref[...], preferred_element_type=F32)

    @pl.when(e == pl.num_programs(1) - 1)
    def _():
        o_ref[...] = x_ref[...] + acc_ref[...]


def moe_experts(x, h, comb, wg, wu, wd, tm):
    t = x.shape[0]
    tok = lambda i, e: (i, 0)
    return pl.pallas_call(
        _experts_kernel,
        grid=(t // tm, N_EXPERTS),
        in_specs=[pl.BlockSpec((tm, D_MODEL), tok), pl.BlockSpec((tm, D_MODEL), tok),
                  pl.BlockSpec((tm, LANES), tok),
                  pl.BlockSpec((None, D_MODEL, EXPERT_FF), lambda i, e: (e, 0, 0)),
                  pl.BlockSpec((None, D_MODEL, EXPERT_FF), lambda i, e: (e, 0, 0)),
                  pl.BlockSpec((None, EXPERT_FF, D_MODEL), lambda i, e: (e, 0, 0))],
        out_specs=pl.BlockSpec((tm, D_MODEL), tok),
        out_shape=jax.ShapeDtypeStruct((t, D_MODEL), F32),
        scratch_shapes=[pltpu.VMEM((tm, D_MODEL), F32)],
        compiler_params=_cparams(("parallel", "arbitrary")),
        name="moe_experts",
    )(x, h, comb, wg, wu, wd)


def _pad_cols(w, n_pad):
    return jnp.pad(w, ((0, 0), (0, n_pad - w.shape[1])))


def _pad_seq(u, rows):
    return jnp.pad(u[:, None, :], ((0, 0), (0, rows - 1), (0, 0)))


def _conv_tail(buf):
    return jnp.pad(buf, ((0, 0), (SUBLANES - (CONV_WIDTH - 1), 0), (0, 0)))


EVEN_SEGS = ((0, 512), (512, 640), (640, 768), (768, 1280), (1280, 2304), (2304, 2432))
ODD_SEGS = ((0, 3072), (3072, 4096), (4096, 4224))


def _trunk(x_seq, p, states, prompt):
    bsz, length = x_seq.shape[0], x_seq.shape[1]
    t = bsz * length
    tm = 512 if t % 512 == 0 else t
    tm_moe = 1024 if t % 1024 == 0 else t
    x = x_seq.reshape(t, D_MODEL)
    k_win, v_win, ssm_h, ssm_conv, dn_s, dn_conv = states

    q, k, v, z, xbc, dt = prenorm_proj(x, p['ln_mix'][0], p['w_in_even'], EVEN_SEGS, tm)
    if prompt:
        seq = lambda u: u.reshape(bsz, length, u.shape[-1])
        q3, k3, v3, z3, xbc3, dt3 = seq(q), seq(k), seq(v), seq(z), seq(xbc), seq(dt)
        att, k_normed = swa_attention(q3, k3, v3, k3, v3, p['q_norm'], p['k_norm'], p['attn_sinks'], prompt=True)
        new_k = k_normed[:, -WINDOW:].reshape(bsz, WINDOW, SWA_KV_HEADS, HEAD_DIM)
        new_v = v3[:, -WINDOW:].reshape(bsz, WINDOW, SWA_KV_HEADS, HEAD_DIM)
        new_ssm_conv = xbc3[:, -(CONV_WIDTH - 1):]
        n_valid_ssm, n_valid_dn = SSM_CHUNK, DN_CHUNK
    else:
        q3, k3, v3 = _pad_seq(q, WINDOW), _pad_seq(k, WINDOW), _pad_seq(v, WINDOW)
        z3, xbc3, dt3 = _pad_seq(z, SSM_CHUNK), _pad_seq(xbc, SSM_CHUNK), _pad_seq(dt, SSM_CHUNK)
        kb = k_win.reshape(bsz, WINDOW, SWA_KV)
        vb = v_win.reshape(bsz, WINDOW, SWA_KV)
        att, k_normed = swa_attention(q3, k3, v3, kb, vb, p['q_norm'], p['k_norm'], p['attn_sinks'], prompt=False)
        att = att[:, :1]
        new_k = jnp.concatenate([kb[:, 1:], k_normed[:, :1]], axis=1).reshape(bsz, WINDOW, SWA_KV_HEADS, HEAD_DIM)
        new_v = jnp.concatenate([vb[:, 1:], v3[:, :1]], axis=1).reshape(bsz, WINDOW, SWA_KV_HEADS, HEAD_DIM)
        new_ssm_conv = jnp.concatenate([ssm_conv[:, 1:], xbc[:, None, :]], axis=1)
        n_valid_ssm, n_valid_dn = 1, 1
    y_ssm, new_h = ssd_mixer(xbc3, z3, dt3, p['ssm_conv_w'], p['ssm_conv_b'], p['ssm_dt_bias'], p['ssm_A_log'],
                             p['ssm_D'], p['ssm_norm'], _conv_tail(ssm_conv),
                             ssm_h.reshape(bsz, SSM_INNER, SSM_STATE), n_valid_ssm)
    if not prompt:
        y_ssm = y_ssm[:, :1]
    x = resid_proj(x, [att.reshape(t, SWA_Q), y_ssm.reshape(t, SSM_INNER)],
                   [p['w_out_even'][:SWA_Q], p['w_out_even'][SWA_Q:]], tm)
    h_ffn, comb = moe_router(x, p['ln_ffn'][0], p['w_route'][0], p['b_route'][0], tm)
    x = moe_experts(x, h_ffn, comb, p['moe_w_gate'][0], p['moe_w_up'][0], p['moe_w_down'][0], tm_moe)

    qkv, zz, ba = prenorm_proj(x, p['ln_mix'][1], p['w_in_odd'], ODD_SEGS, tm // 2 if tm >= 512 else tm)
    if prompt:
        qkv3 = qkv.reshape(bsz, length, DN_CONV_CH)
        zz3 = zz.reshape(bsz, length, DN_VW)
        ba3 = ba.reshape(bsz, length, LANES)
        new_dn_conv = qkv3[:, -(CONV_WIDTH - 1):]
    else:
        qkv3, zz3, ba3 = _pad_seq(qkv, DN_CHUNK), _pad_seq(zz, DN_CHUNK), _pad_seq(ba, DN_CHUNK)
        new_dn_conv = jnp.concatenate([dn_conv[:, 1:], qkv[:, None, :]], axis=1)
    o_dn, new_s = dn_mixer(qkv3, zz3, ba3, p['dn_conv_w'], p['dn_dt_bias'], p['dn_A_log'], p['dn_norm'],
                           _conv_tail(dn_conv), dn_s.reshape(bsz, DN_QK, DN_DV), n_valid_dn, DN_SEQS_PER_STEP)
    if not prompt:
        o_dn = o_dn[:, :1]
    x = resid_proj(x, [o_dn.reshape(t, DN_VW)], [p['w_out_odd']], tm)
    h_ffn, comb = moe_router(x, p['ln_ffn'][1], p['w_route'][1], p['b_route'][1], tm)
    x = moe_experts(x, h_ffn, comb, p['moe_w_gate'][1], p['moe_w_up'][1], p['moe_w_down'][1], tm_moe)

    return (x.reshape(bsz, length, D_MODEL), new_k[None], new_v[None],
            new_h.reshape(1, bsz, SSM_HEADS, SSM_HEAD_DIM, SSM_STATE), new_ssm_conv[None],
            new_s.reshape(1, bsz, DN_HEADS, DN_DK, DN_DV), new_dn_conv[None])


def kernel(x_prompt, x_sample, cache_k_win, cache_v_win, state_ssm, state_ssm_conv, state_dn, state_dn_conv,
           ln_mix, ln_ffn, w_in_even, q_norm, k_norm, attn_sinks, ssm_conv_w, ssm_conv_b, ssm_dt_bias,
           ssm_A_log, ssm_D, ssm_norm, w_out_even, w_in_odd, dn_conv_w, dn_dt_bias, dn_A_log, dn_norm,
           w_out_odd, moe_w_group, moe_b_group, moe_w_router, moe_b_router, moe_w_gate, moe_w_up, moe_w_down):
    w_route = _pad_cols(jnp.concatenate([moe_w_router, moe_w_group], axis=-1).reshape(-1, N_EXPERTS + N_GROUPS),
                        LANES).reshape(2, D_MODEL, LANES)
    b_route = _pad_cols(jnp.concatenate([moe_b_router, moe_b_group], axis=-1), LANES).reshape(2, 1, LANES)
    p = {
        'ln_mix': ln_mix, 'ln_ffn': ln_ffn,
        'w_in_even': _pad_cols(w_in_even[0], EVEN_SEGS[-1][1]).astype(BF16),
        'q_norm': q_norm[0], 'k_norm': k_norm[0], 'attn_sinks': attn_sinks[0],
        'ssm_conv_w': ssm_conv_w[0], 'ssm_conv_b': ssm_conv_b[0], 'ssm_dt_bias': ssm_dt_bias[0],
        'ssm_A_log': ssm_A_log[0], 'ssm_D': ssm_D[0], 'ssm_norm': ssm_norm[0],
        'w_out_even': w_out_even[0].astype(BF16),
        'w_in_odd': _pad_cols(w_in_odd[0], ODD_SEGS[-1][1]).astype(BF16),
        'dn_conv_w': dn_conv_w[0], 'dn_dt_bias': dn_dt_bias[0], 'dn_A_log': dn_A_log[0], 'dn_norm': dn_norm[0],
        'w_out_odd': w_out_odd[0].astype(BF16),
        'w_route': w_route, 'b_route': b_route,
        'moe_w_gate': moe_w_gate.astype(BF16), 'moe_w_up': moe_w_up.astype(BF16),
        'moe_w_down': moe_w_down.astype(BF16),
    }
    bp = x_prompt.shape[0]
    zeros = lambda *s: jnp.zeros(s, F32)
    prompt_states = (None, None, zeros(bp, SSM_INNER, SSM_STATE), zeros(bp, CONV_WIDTH - 1, SSM_CONV_CH),
                     zeros(bp, DN_QK, DN_DV), zeros(bp, CONV_WIDTH - 1, DN_CONV_CH))
    y_p, kp, vp, sp, scp, dnp, dncp = _trunk(x_prompt, p, prompt_states, True)
    sample_states = (cache_k_win[0], cache_v_win[0], state_ssm[0], state_ssm_conv[0], state_dn[0],
                     state_dn_conv[0])
    y_s, ks, vs, ss, scs, dns, dncs = _trunk(x_sample, p, sample_states, False)
    return (y_p, y_s, kp, ks, vp, vs, sp, ss, scp, scs, dnp, dns, dncp, dncs)
```

```python
import functools

import jax
import jax.numpy as jnp
from jax import lax
from jax.experimental import pallas as pl
from jax.experimental.pallas import tpu as pltpu

F32 = jnp.float32
BF16 = jnp.bfloat16
EPS = 1e-6

D_MODEL = 1024
SWA_HEADS = 8
SWA_KV_HEADS = 2
SWA_GROUP = SWA_HEADS // SWA_KV_HEADS
HEAD_DIM = 64
WINDOW = 128
SWA_Q = SWA_HEADS * HEAD_DIM
SWA_KV = SWA_KV_HEADS * HEAD_DIM
SSM_HEADS = 8
SSM_HEAD_DIM = 64
SSM_GROUPS = 2
SSM_STATE = 128
SSM_INNER = SSM_HEADS * SSM_HEAD_DIM
SSM_CHUNK = 128
SSM_CONV_CH = SSM_INNER + 2 * SSM_GROUPS * SSM_STATE
CONV_WIDTH = 4
DN_HEADS = 8
DN_DK = 128
DN_DV = 128
DN_CHUNK = 64
DN_QK = DN_HEADS * DN_DK
DN_VW = DN_HEADS * DN_DV
DN_CONV_CH = 2 * DN_QK + DN_VW
N_GROUPS = 4
EXPERTS_PER_GROUP = 8
N_EXPERTS = N_GROUPS * EXPERTS_PER_GROUP
EXPERT_FF = 256

LANES = 128
SUBLANES = 8
VMEM_LIMIT = 56 * 1024 * 1024

DN_SEQS_PER_STEP = 2
MOE_TILE = 512

NT_DIMS = (((1,), (1,)), ((), ()))
TN_DIMS = (((0,), (0,)), ((), ()))


def _cparams(sem):
    return pltpu.CompilerParams(dimension_semantics=sem, vmem_limit_bytes=VMEM_LIMIT)


def _const_spec(shape):
    nd = len(shape)
    return pl.BlockSpec(shape, lambda *_: (0,) * nd)


def _sigmoid(x):
    return 1.0 / (1.0 + jnp.exp(-x))


def _silu(x):
    return x * _sigmoid(x)


def _softplus(x):
    return jnp.maximum(x, 0.0) + jnp.log(1.0 + jnp.exp(-jnp.abs(x)))


def _rms(x, gain):
    return x * lax.rsqrt(jnp.mean(x * x, axis=-1, keepdims=True) + EPS) * gain


def _prenorm_proj_kernel(x_ref, g_ref, w_ref, *out_refs, segs):
    h = _rms(x_ref[...], g_ref[...]).astype(BF16)
    for o_ref, (a, b) in zip(out_refs, segs):
        o_ref[...] = jnp.dot(h, w_ref[:, a:b], preferred_element_type=F32)


def prenorm_proj(x, gain, w_bf16, segs, tm):
    t = x.shape[0]
    n_pad = w_bf16.shape[1]
    return pl.pallas_call(
        functools.partial(_prenorm_proj_kernel, segs=segs),
        grid=(t // tm,),
        in_specs=[pl.BlockSpec((tm, D_MODEL), lambda i: (i, 0)),
                  _const_spec((1, D_MODEL)),
                  _const_spec((D_MODEL, n_pad))],
        out_specs=[pl.BlockSpec((tm, b - a), lambda i: (i, 0)) for a, b in segs],
        out_shape=[jax.ShapeDtypeStruct((t, b - a), F32) for a, b in segs],
        compiler_params=_cparams(("parallel",)),
        name="prenorm_proj",
    )(x, gain.reshape(1, D_MODEL), w_bf16)


def _resid_proj_kernel(x_ref, *refs, n_in):
    acc = x_ref[...]
    for a_ref, w_ref in zip(refs[:n_in], refs[n_in:2 * n_in]):
        acc = acc + jnp.dot(a_ref[...].astype(BF16), w_ref[...], preferred_element_type=F32)
    refs[2 * n_in][...] = acc


def resid_proj(x, acts, ws, tm):
    t = x.shape[0]
    n_in = len(acts)
    return pl.pallas_call(
        functools.partial(_resid_proj_kernel, n_in=n_in),
        grid=(t // tm,),
        in_specs=([pl.BlockSpec((tm, D_MODEL), lambda i: (i, 0))]
                  + [pl.BlockSpec((tm, a.shape[1]), lambda i: (i, 0)) for a in acts]
                  + [_const_spec(w.shape) for w in ws]),
        out_specs=pl.BlockSpec((tm, D_MODEL), lambda i: (i, 0)),
        out_shape=jax.ShapeDtypeStruct((t, D_MODEL), F32),
        compiler_params=_cparams(("parallel",)),
        name="resid_proj",
    )(x, *acts, *ws)


def _swa_kernel(sink_ref, q_ref, kc_ref, kp_ref, vc_ref, vp_ref, qn_ref, kn_ref, o_ref, kout_ref, *,
                norm_prev, first_has_prev):
    n = pl.program_id(1)
    blk = q_ref.shape[0]
    q = q_ref[...]
    kc, kp, vc, vp = kc_ref[...], kp_ref[...], vc_ref[...], vp_ref[...]
    qn, kn = qn_ref[...], kn_ref[...]
    row = lax.broadcasted_iota(jnp.int32, (blk, 2 * blk), 0)
    col = lax.broadcasted_iota(jnp.int32, (blk, 2 * blk), 1)
    rel = row + blk - col
    mask = (rel >= 0) & (rel <= WINDOW)
    if not first_has_prev:
        mask = mask & ((n > 0) | (col >= blk))
    outs, kouts = [], []
    for j in range(SWA_KV_HEADS):
        sl = slice(j * HEAD_DIM, (j + 1) * HEAD_DIM)
        kcj = _rms(kc[:, sl], kn)
        kpj = _rms(kp[:, sl], kn) if norm_prev else kp[:, sl]
        kouts.append(kcj)
        kcat = jnp.concatenate([kpj, kcj], axis=0).astype(BF16)
        vcat = jnp.concatenate([vp[:, sl], vc[:, sl]], axis=0).astype(BF16)
        for g in range(SWA_GROUP):
            h = j * SWA_GROUP + g
            qh = _rms(q[:, h * HEAD_DIM:(h + 1) * HEAD_DIM], qn) * (HEAD_DIM ** -0.5)
            s = lax.dot_general(qh.astype(BF16), kcat, NT_DIMS, preferred_element_type=F32)
            s = jnp.where(mask, s, -jnp.inf)
            sink = sink_ref[h]
            m = jnp.maximum(jnp.max(s, axis=-1, keepdims=True), sink)
            p = jnp.exp(s - m)
            p = p / (jnp.sum(p, axis=-1, keepdims=True) + jnp.exp(sink - m))
            outs.append(jnp.dot(p.astype(BF16), vcat, preferred_element_type=F32))
    o_ref[...] = jnp.concatenate(outs, axis=1)
    kout_ref[...] = jnp.concatenate(kouts, axis=1)


def swa_attention(q, k, v, k_prev, v_prev, q_norm, k_norm, sinks, *, prompt):
    bsz, length = q.shape[0], q.shape[1]
    nb = length // WINDOW
    if prompt:
        prev_map = lambda b, n: (b, jnp.maximum(n - 1, 0), 0)
    else:
        prev_map = lambda b, n: (b, 0, 0)
    cur = lambda b, n: (b, n, 0)
    kv_blk = (None, WINDOW, SWA_KV)
    return pl.pallas_call(
        functools.partial(_swa_kernel, norm_prev=prompt, first_has_prev=not prompt),
        grid=(bsz, nb),
        in_specs=[pl.BlockSpec(memory_space=pltpu.SMEM),
                  pl.BlockSpec((None, WINDOW, SWA_Q), cur),
                  pl.BlockSpec(kv_blk, cur), pl.BlockSpec(kv_blk, prev_map),
                  pl.BlockSpec(kv_blk, cur), pl.BlockSpec(kv_blk, prev_map),
                  _const_spec((1, HEAD_DIM)), _const_spec((1, HEAD_DIM))],
        out_specs=[pl.BlockSpec((None, WINDOW, SWA_Q), cur), pl.BlockSpec(kv_blk, cur)],
        out_shape=[jax.ShapeDtypeStruct((bsz, length, SWA_Q), F32),
                   jax.ShapeDtypeStruct((bsz, length, SWA_KV), F32)],
        compiler_params=_cparams(("parallel", "arbitrary")),
        name="swa_attention",
    )(sinks, q, k, k_prev, v, v_prev, q_norm.reshape(1, HEAD_DIM), k_norm.reshape(1, HEAD_DIM))


def _chunk_conv(x_ref, xx_scr, w_ref, rows):
    xx_scr[SUBLANES:SUBLANES + rows, :] = x_ref[...]
    acc = None
    for tap in range(CONV_WIDTH):
        off = SUBLANES - (CONV_WIDTH - 1) + tap
        term = w_ref[tap:tap + 1, :] * xx_scr[off:off + rows, :]
        acc = term if acc is None else acc + term
    return acc


def _carry_conv_tail(xx_scr, rows):
    xx_scr[0:SUBLANES, :] = xx_scr[rows:rows + SUBLANES, :]


def _ssd_kernel(xbc_ref, z_ref, dt_ref, cw_ref, cb_ref, dtb_ref, alog_ref, dd_ref, nrm_ref, tail0_ref, h0_ref,
                y_ref, hout_ref, xx_scr, h_scr, *, n_valid):
    q_len = xbc_ref.shape[0]

    @pl.when(pl.program_id(1) == 0)
    def _():
        xx_scr[0:SUBLANES, :] = tail0_ref[...]
        h_scr[...] = h0_ref[...]

    act = _silu(_chunk_conv(xbc_ref, xx_scr, cw_ref, q_len) + cb_ref[...])
    _carry_conv_tail(xx_scr, q_len)
    xs = act[:, :SSM_INNER]
    bm = act[:, SSM_INNER:SSM_INNER + SSM_GROUPS * SSM_STATE]
    cm = act[:, SSM_INNER + SSM_GROUPS * SSM_STATE:]

    row = lax.broadcasted_iota(jnp.int32, (q_len, q_len), 0)
    col = lax.broadcasted_iota(jnp.int32, (q_len, q_len), 1)
    causal = row >= col
    dt = _softplus(dt_ref[...] + dtb_ref[...])
    if n_valid < q_len:
        dt = jnp.where(lax.broadcasted_iota(jnp.int32, dt.shape, 0) < n_valid, dt, 0.0)
    da = dt * (-jnp.exp(alog_ref[...]))
    cum = jnp.dot(causal.astype(F32), da, preferred_element_type=F32, precision=lax.Precision.HIGHEST)
    cum_t = cum.T
    dt_t = dt.T
    e_cum = jnp.exp(cum)
    hpg = SSM_HEADS // SSM_GROUPS
    gw = hpg * SSM_HEAD_DIM
    ys = []
    for g in range(SSM_GROUPS):
        bm_g = bm[:, g * SSM_STATE:(g + 1) * SSM_STATE].astype(BF16)
        cm_g = cm[:, g * SSM_STATE:(g + 1) * SSM_STATE].astype(BF16)
        cb = lax.dot_general(cm_g, bm_g, NT_DIMS, preferred_element_type=F32)
        h_g = h_scr[g * gw:(g + 1) * gw, :]
        y_state = lax.dot_general(cm_g, h_g.astype(BF16), NT_DIMS, preferred_element_type=F32)
        xt_parts, dec_parts = [], []
        for hh in range(hpg):
            h = g * hpg + hh
            x_h = xs[:, h * SSM_HEAD_DIM:(h + 1) * SSM_HEAD_DIM]
            cum_c = cum[:, h:h + 1]
            seg = jnp.exp(jnp.where(causal, cum_c - cum_t[h:h + 1, :], -jnp.inf))
            wgt = cb * seg * dt_t[h:h + 1, :]
            y = jnp.dot(wgt.astype(BF16), x_h.astype(BF16), preferred_element_type=F32)
            y = y + y_state[:, hh * SSM_HEAD_DIM:(hh + 1) * SSM_HEAD_DIM] * e_cum[:, h:h + 1]
            ys.append(y + dd_ref[0, h] * x_h)
            c_last = cum[q_len - 1:q_len, h:h + 1]
            xt_parts.append(x_h * (jnp.exp(c_last - cum_c) * dt[:, h:h + 1]))
            dec_parts.append(jnp.broadcast_to(jnp.exp(c_last), (SSM_HEAD_DIM, SSM_STATE)))
        xt = jnp.concatenate(xt_parts, axis=1).astype(BF16)
        upd = lax.dot_general(xt, bm_g, TN_DIMS, preferred_element_type=F32)
        h_scr[g * gw:(g + 1) * gw, :] = h_g * jnp.concatenate(dec_parts, axis=0) + upd
    y_all = jnp.concatenate(ys, axis=1) * _silu(z_ref[...])
    nrm = nrm_ref[...]
    y_ref[...] = jnp.concatenate(
        [_rms(y_all[:, g * gw:(g + 1) * gw], nrm[:, g * gw:(g + 1) * gw]) for g in range(SSM_GROUPS)], axis=1)
    hout_ref[...] = h_scr[...]


def ssd_mixer(xbc, z, dt, conv_w, conv_b, dt_bias, a_log, d_skip, norm_g, tail0, h0, n_valid):
    bsz, length = xbc.shape[0], xbc.shape[1]
    nc = length // SSM_CHUNK
    cur = lambda b, c: (b, c, 0)
    per_b = lambda b, c: (b, 0, 0)
    pad8 = lambda v: jnp.pad(v.reshape(1, SSM_HEADS), ((0, 0), (0, LANES - SSM_HEADS)))
    return pl.pallas_call(
        functools.partial(_ssd_kernel, n_valid=n_valid),
        grid=(bsz, nc),
        in_specs=[pl.BlockSpec((None, SSM_CHUNK, SSM_CONV_CH), cur),
                  pl.BlockSpec((None, SSM_CHUNK, SSM_INNER), cur),
                  pl.BlockSpec((None, SSM_CHUNK, LANES), cur),
                  _const_spec((CONV_WIDTH, SSM_CONV_CH)), _const_spec((1, SSM_CONV_CH)),
                  _const_spec((1, LANES)), _const_spec((1, LANES)),
                  pl.BlockSpec(memory_space=pltpu.SMEM),
                  _const_spec((1, SSM_INNER)),
                  pl.BlockSpec((None, SUBLANES, SSM_CONV_CH), per_b),
                  pl.BlockSpec((None, SSM_INNER, SSM_STATE), per_b)],
        out_specs=[pl.BlockSpec((None, SSM_CHUNK, SSM_INNER), cur),
                   pl.BlockSpec((None, SSM_INNER, SSM_STATE), per_b)],
        out_shape=[jax.ShapeDtypeStruct((bsz, length, SSM_INNER), F32),
                   jax.ShapeDtypeStruct((bsz, SSM_INNER, SSM_STATE), F32)],
        scratch_shapes=[pltpu.VMEM((SSM_CHUNK + SUBLANES, SSM_CONV_CH), F32),
                        pltpu.VMEM((SSM_INNER, SSM_STATE), F32)],
        compiler_params=_cparams(("parallel", "arbitrary")),
        name="ssd_mixer",
    )(xbc, z, dt, conv_w, conv_b.reshape(1, SSM_CONV_CH), pad8(dt_bias), pad8(a_log),
      d_skip.reshape(1, SSM_HEADS), norm_g.reshape(1, SSM_INNER), tail0, h0)


def _unit_lower_inverses(lmats, row, col):
    c = lmats[0].shape[0]
    mm = lambda a, b: jnp.dot(a.astype(BF16), b.astype(BF16), preferred_element_type=F32)
    eye = (row == col).astype(F32)
    blk = SUBLANES
    same = (row // blk) == (col // blk)
    xs = [jnp.where(same, -l, 0.0) for l in lmats]
    invs = [eye + x for x in xs]
    p = blk
    while p > 2:
        xs = [mm(x, x) for x in xs]
        invs = [i + mm(i, x) for i, x in zip(invs, xs)]
        p //= 2
    while blk < c:
        outer = ((row // (2 * blk)) == (col // (2 * blk))) & ((row // blk) != (col // blk))
        ts = [mm(i, jnp.where(outer, l, 0.0)) for i, l in zip(invs, lmats)]
        invs = [i - mm(t, i) for i, t in zip(invs, ts)]
        blk *= 2
    return invs


def _dn_kernel(qkv_ref, z_ref, ba_ref, cw_ref, dtb_ref, alog_ref, nrm_ref, tail0_ref, s0_ref,
               o_ref, sout_ref, xx_scr, s_scr, *, n_valid):
    nseq, c_len = qkv_ref.shape[0], qkv_ref.shape[1]
    seqs = range(nseq)
    chains = [(s, h) for s in seqs for h in range(DN_HEADS)]
    ids = range(len(chains))

    @pl.when(pl.program_id(1) == 0)
    def _():
        xx_scr[:, 0:SUBLANES, :] = tail0_ref[...]
        s_scr[...] = s0_ref[...]

    row = lax.broadcasted_iota(jnp.int32, (c_len, c_len), 0)
    col = lax.broadcasted_iota(jnp.int32, (c_len, c_len), 1)
    incl = row >= col
    strict = row > col
    tri = incl.astype(F32)
    nrm = nrm_ref[...]
    bf = lambda t: t.astype(BF16)
    mm = lambda a, b: jnp.dot(bf(a), bf(b), preferred_element_type=F32)

    act, beta, cum, cum_t, e_cum, e_rest, e_last = [], [], [], [], [], [], []
    for s in seqs:
        act.append(_silu(_chunk_conv(qkv_ref.at[s], xx_scr.at[s], cw_ref, c_len)))
        _carry_conv_tail(xx_scr.at[s], c_len)
        ba = ba_ref[s]
        beta_s = _sigmoid(ba)
        gate = -jnp.exp(alog_ref[...]) * _softplus(ba + dtb_ref[...])
        if n_valid < c_len:
            valid = lax.broadcasted_iota(jnp.int32, ba.shape, 0) < n_valid
            beta_s = jnp.where(valid, beta_s, 0.0)
            gate = jnp.where(valid, gate, 0.0)
        beta.append(beta_s)
        cum_s = jnp.dot(tri, gate, preferred_element_type=F32, precision=lax.Precision.HIGHEST)
        c_last = cum_s[c_len - 1:c_len, :]
        cum.append(cum_s)
        cum_t.append(cum_s.T)
        e_cum.append(jnp.exp(cum_s))
        e_rest.append(jnp.exp(c_last - cum_s))
        e_last.append(jnp.exp(c_last))

    q, k, kb, rhs, decay = [], [], [], [], []
    for i in ids:
        s, h = chains[i]
        gl = DN_HEADS + h
        q_h = act[s][:, h * DN_DK:(h + 1) * DN_DK]
        k_h = act[s][:, DN_QK + h * DN_DK:DN_QK + (h + 1) * DN_DK]
        v_h = act[s][:, 2 * DN_QK + h * DN_DV:2 * DN_QK + (h + 1) * DN_DV]
        q.append(q_h * lax.rsqrt(jnp.sum(q_h * q_h, axis=-1, keepdims=True) + EPS) * (DN_DK ** -0.5))
        k.append(k_h * lax.rsqrt(jnp.sum(k_h * k_h, axis=-1, keepdims=True) + EPS))
        beta_c = beta[s][:, h:h + 1]
        kb.append(k[i] * beta_c)
        rhs.append(jnp.concatenate([v_h * beta_c, kb[i] * e_cum[s][:, gl:gl + 1]], axis=1))
        decay.append(jnp.exp(jnp.where(incl, cum[s][:, gl:gl + 1] - cum_t[s][gl:gl + 1, :], -jnp.inf)))
    kq = [lax.dot_general(bf(jnp.concatenate([kb[i], q[i]], axis=0)), bf(k[i]), NT_DIMS,
                          preferred_element_type=F32) for i in ids]
    lmat = [jnp.where(strict, kq[i][:c_len] * decay[i], 0.0) for i in ids]
    attn = [kq[i][c_len:] * decay[i] for i in ids]
    tinv = _unit_lower_inverses(lmat, row, col)
    uw = [mm(tinv[i], rhs[i]) for i in ids]
    s_old = [s_scr[s, h * DN_DK:(h + 1) * DN_DK, :] for s, h in chains]
    qe = [q[i] * e_cum[s][:, DN_HEADS + h:DN_HEADS + h + 1] for i, (s, h) in enumerate(chains)]
    wq = [mm(jnp.concatenate([uw[i][:, DN_DV:], qe[i]], axis=0), s_old[i]) for i in ids]
    v_new = [uw[i][:, :DN_DV] - wq[i][:c_len] for i in ids]
    o = [wq[i][c_len:] + mm(attn[i], v_new[i]) for i in ids]
    for i in ids:
        s, h = chains[i]
        gl = DN_HEADS + h
        s_scr[s, h * DN_DK:(h + 1) * DN_DK, :] = s_old[i] * e_last[s][:, gl:gl + 1] + lax.dot_general(
            bf(k[i] * e_rest[s][:, gl:gl + 1]), bf(v_new[i]), TN_DIMS, preferred_element_type=F32)
    for s in seqs:
        z = z_ref[s]
        o_ref[s] = jnp.concatenate(
            [_rms(o[s * DN_HEADS + h], nrm) * _silu(z[:, h * DN_DV:(h + 1) * DN_DV]) for h in range(DN_HEADS)],
            axis=1)
    sout_ref[...] = s_scr[...]


def dn_mixer(qkv, z, ba, conv_w, dt_bias, a_log, norm_g, tail0, s0, n_valid, nseq):
    bsz, length = qkv.shape[0], qkv.shape[1]
    nc = length // DN_CHUNK
    cur = lambda b, c: (b, c, 0)
    per_b = lambda b, c: (b, 0, 0)
    pad_a = lambda v: jnp.pad(v.reshape(1, DN_HEADS), ((0, 0), (DN_HEADS, LANES - 2 * DN_HEADS)))
    return pl.pallas_call(
        functools.partial(_dn_kernel, n_valid=n_valid),
        grid=(bsz // nseq, nc),
        in_specs=[pl.BlockSpec((nseq, DN_CHUNK, DN_CONV_CH), cur),
                  pl.BlockSpec((nseq, DN_CHUNK, DN_VW), cur),
                  pl.BlockSpec((nseq, DN_CHUNK, LANES), cur),
                  _const_spec((CONV_WIDTH, DN_CONV_CH)),
                  _const_spec((1, LANES)), _const_spec((1, LANES)), _const_spec((1, DN_DV)),
                  pl.BlockSpec((nseq, SUBLANES, DN_CONV_CH), per_b),
                  pl.BlockSpec((nseq, DN_QK, DN_DV), per_b)],
        out_specs=[pl.BlockSpec((nseq, DN_CHUNK, DN_VW), cur),
                   pl.BlockSpec((nseq, DN_QK, DN_DV), per_b)],
        out_shape=[jax.ShapeDtypeStruct((bsz, length, DN_VW), F32),
                   jax.ShapeDtypeStruct((bsz, DN_QK, DN_DV), F32)],
        scratch_shapes=[pltpu.VMEM((nseq, DN_CHUNK + SUBLANES, DN_CONV_CH), F32),
                        pltpu.VMEM((nseq, DN_QK, DN_DV), F32)],
        compiler_params=_cparams(("parallel", "arbitrary")),
        name="dn_mixer",
    )(qkv, z, ba, conv_w, pad_a(dt_bias), pad_a(a_log), norm_g.reshape(1, DN_DV), tail0, s0)


XG_WIDTH = D_MODEL + LANES


def _router_kernel(x_ref, g_ref, wr_ref, br_ref, xg_ref, grp_ref, rank_ref, cnt_ref, carry_scr):
    tm = x_ref.shape[0]

    @pl.when(pl.program_id(0) == 0)
    def _():
        carry_scr[...] = jnp.zeros_like(carry_scr)

    x = x_ref[...]
    h = _rms(x, g_ref[...])
    logits = jnp.dot(h, wr_ref[...], preferred_element_type=F32, precision=lax.Precision.HIGHEST) + br_ref[...]
    lt = logits.T
    sub = lax.broadcasted_iota(jnp.int32, (SUBLANES, tm), 0)
    neg = -jnp.inf
    glog = jnp.where(sub < N_GROUPS, lt[N_EXPERTS:N_EXPERTS + SUBLANES], neg)
    gmax = jnp.max(glog, axis=0, keepdims=True)
    g_p = 1.0 / jnp.sum(jnp.exp(glog - gmax), axis=0, keepdims=True)
    g_i = jnp.min(jnp.where(glog == gmax, sub, SUBLANES), axis=0, keepdims=True)
    sel = lt[0:EXPERTS_PER_GROUP]
    for g in range(1, N_GROUPS):
        sel = jnp.where(g_i == g, lt[g * EXPERTS_PER_GROUP:(g + 1) * EXPERTS_PER_GROUP], sel)
    m1 = jnp.max(sel, axis=0, keepdims=True)
    zsum = jnp.sum(jnp.exp(sel - m1), axis=0, keepdims=True)
    i1 = jnp.min(jnp.where(sel == m1, sub, SUBLANES), axis=0, keepdims=True)
    sel2 = jnp.where(sub == i1, neg, sel)
    m2 = jnp.max(sel2, axis=0, keepdims=True)
    i2 = jnp.min(jnp.where(sel2 == m2, sub, SUBLANES), axis=0, keepdims=True)
    p1 = 1.0 / zsum
    p2 = jnp.exp(m2 - m1) / zsum
    gate1 = g_p * p1 / (p1 + p2)
    gate2 = g_p * p2 / (p1 + p2)
    gates = jnp.where(sub == i1, gate1, 0.0) + jnp.where(sub == i2, gate2, 0.0)

    onehot = (sub == g_i).astype(F32)
    before = (lax.broadcasted_iota(jnp.int32, (tm, tm), 0) < lax.broadcasted_iota(jnp.int32, (tm, tm), 1))
    rank_in_tile = jnp.dot(onehot.astype(BF16), before.astype(BF16), preferred_element_type=F32)
    carry = carry_scr[...]
    rank = jnp.sum(onehot * (rank_in_tile + carry[:, 0:1]), axis=0, keepdims=True)
    grp_ref[...] = g_i
    rank_ref[...] = rank.astype(jnp.int32)
    new_carry = carry + jnp.sum(onehot, axis=1, keepdims=True)
    carry_scr[...] = new_carry
    cnt_ref[...] = new_carry

    xg_ref[:, :D_MODEL] = x
    xg_ref[:, D_MODEL:] = jnp.concatenate([gates, jnp.zeros((LANES - SUBLANES, tm), F32)], axis=0).T


def moe_router(x, gain, w_router_pad, b_router_pad, tm):
    t = x.shape[0]
    return pl.pallas_call(
        _router_kernel,
        grid=(t // tm,),
        in_specs=[pl.BlockSpec((tm, D_MODEL), lambda i: (i, 0)),
                  _const_spec((1, D_MODEL)), _const_spec((D_MODEL, LANES)), _const_spec((1, LANES))],
        out_specs=[pl.BlockSpec((tm, XG_WIDTH), lambda i: (i, 0)),
                   pl.BlockSpec((None, 1, tm), lambda i: (i, 0, 0)),
                   pl.BlockSpec((None, 1, tm), lambda i: (i, 0, 0)),
                   _const_spec((SUBLANES, LANES))],
        out_shape=[jax.ShapeDtypeStruct((t, XG_WIDTH), F32),
                   jax.ShapeDtypeStruct((t // tm, 1, tm), jnp.int32),
                   jax.ShapeDtypeStruct((t // tm, 1, tm), jnp.int32),
                   jax.ShapeDtypeStruct((SUBLANES, LANES), F32)],
        scratch_shapes=[pltpu.VMEM((SUBLANES, LANES), F32)],
        compiler_params=_cparams(("arbitrary",)),
        name="moe_router",
    )(x, gain.reshape(1, D_MODEL), w_router_pad, b_router_pad)


def _row_copy(src_hbm, src_row, dst_hbm, dst_row, sem):
    return pltpu.make_async_copy(src_hbm.at[pl.ds(src_row, 1)], dst_hbm.at[pl.ds(dst_row, 1)], sem)


def _dispatch_kernel(pos_ref, pad_lo_ref, pad_hi_ref, xg_hbm, zero_hbm, xs_hbm, sem, zsem, *, tm):
    i = pl.program_id(0)
    base = i * tm

    def issue(r, carry):
        _row_copy(xg_hbm, base + r, xs_hbm, pos_ref[0, r], sem).start()
        return carry

    lax.fori_loop(0, tm, issue, 0, unroll=8)
    pltpu.make_async_copy(xg_hbm.at[pl.ds(0, tm)], xs_hbm.at[pl.ds(0, tm)], sem).wait()

    @pl.when(i == pl.num_programs(0) - 1)
    def _():
        for k in range(pad_lo_ref.shape[0]):
            def fill(j, carry):
                _row_copy(zero_hbm, 0, xs_hbm, j, zsem).start()
                return carry

            def drain(j, carry):
                _row_copy(zero_hbm, 0, xs_hbm, j, zsem).wait()
                return carry

            lax.fori_loop(pad_lo_ref[k], pad_hi_ref[k], fill, 0)
            lax.fori_loop(pad_lo_ref[k], pad_hi_ref[k], drain, 0)


def moe_dispatch(pos, pad_lo, pad_hi, xg, tm, n_slots):
    t = xg.shape[0]
    return pl.pallas_call(
        functools.partial(_dispatch_kernel, tm=tm),
        grid=(t // tm,),
        in_specs=[pl.BlockSpec((None, 1, tm), lambda i: (i, 0, 0), memory_space=pltpu.SMEM),
                  pl.BlockSpec(memory_space=pltpu.SMEM), pl.BlockSpec(memory_space=pltpu.SMEM),
                  pl.BlockSpec(memory_space=pl.ANY), pl.BlockSpec(memory_space=pl.ANY)],
        out_specs=pl.BlockSpec(memory_space=pl.ANY),
        out_shape=jax.ShapeDtypeStruct((n_slots, XG_WIDTH), F32),
        scratch_shapes=[pltpu.SemaphoreType.DMA, pltpu.SemaphoreType.DMA],
        compiler_params=pltpu.CompilerParams(dimension_semantics=("arbitrary",), has_side_effects=True),
        name="moe_dispatch",
    )(pos, pad_lo, pad_hi, xg, jnp.zeros((SUBLANES, XG_WIDTH), F32))


def _collect_kernel(pos_ref, os_hbm, out_hbm, sem, *, tm):
    base = pl.program_id(0) * tm

    def issue(r, carry):
        _row_copy(os_hbm, pos_ref[0, r], out_hbm, base + r, sem).start()
        return carry

    lax.fori_loop(0, tm, issue, 0, unroll=8)
    pltpu.make_async_copy(os_hbm.at[pl.ds(0, tm)], out_hbm.at[pl.ds(0, tm)], sem).wait()


def moe_collect(pos, os_sorted, t, tm):
    return pl.pallas_call(
        functools.partial(_collect_kernel, tm=tm),
        grid=(t // tm,),
        in_specs=[pl.BlockSpec((None, 1, tm), lambda i: (i, 0, 0), memory_space=pltpu.SMEM),
                  pl.BlockSpec(memory_space=pl.ANY)],
        out_specs=pl.BlockSpec(memory_space=pl.ANY),
        out_shape=jax.ShapeDtypeStruct((t, D_MODEL), F32),
        scratch_shapes=[pltpu.SemaphoreType.DMA],
        compiler_params=pltpu.CompilerParams(dimension_semantics=("arbitrary",), has_side_effects=True),
        name="moe_collect",
    )(pos, os_sorted)


def _group_experts_kernel(grp_ref, xg_ref, g_ref, wg_ref, wu_ref, wd_ref, o_ref):
    del grp_ref
    x = xg_ref[:, :D_MODEL]
    gates = xg_ref[:, D_MODEL:]
    h = _rms(x, g_ref[...]).astype(BF16)
    acc = x
    for e in range(EXPERTS_PER_GROUP):
        gate = jnp.dot(h, wg_ref[e], preferred_element_type=F32)
        up = jnp.dot(h, wu_ref[e], preferred_element_type=F32)
        act = (_silu(gate) * up).astype(BF16)
        acc = acc + gates[:, e:e + 1] * jnp.dot(act, wd_ref[e], preferred_element_type=F32)
    o_ref[...] = acc


def moe_group_experts(tile_grp, xs, gain, wg, wu, wd, tile):
    n_steps = xs.shape[0] // tile
    rows = lambda i, grp: (i, 0)
    wts = lambda i, grp: (grp[i], 0, 0, 0)
    return pl.pallas_call(
        _group_experts_kernel,
        grid_spec=pltpu.PrefetchScalarGridSpec(
            num_scalar_prefetch=1,
            grid=(n_steps,),
            in_specs=[pl.BlockSpec((tile, XG_WIDTH), rows),
                      pl.BlockSpec((1, D_MODEL), lambda i, grp: (0, 0)),
                      pl.BlockSpec((None, EXPERTS_PER_GROUP, D_MODEL, EXPERT_FF), wts),
                      pl.BlockSpec((None, EXPERTS_PER_GROUP, D_MODEL, EXPERT_FF), wts),
                      pl.BlockSpec((None, EXPERTS_PER_GROUP, EXPERT_FF, D_MODEL), wts)],
            out_specs=pl.BlockSpec((tile, D_MODEL), rows)),
        out_shape=jax.ShapeDtypeStruct((xs.shape[0], D_MODEL), F32),
        compiler_params=_cparams(("arbitrary",)),
        name="moe_group_experts",
    )(tile_grp, xs, gain.reshape(1, D_MODEL), wg, wu, wd)


def moe_layer(x, gain, w_route, b_route, wg, wu, wd):
    t = x.shape[0]
    tile = MOE_TILE if t % MOE_TILE == 0 else t
    n_tiles = t // tile + N_GROUPS
    xg, grp, rank, cnt = moe_router(x, gain, w_route, b_route, tile)
    counts = cnt[:N_GROUPS, 0].astype(jnp.int32)
    tile_ends = jnp.cumsum((counts + tile - 1) // tile)
    offs = (tile_ends - (counts + tile - 1) // tile) * tile
    pos = rank + sum(jnp.where(grp == g, offs[g], 0) for g in range(N_GROUPS))
    pad_lo = jnp.concatenate([offs + counts, tile_ends[-1:] * tile])
    pad_hi = jnp.concatenate([tile_ends * tile, jnp.full((1,), n_tiles * tile, jnp.int32)])
    tile_grp = jnp.minimum(jnp.sum((jnp.arange(n_tiles, dtype=jnp.int32)[:, None] >= tile_ends[None, :])
                                   .astype(jnp.int32), axis=1), N_GROUPS - 1)
    xs = moe_dispatch(pos, pad_lo, pad_hi, xg, tile, n_tiles * tile)
    os_sorted = moe_group_experts(tile_grp, xs, gain, wg, wu, wd, tile)
    return moe_collect(pos, os_sorted, t, tile)


def _pad_cols(w, n_pad):
    return jnp.pad(w, ((0, 0), (0, n_pad - w.shape[1])))


def _pad_seq(u, rows):
    return jnp.pad(u[:, None, :], ((0, 0), (0, rows - 1), (0, 0)))


def _conv_tail(buf):
    return jnp.pad(buf, ((0, 0), (SUBLANES - (CONV_WIDTH - 1), 0), (0, 0)))


EVEN_SEGS = ((0, 512), (512, 640), (640, 768), (768, 1280), (1280, 2304), (2304, 2432))
ODD_SEGS = ((0, 3072), (3072, 4096), (4096, 4224))


def _trunk(x_seq, p, states, prompt):
    bsz, length = x_seq.shape[0], x_seq.shape[1]
    t = bsz * length
    tm = 512 if t % 512 == 0 else t
    x = x_seq.reshape(t, D_MODEL)
    k_win, v_win, ssm_h, ssm_conv, dn_s, dn_conv = states

    q, k, v, z, xbc, dt = prenorm_proj(x, p['ln_mix'][0], p['w_in_even'], EVEN_SEGS, tm)
    if prompt:
        seq = lambda u: u.reshape(bsz, length, u.shape[-1])
        q3, k3, v3, z3, xbc3, dt3 = seq(q), seq(k), seq(v), seq(z), seq(xbc), seq(dt)
        att, k_normed = swa_attention(q3, k3, v3, k3, v3, p['q_norm'], p['k_norm'], p['attn_sinks'], prompt=True)
        new_k = k_normed[:, -WINDOW:].reshape(bsz, WINDOW, SWA_KV_HEADS, HEAD_DIM)
        new_v = v3[:, -WINDOW:].reshape(bsz, WINDOW, SWA_KV_HEADS, HEAD_DIM)
        new_ssm_conv = xbc3[:, -(CONV_WIDTH - 1):]
        n_valid_ssm, n_valid_dn = SSM_CHUNK, DN_CHUNK
    else:
        q3, k3, v3 = _pad_seq(q, WINDOW), _pad_seq(k, WINDOW), _pad_seq(v, WINDOW)
        z3, xbc3, dt3 = _pad_seq(z, SSM_CHUNK), _pad_seq(xbc, SSM_CHUNK), _pad_seq(dt, SSM_CHUNK)
        kb = k_win.reshape(bsz, WINDOW, SWA_KV)
        vb = v_win.reshape(bsz, WINDOW, SWA_KV)
        att, k_normed = swa_attention(q3, k3, v3, kb, vb, p['q_norm'], p['k_norm'], p['attn_sinks'], prompt=False)
        att = att[:, :1]
        new_k = jnp.concatenate([kb[:, 1:], k_normed[:, :1]], axis=1).reshape(bsz, WINDOW, SWA_KV_HEADS, HEAD_DIM)
        new_v = jnp.concatenate([vb[:, 1:], v3[:, :1]], axis=1).reshape(bsz, WINDOW, SWA_KV_HEADS, HEAD_DIM)
        new_ssm_conv = jnp.concatenate([ssm_conv[:, 1:], xbc[:, None, :]], axis=1)
        n_valid_ssm, n_valid_dn = 1, 1
    y_ssm, new_h = ssd_mixer(xbc3, z3, dt3, p['ssm_conv_w'], p['ssm_conv_b'], p['ssm_dt_bias'], p['ssm_A_log'],
                             p['ssm_D'], p['ssm_norm'], _conv_tail(ssm_conv),
                             ssm_h.reshape(bsz, SSM_INNER, SSM_STATE), n_valid_ssm)
    if not prompt:
        y_ssm = y_ssm[:, :1]
    x = resid_proj(x, [att.reshape(t, SWA_Q), y_ssm.reshape(t, SSM_INNER)],
                   [p['w_out_even'][:SWA_Q], p['w_out_even'][SWA_Q:]], tm)
    x = moe_layer(x, p['ln_ffn'][0], p['w_route'][0], p['b_route'][0],
                  p['moe_w_gate'][0], p['moe_w_up'][0], p['moe_w_down'][0])

    qkv, zz, ba = prenorm_proj(x, p['ln_mix'][1], p['w_in_odd'], ODD_SEGS, tm // 2 if tm >= 512 else tm)
    if prompt:
        qkv3 = qkv.reshape(bsz, length, DN_CONV_CH)
        zz3 = zz.reshape(bsz, length, DN_VW)
        ba3 = ba.reshape(bsz, length, LANES)
        new_dn_conv = qkv3[:, -(CONV_WIDTH - 1):]
    else:
        qkv3, zz3, ba3 = _pad_seq(qkv, DN_CHUNK), _pad_seq(zz, DN_CHUNK), _pad_seq(ba, DN_CHUNK)
        new_dn_conv = jnp.concatenate([dn_conv[:, 1:], qkv[:, None, :]], axis=1)
    o_dn, new_s = dn_mixer(qkv3, zz3, ba3, p['dn_conv_w'], p['dn_dt_bias'], p['dn_A_log'], p['dn_norm'],
                           _conv_tail(dn_conv), dn_s.reshape(bsz, DN_QK, DN_DV), n_valid_dn, DN_SEQS_PER_STEP)
    if not prompt:
        o_dn = o_dn[:, :1]
    x = resid_proj(x, [o_dn.reshape(t, DN_VW)], [p['w_out_odd']], tm)
    x = moe_layer(x, p['ln_ffn'][1], p['w_route'][1], p['b_route'][1],
                  p['moe_w_gate'][1], p['moe_w_up'][1], p['moe_w_down'][1])

    return (x.reshape(bsz, length, D_MODEL), new_k[None], new_v[None],
            new_h.reshape(1, bsz, SSM_HEADS, SSM_HEAD_DIM, SSM_STATE), new_ssm_conv[None],
            new_s.reshape(1, bsz, DN_HEADS, DN_DK, DN_DV), new_dn_conv[None])


def kernel(x_prompt, x_sample, cache_k_win, cache_v_win, state_ssm, state_ssm_conv, state_dn, state_dn_conv,
           ln_mix, ln_ffn, w_in_even, q_norm, k_norm, attn_sinks, ssm_conv_w, ssm_conv_b, ssm_dt_bias,
           ssm_A_log, ssm_D, ssm_norm, w_out_even, w_in_odd, dn_conv_w, dn_dt_bias, dn_A_log, dn_norm,
           w_out_odd, moe_w_group, moe_b_group, moe_w_router, moe_b_router, moe_w_gate, moe_w_up, moe_w_down):
    w_route = _pad_cols(jnp.concatenate([moe_w_router, moe_w_group], axis=-1).reshape(-1, N_EXPERTS + N_GROUPS),
                        LANES).reshape(2, D_MODEL, LANES)
    b_route = _pad_cols(jnp.concatenate([moe_b_router, moe_b_group], axis=-1), LANES).reshape(2, 1, LANES)
    p = {
        'ln_mix': ln_mix, 'ln_ffn': ln_ffn,
        'w_in_even': _pad_cols(w_in_even[0], EVEN_SEGS[-1][1]).astype(BF16),
        'q_norm': q_norm[0], 'k_norm': k_norm[0], 'attn_sinks': attn_sinks[0],
        'ssm_conv_w': ssm_conv_w[0], 'ssm_conv_b': ssm_conv_b[0], 'ssm_dt_bias': ssm_dt_bias[0],
        'ssm_A_log': ssm_A_log[0], 'ssm_D': ssm_D[0], 'ssm_norm': ssm_norm[0],
        'w_out_even': w_out_even[0].astype(BF16),
        'w_in_odd': _pad_cols(w_in_odd[0], ODD_SEGS[-1][1]).astype(BF16),
        'dn_conv_w': dn_conv_w[0], 'dn_dt_bias': dn_dt_bias[0], 'dn_A_log': dn_A_log[0], 'dn_norm': dn_norm[0],
        'w_out_odd': w_out_odd[0].astype(BF16),
        'w_route': w_route, 'b_route': b_route,
        'moe_w_gate': moe_w_gate.astype(BF16).reshape(2, N_GROUPS, EXPERTS_PER_GROUP, D_MODEL, EXPERT_FF),
        'moe_w_up': moe_w_up.astype(BF16).reshape(2, N_GROUPS, EXPERTS_PER_GROUP, D_MODEL, EXPERT_FF),
        'moe_w_down': moe_w_down.astype(BF16).reshape(2, N_GROUPS, EXPERTS_PER_GROUP, EXPERT_FF, D_MODEL),
    }
    bp = x_prompt.shape[0]
    zeros = lambda *s: jnp.zeros(s, F32)
    prompt_states = (None, None, zeros(bp, SSM_INNER, SSM_STATE), zeros(bp, CONV_WIDTH - 1, SSM_CONV_CH),
                     zeros(bp, DN_QK, DN_DV), zeros(bp, CONV_WIDTH - 1, DN_CONV_CH))
    y_p, kp, vp, sp, scp, dnp, dncp = _trunk(x_prompt, p, prompt_states, True)
    sample_states = (cache_k_win[0], cache_v_win[0], state_ssm[0], state_ssm_conv[0], state_dn[0],
                     state_dn_conv[0])
    y_s, ks, vs, ss, scs, dns, dncs = _trunk(x_sample, p, sample_states, False)
    return (y_p, y_s, kp, ks, vp, vs, sp, ss, scp, scs, dnp, dns, dncp, dncs)
```

```python
import functools

import jax
import jax.numpy as jnp
from jax import lax
from jax.experimental import pallas as pl
from jax.experimental.pallas import tpu as pltpu

F32 = jnp.float32
BF16 = jnp.bfloat16
EPS = 1e-6

D_MODEL = 1024
SWA_HEADS = 8
SWA_KV_HEADS = 2
SWA_GROUP = SWA_HEADS // SWA_KV_HEADS
HEAD_DIM = 64
WINDOW = 128
SWA_Q = SWA_HEADS * HEAD_DIM
SWA_KV = SWA_KV_HEADS * HEAD_DIM
SSM_HEADS = 8
SSM_HEAD_DIM = 64
SSM_GROUPS = 2
SSM_STATE = 128
SSM_INNER = SSM_HEADS * SSM_HEAD_DIM
SSM_CHUNK = 128
SSM_CONV_CH = SSM_INNER + 2 * SSM_GROUPS * SSM_STATE
CONV_WIDTH = 4
DN_HEADS = 8
DN_DK = 128
DN_DV = 128
DN_CHUNK = 64
DN_QK = DN_HEADS * DN_DK
DN_VW = DN_HEADS * DN_DV
DN_CONV_CH = 2 * DN_QK + DN_VW
N_GROUPS = 4
EXPERTS_PER_GROUP = 8
N_EXPERTS = N_GROUPS * EXPERTS_PER_GROUP
EXPERT_FF = 256

LANES = 128
SUBLANES = 8
VMEM_LIMIT = 56 * 1024 * 1024

DN_SEQS_PER_STEP = 2
MOE_TILE = 512

NT_DIMS = (((1,), (1,)), ((), ()))
TN_DIMS = (((0,), (0,)), ((), ()))


def _cparams(sem):
    return pltpu.CompilerParams(dimension_semantics=sem, vmem_limit_bytes=VMEM_LIMIT)


def _const_spec(shape):
    nd = len(shape)
    return pl.BlockSpec(shape, lambda *_: (0,) * nd)


def _sigmoid(x):
    return 1.0 / (1.0 + jnp.exp(-x))


def _silu(x):
    return x * _sigmoid(x)


def _softplus(x):
    return jnp.maximum(x, 0.0) + jnp.log(1.0 + jnp.exp(-jnp.abs(x)))


def _rms(x, gain):
    return x * lax.rsqrt(jnp.mean(x * x, axis=-1, keepdims=True) + EPS) * gain


def _prenorm_proj_kernel(x_ref, g_ref, w_ref, *out_refs, segs):
    h = _rms(x_ref[...], g_ref[...]).astype(BF16)
    for o_ref, (a, b) in zip(out_refs, segs):
        o_ref[...] = jnp.dot(h, w_ref[:, a:b], preferred_element_type=F32)


def prenorm_proj(x, gain, w_bf16, segs, tm):
    t = x.shape[0]
    n_pad = w_bf16.shape[1]
    return pl.pallas_call(
        functools.partial(_prenorm_proj_kernel, segs=segs),
        grid=(t // tm,),
        in_specs=[pl.BlockSpec((tm, D_MODEL), lambda i: (i, 0)),
                  _const_spec((1, D_MODEL)),
                  _const_spec((D_MODEL, n_pad))],
        out_specs=[pl.BlockSpec((tm, b - a), lambda i: (i, 0)) for a, b in segs],
        out_shape=[jax.ShapeDtypeStruct((t, b - a), F32) for a, b in segs],
        compiler_params=_cparams(("parallel",)),
        name="prenorm_proj",
    )(x, gain.reshape(1, D_MODEL), w_bf16)


def _resid_proj_kernel(x_ref, *refs, n_in):
    acc = x_ref[...]
    for a_ref, w_ref in zip(refs[:n_in], refs[n_in:2 * n_in]):
        acc = acc + jnp.dot(a_ref[...].astype(BF16), w_ref[...], preferred_element_type=F32)
    refs[2 * n_in][...] = acc


def resid_proj(x, acts, ws, tm):
    t = x.shape[0]
    n_in = len(acts)
    return pl.pallas_call(
        functools.partial(_resid_proj_kernel, n_in=n_in),
        grid=(t // tm,),
        in_specs=([pl.BlockSpec((tm, D_MODEL), lambda i: (i, 0))]
                  + [pl.BlockSpec((tm, a.shape[1]), lambda i: (i, 0)) for a in acts]
                  + [_const_spec(w.shape) for w in ws]),
        out_specs=pl.BlockSpec((tm, D_MODEL), lambda i: (i, 0)),
        out_shape=jax.ShapeDtypeStruct((t, D_MODEL), F32),
        compiler_params=_cparams(("parallel",)),
        name="resid_proj",
    )(x, *acts, *ws)


def _swa_kernel(sink_ref, q_ref, kc_ref, kp_ref, vc_ref, vp_ref, qn_ref, kn_ref, o_ref, kout_ref, *,
                norm_prev, first_has_prev):
    n = pl.program_id(1)
    blk = q_ref.shape[0]
    q = q_ref[...]
    kc, kp, vc, vp = kc_ref[...], kp_ref[...], vc_ref[...], vp_ref[...]
    qn, kn = qn_ref[...], kn_ref[...]
    row = lax.broadcasted_iota(jnp.int32, (blk, 2 * blk), 0)
    col = lax.broadcasted_iota(jnp.int32, (blk, 2 * blk), 1)
    rel = row + blk - col
    mask = (rel >= 0) & (rel <= WINDOW)
    if not first_has_prev:
        mask = mask & ((n > 0) | (col >= blk))
    outs, kouts = [], []
    for j in range(SWA_KV_HEADS):
        sl = slice(j * HEAD_DIM, (j + 1) * HEAD_DIM)
        kcj = _rms(kc[:, sl], kn)
        kpj = _rms(kp[:, sl], kn) if norm_prev else kp[:, sl]
        kouts.append(kcj)
        kcat = jnp.concatenate([kpj, kcj], axis=0).astype(BF16)
        vcat = jnp.concatenate([vp[:, sl], vc[:, sl]], axis=0).astype(BF16)
        for g in range(SWA_GROUP):
            h = j * SWA_GROUP + g
            qh = _rms(q[:, h * HEAD_DIM:(h + 1) * HEAD_DIM], qn) * (HEAD_DIM ** -0.5)
            s = lax.dot_general(qh.astype(BF16), kcat, NT_DIMS, preferred_element_type=F32)
            s = jnp.where(mask, s, -jnp.inf)
            sink = sink_ref[h]
            m = jnp.maximum(jnp.max(s, axis=-1, keepdims=True), sink)
            p = jnp.exp(s - m)
            p = p / (jnp.sum(p, axis=-1, keepdims=True) + jnp.exp(sink - m))
            outs.append(jnp.dot(p.astype(BF16), vcat, preferred_element_type=F32))
    o_ref[...] = jnp.concatenate(outs, axis=1)
    kout_ref[...] = jnp.concatenate(kouts, axis=1)


def swa_attention(q, k, v, k_prev, v_prev, q_norm, k_norm, sinks, *, prompt):
    bsz, length = q.shape[0], q.shape[1]
    nb = length // WINDOW
    if prompt:
        prev_map = lambda b, n: (b, jnp.maximum(n - 1, 0), 0)
    else:
        prev_map = lambda b, n: (b, 0, 0)
    cur = lambda b, n: (b, n, 0)
    kv_blk = (None, WINDOW, SWA_KV)
    return pl.pallas_call(
        functools.partial(_swa_kernel, norm_prev=prompt, first_has_prev=not prompt),
        grid=(bsz, nb),
        in_specs=[pl.BlockSpec(memory_space=pltpu.SMEM),
                  pl.BlockSpec((None, WINDOW, SWA_Q), cur),
                  pl.BlockSpec(kv_blk, cur), pl.BlockSpec(kv_blk, prev_map),
                  pl.BlockSpec(kv_blk, cur), pl.BlockSpec(kv_blk, prev_map),
                  _const_spec((1, HEAD_DIM)), _const_spec((1, HEAD_DIM))],
        out_specs=[pl.BlockSpec((None, WINDOW, SWA_Q), cur), pl.BlockSpec(kv_blk, cur)],
        out_shape=[jax.ShapeDtypeStruct((bsz, length, SWA_Q), F32),
                   jax.ShapeDtypeStruct((bsz, length, SWA_KV), F32)],
        compiler_params=_cparams(("parallel", "arbitrary")),
        name="swa_attention",
    )(sinks, q, k, k_prev, v, v_prev, q_norm.reshape(1, HEAD_DIM), k_norm.reshape(1, HEAD_DIM))


def _chunk_conv(x_ref, xx_scr, w_ref, rows):
    xx_scr[SUBLANES:SUBLANES + rows, :] = x_ref[...]
    acc = None
    for tap in range(CONV_WIDTH):
        off = SUBLANES - (CONV_WIDTH - 1) + tap
        term = w_ref[tap:tap + 1, :] * xx_scr[off:off + rows, :]
        acc = term if acc is None else acc + term
    return acc


def _carry_conv_tail(xx_scr, rows):
    xx_scr[0:SUBLANES, :] = xx_scr[rows:rows + SUBLANES, :]


def _ssd_kernel(xbc_ref, z_ref, dt_ref, cw_ref, cb_ref, dtb_ref, alog_ref, dd_ref, nrm_ref, tail0_ref, h0_ref,
                y_ref, hout_ref, xx_scr, h_scr, *, n_valid):
    q_len = xbc_ref.shape[0]

    @pl.when(pl.program_id(1) == 0)
    def _():
        xx_scr[0:SUBLANES, :] = tail0_ref[...]
        h_scr[...] = h0_ref[...]

    act = _silu(_chunk_conv(xbc_ref, xx_scr, cw_ref, q_len) + cb_ref[...])
    _carry_conv_tail(xx_scr, q_len)
    xs = act[:, :SSM_INNER]
    bm = act[:, SSM_INNER:SSM_INNER + SSM_GROUPS * SSM_STATE]
    cm = act[:, SSM_INNER + SSM_GROUPS * SSM_STATE:]

    row = lax.broadcasted_iota(jnp.int32, (q_len, q_len), 0)
    col = lax.broadcasted_iota(jnp.int32, (q_len, q_len), 1)
    causal = row >= col
    dt = _softplus(dt_ref[...] + dtb_ref[...])
    if n_valid < q_len:
        dt = jnp.where(lax.broadcasted_iota(jnp.int32, dt.shape, 0) < n_valid, dt, 0.0)
    da = dt * (-jnp.exp(alog_ref[...]))
    cum = jnp.dot(causal.astype(F32), da, preferred_element_type=F32, precision=lax.Precision.HIGHEST)
    cum_t = cum.T
    dt_t = dt.T
    e_cum = jnp.exp(cum)
    hpg = SSM_HEADS // SSM_GROUPS
    gw = hpg * SSM_HEAD_DIM
    ys = []
    for g in range(SSM_GROUPS):
        bm_g = bm[:, g * SSM_STATE:(g + 1) * SSM_STATE].astype(BF16)
        cm_g = cm[:, g * SSM_STATE:(g + 1) * SSM_STATE].astype(BF16)
        cb = lax.dot_general(cm_g, bm_g, NT_DIMS, preferred_element_type=F32)
        h_g = h_scr[g * gw:(g + 1) * gw, :]
        y_state = lax.dot_general(cm_g, h_g.astype(BF16), NT_DIMS, preferred_element_type=F32)
        xt_parts, dec_parts = [], []
        for hh in range(hpg):
            h = g * hpg + hh
            x_h = xs[:, h * SSM_HEAD_DIM:(h + 1) * SSM_HEAD_DIM]
            cum_c = cum[:, h:h + 1]
            seg = jnp.exp(jnp.where(causal, cum_c - cum_t[h:h + 1, :], -jnp.inf))
            wgt = cb * seg * dt_t[h:h + 1, :]
            y = jnp.dot(wgt.astype(BF16), x_h.astype(BF16), preferred_element_type=F32)
            y = y + y_state[:, hh * SSM_HEAD_DIM:(hh + 1) * SSM_HEAD_DIM] * e_cum[:, h:h + 1]
            ys.append(y + dd_ref[0, h] * x_h)
            c_last = cum[q_len - 1:q_len, h:h + 1]
            xt_parts.append(x_h * (jnp.exp(c_last - cum_c) * dt[:, h:h + 1]))
            dec_parts.append(jnp.broadcast_to(jnp.exp(c_last), (SSM_HEAD_DIM, SSM_STATE)))
        xt = jnp.concatenate(xt_parts, axis=1).astype(BF16)
        upd = lax.dot_general(xt, bm_g, TN_DIMS, preferred_element_type=F32)
        h_scr[g * gw:(g + 1) * gw, :] = h_g * jnp.concatenate(dec_parts, axis=0) + upd
    y_all = jnp.concatenate(ys, axis=1) * _silu(z_ref[...])
    nrm = nrm_ref[...]
    y_ref[...] = jnp.concatenate(
        [_rms(y_all[:, g * gw:(g + 1) * gw], nrm[:, g * gw:(g + 1) * gw]) for g in range(SSM_GROUPS)], axis=1)
    hout_ref[...] = h_scr[...]


def ssd_mixer(xbc, z, dt, conv_w, conv_b, dt_bias, a_log, d_skip, norm_g, tail0, h0, n_valid):
    bsz, length = xbc.shape[0], xbc.shape[1]
    nc = length // SSM_CHUNK
    cur = lambda b, c: (b, c, 0)
    per_b = lambda b, c: (b, 0, 0)
    pad8 = lambda v: jnp.pad(v.reshape(1, SSM_HEADS), ((0, 0), (0, LANES - SSM_HEADS)))
    return pl.pallas_call(
        functools.partial(_ssd_kernel, n_valid=n_valid),
        grid=(bsz, nc),
        in_specs=[pl.BlockSpec((None, SSM_CHUNK, SSM_CONV_CH), cur),
                  pl.BlockSpec((None, SSM_CHUNK, SSM_INNER), cur),
                  pl.BlockSpec((None, SSM_CHUNK, LANES), cur),
                  _const_spec((CONV_WIDTH, SSM_CONV_CH)), _const_spec((1, SSM_CONV_CH)),
                  _const_spec((1, LANES)), _const_spec((1, LANES)),
                  pl.BlockSpec(memory_space=pltpu.SMEM),
                  _const_spec((1, SSM_INNER)),
                  pl.BlockSpec((None, SUBLANES, SSM_CONV_CH), per_b),
                  pl.BlockSpec((None, SSM_INNER, SSM_STATE), per_b)],
        out_specs=[pl.BlockSpec((None, SSM_CHUNK, SSM_INNER), cur),
                   pl.BlockSpec((None, SSM_INNER, SSM_STATE), per_b)],
        out_shape=[jax.ShapeDtypeStruct((bsz, length, SSM_INNER), F32),
                   jax.ShapeDtypeStruct((bsz, SSM_INNER, SSM_STATE), F32)],
        scratch_shapes=[pltpu.VMEM((SSM_CHUNK + SUBLANES, SSM_CONV_CH), F32),
                        pltpu.VMEM((SSM_INNER, SSM_STATE), F32)],
        compiler_params=_cparams(("parallel", "arbitrary")),
        name="ssd_mixer",
    )(xbc, z, dt, conv_w, conv_b.reshape(1, SSM_CONV_CH), pad8(dt_bias), pad8(a_log),
      d_skip.reshape(1, SSM_HEADS), norm_g.reshape(1, SSM_INNER), tail0, h0)


def _unit_lower_inverses(lmats, row, col):
    c = lmats[0].shape[0]
    mm = lambda a, b: jnp.dot(a.astype(BF16), b.astype(BF16), preferred_element_type=F32)
    eye = (row == col).astype(F32)
    blk = SUBLANES
    same = (row // blk) == (col // blk)
    xs = [jnp.where(same, -l, 0.0) for l in lmats]
    invs = [eye + x for x in xs]
    p = blk
    while p > 2:
        xs = [mm(x, x) for x in xs]
        invs = [i + mm(i, x) for i, x in zip(invs, xs)]
        p //= 2
    while blk < c:
        outer = ((row // (2 * blk)) == (col // (2 * blk))) & ((row // blk) != (col // blk))
        ts = [mm(i, jnp.where(outer, l, 0.0)) for i, l in zip(invs, lmats)]
        invs = [i - mm(t, i) for i, t in zip(invs, ts)]
        blk *= 2
    return invs


def _dn_kernel(qkv_ref, z_ref, ba_ref, cw_ref, dtb_ref, alog_ref, nrm_ref, tail0_ref, s0_ref,
               o_ref, sout_ref, xx_scr, s_scr, *, n_valid):
    nseq, c_len = qkv_ref.shape[0], qkv_ref.shape[1]
    seqs = range(nseq)
    chains = [(s, h) for s in seqs for h in range(DN_HEADS)]
    ids = range(len(chains))

    @pl.when(pl.program_id(1) == 0)
    def _():
        xx_scr[:, 0:SUBLANES, :] = tail0_ref[...]
        s_scr[...] = s0_ref[...]

    row = lax.broadcasted_iota(jnp.int32, (c_len, c_len), 0)
    col = lax.broadcasted_iota(jnp.int32, (c_len, c_len), 1)
    incl = row >= col
    strict = row > col
    tri = incl.astype(F32)
    nrm = nrm_ref[...]
    bf = lambda t: t.astype(BF16)
    mm = lambda a, b: jnp.dot(bf(a), bf(b), preferred_element_type=F32)

    act, beta, cum, cum_t, e_cum, e_rest, e_last = [], [], [], [], [], [], []
    for s in seqs:
        act.append(_silu(_chunk_conv(qkv_ref.at[s], xx_scr.at[s], cw_ref, c_len)))
        _carry_conv_tail(xx_scr.at[s], c_len)
        ba = ba_ref[s]
        beta_s = _sigmoid(ba)
        gate = -jnp.exp(alog_ref[...]) * _softplus(ba + dtb_ref[...])
        if n_valid < c_len:
            valid = lax.broadcasted_iota(jnp.int32, ba.shape, 0) < n_valid
            beta_s = jnp.where(valid, beta_s, 0.0)
            gate = jnp.where(valid, gate, 0.0)
        beta.append(beta_s)
        cum_s = jnp.dot(tri, gate, preferred_element_type=F32, precision=lax.Precision.HIGHEST)
        c_last = cum_s[c_len - 1:c_len, :]
        cum.append(cum_s)
        cum_t.append(cum_s.T)
        e_cum.append(jnp.exp(cum_s))
        e_rest.append(jnp.exp(c_last - cum_s))
        e_last.append(jnp.exp(c_last))

    q, k, kb, rhs, decay = [], [], [], [], []
    for i in ids:
        s, h = chains[i]
        gl = DN_HEADS + h
        q_h = act[s][:, h * DN_DK:(h + 1) * DN_DK]
        k_h = act[s][:, DN_QK + h * DN_DK:DN_QK + (h + 1) * DN_DK]
        v_h = act[s][:, 2 * DN_QK + h * DN_DV:2 * DN_QK + (h + 1) * DN_DV]
        q.append(q_h * lax.rsqrt(jnp.sum(q_h * q_h, axis=-1, keepdims=True) + EPS) * (DN_DK ** -0.5))
        k.append(k_h * lax.rsqrt(jnp.sum(k_h * k_h, axis=-1, keepdims=True) + EPS))
        beta_c = beta[s][:, h:h + 1]
        kb.append(k[i] * beta_c)
        rhs.append(jnp.concatenate([v_h * beta_c, kb[i] * e_cum[s][:, gl:gl + 1]], axis=1))
        decay.append(jnp.exp(jnp.where(incl, cum[s][:, gl:gl + 1] - cum_t[s][gl:gl + 1, :], -jnp.inf)))
    kq = [lax.dot_general(bf(jnp.concatenate([kb[i], q[i]], axis=0)), bf(k[i]), NT_DIMS,
                          preferred_element_type=F32) for i in ids]
    lmat = [jnp.where(strict, kq[i][:c_len] * decay[i], 0.0) for i in ids]
    attn = [kq[i][c_len:] * decay[i] for i in ids]
    tinv = _unit_lower_inverses(lmat, row, col)
    uw = [mm(tinv[i], rhs[i]) for i in ids]
    s_old = [s_scr[s, h * DN_DK:(h + 1) * DN_DK, :] for s, h in chains]
    qe = [q[i] * e_cum[s][:, DN_HEADS + h:DN_HEADS + h + 1] for i, (s, h) in enumerate(chains)]
    wq = [mm(jnp.concatenate([uw[i][:, DN_DV:], qe[i]], axis=0), s_old[i]) for i in ids]
    v_new = [uw[i][:, :DN_DV] - wq[i][:c_len] for i in ids]
    o = [wq[i][c_len:] + mm(attn[i], v_new[i]) for i in ids]
    for i in ids:
        s, h = chains[i]
        gl = DN_HEADS + h
        s_scr[s, h * DN_DK:(h + 1) * DN_DK, :] = s_old[i] * e_last[s][:, gl:gl + 1] + lax.dot_general(
            bf(k[i] * e_rest[s][:, gl:gl + 1]), bf(v_new[i]), TN_DIMS, preferred_element_type=F32)
    for s in seqs:
        z = z_ref[s]
        o_ref[s] = jnp.concatenate(
            [_rms(o[s * DN_HEADS + h], nrm) * _silu(z[:, h * DN_DV:(h + 1) * DN_DV]) for h in range(DN_HEADS)],
            axis=1)
    sout_ref[...] = s_scr[...]


def dn_mixer(qkv, z, ba, conv_w, dt_bias, a_log, norm_g, tail0, s0, n_valid, nseq):
    bsz, length = qkv.shape[0], qkv.shape[1]
    nc = length // DN_CHUNK
    cur = lambda b, c: (b, c, 0)
    per_b = lambda b, c: (b, 0, 0)
    pad_a = lambda v: jnp.pad(v.reshape(1, DN_HEADS), ((0, 0), (DN_HEADS, LANES - 2 * DN_HEADS)))
    return pl.pallas_call(
        functools.partial(_dn_kernel, n_valid=n_valid),
        grid=(bsz // nseq, nc),
        in_specs=[pl.BlockSpec((nseq, DN_CHUNK, DN_CONV_CH), cur),
                  pl.BlockSpec((nseq, DN_CHUNK, DN_VW), cur),
                  pl.BlockSpec((nseq, DN_CHUNK, LANES), cur),
                  _const_spec((CONV_WIDTH, DN_CONV_CH)),
                  _const_spec((1, LANES)), _const_spec((1, LANES)), _const_spec((1, DN_DV)),
                  pl.BlockSpec((nseq, SUBLANES, DN_CONV_CH), per_b),
                  pl.BlockSpec((nseq, DN_QK, DN_DV), per_b)],
        out_specs=[pl.BlockSpec((nseq, DN_CHUNK, DN_VW), cur),
                   pl.BlockSpec((nseq, DN_QK, DN_DV), per_b)],
        out_shape=[jax.ShapeDtypeStruct((bsz, length, DN_VW), F32),
                   jax.ShapeDtypeStruct((bsz, DN_QK, DN_DV), F32)],
        scratch_shapes=[pltpu.VMEM((nseq, DN_CHUNK + SUBLANES, DN_CONV_CH), F32),
                        pltpu.VMEM((nseq, DN_QK, DN_DV), F32)],
        compiler_params=_cparams(("parallel", "arbitrary")),
        name="dn_mixer",
    )(qkv, z, ba, conv_w, pad_a(dt_bias), pad_a(a_log), norm_g.reshape(1, DN_DV), tail0, s0)


X_SUB = D_MODEL // LANES
ROW_SUB = 2 * SUBLANES


def _rows_to_slabs(x):
    parts = jnp.stack([x[:, s * LANES:(s + 1) * LANES] for s in range(X_SUB)], axis=0)
    return jnp.transpose(parts, (1, 0, 2))


def _slabs_to_rows(slab):
    parts = jnp.transpose(slab, (1, 0, 2))
    return jnp.concatenate([parts[s] for s in range(X_SUB)], axis=1)


def _router_kernel(x_ref, g_ref, wr_ref, br_ref, xg_ref, grp_ref, rank_ref, cnt_ref, carry_scr):
    tm = x_ref.shape[0]

    @pl.when(pl.program_id(0) == 0)
    def _():
        carry_scr[...] = jnp.zeros_like(carry_scr)

    x = x_ref[...]
    h = _rms(x, g_ref[...])
    logits = jnp.dot(h, wr_ref[...], preferred_element_type=F32, precision=lax.Precision.HIGHEST) + br_ref[...]
    lt = logits.T
    sub = lax.broadcasted_iota(jnp.int32, (SUBLANES, tm), 0)
    neg = -jnp.inf
    glog = jnp.where(sub < N_GROUPS, lt[N_EXPERTS:N_EXPERTS + SUBLANES], neg)
    gmax = jnp.max(glog, axis=0, keepdims=True)
    g_p = 1.0 / jnp.sum(jnp.exp(glog - gmax), axis=0, keepdims=True)
    g_i = jnp.min(jnp.where(glog == gmax, sub, SUBLANES), axis=0, keepdims=True)
    sel = lt[0:EXPERTS_PER_GROUP]
    for g in range(1, N_GROUPS):
        sel = jnp.where(g_i == g, lt[g * EXPERTS_PER_GROUP:(g + 1) * EXPERTS_PER_GROUP], sel)
    m1 = jnp.max(sel, axis=0, keepdims=True)
    zsum = jnp.sum(jnp.exp(sel - m1), axis=0, keepdims=True)
    i1 = jnp.min(jnp.where(sel == m1, sub, SUBLANES), axis=0, keepdims=True)
    sel2 = jnp.where(sub == i1, neg, sel)
    m2 = jnp.max(sel2, axis=0, keepdims=True)
    i2 = jnp.min(jnp.where(sel2 == m2, sub, SUBLANES), axis=0, keepdims=True)
    p1 = 1.0 / zsum
    p2 = jnp.exp(m2 - m1) / zsum
    gate1 = g_p * p1 / (p1 + p2)
    gate2 = g_p * p2 / (p1 + p2)
    gates = jnp.where(sub == i1, gate1, 0.0) + jnp.where(sub == i2, gate2, 0.0)

    onehot = (sub == g_i).astype(F32)
    before = (lax.broadcasted_iota(jnp.int32, (tm, tm), 0) < lax.broadcasted_iota(jnp.int32, (tm, tm), 1))
    rank_in_tile = jnp.dot(onehot.astype(BF16), before.astype(BF16), preferred_element_type=F32)
    carry = carry_scr[...]
    rank = jnp.sum(onehot * (rank_in_tile + carry[:, 0:1]), axis=0, keepdims=True)
    grp_ref[...] = g_i
    rank_ref[...] = rank.astype(jnp.int32)
    new_carry = carry + jnp.sum(onehot, axis=1, keepdims=True)
    carry_scr[...] = new_carry
    cnt_ref[...] = new_carry

    gates_rows = jnp.concatenate([gates, jnp.zeros((LANES - SUBLANES, tm), F32)], axis=0).T
    xg_ref[:, :X_SUB, :] = _rows_to_slabs(x)
    xg_ref[:, X_SUB:, :] = _rows_to_slabs(jnp.concatenate(
        [gates_rows, jnp.zeros((tm, D_MODEL - LANES), F32)], axis=1))


def moe_router(x, gain, w_router_pad, b_router_pad, tm):
    t = x.shape[0]
    return pl.pallas_call(
        _router_kernel,
        grid=(t // tm,),
        in_specs=[pl.BlockSpec((tm, D_MODEL), lambda i: (i, 0)),
                  _const_spec((1, D_MODEL)), _const_spec((D_MODEL, LANES)), _const_spec((1, LANES))],
        out_specs=[pl.BlockSpec((tm, ROW_SUB, LANES), lambda i: (i, 0, 0)),
                   pl.BlockSpec((None, 1, tm), lambda i: (i, 0, 0)),
                   pl.BlockSpec((None, 1, tm), lambda i: (i, 0, 0)),
                   _const_spec((SUBLANES, LANES))],
        out_shape=[jax.ShapeDtypeStruct((t, ROW_SUB, LANES), F32),
                   jax.ShapeDtypeStruct((t // tm, 1, tm), jnp.int32),
                   jax.ShapeDtypeStruct((t // tm, 1, tm), jnp.int32),
                   jax.ShapeDtypeStruct((SUBLANES, LANES), F32)],
        scratch_shapes=[pltpu.VMEM((SUBLANES, LANES), F32)],
        compiler_params=_cparams(("arbitrary",)),
        name="moe_router",
    )(x, gain.reshape(1, D_MODEL), w_router_pad, b_router_pad)


def _dispatch_kernel(pos_ref, pad_lo_ref, pad_hi_ref, xg_ref, xs_hbm, zero_scr, sem, zsem):
    tm = xg_ref.shape[0]

    def issue(r, carry):
        pltpu.make_async_copy(xg_ref.at[r], xs_hbm.at[pos_ref[0, r]], sem).start()
        return carry

    lax.fori_loop(0, tm, issue, 0, unroll=8)
    pltpu.make_async_copy(xg_ref, xs_hbm.at[pl.ds(0, tm)], sem).wait()

    @pl.when(pl.program_id(0) == pl.num_programs(0) - 1)
    def _():
        zero_scr[...] = jnp.zeros_like(zero_scr)
        for k in range(pad_lo_ref.shape[0]):
            def fill(j, carry):
                pltpu.make_async_copy(zero_scr, xs_hbm.at[j], zsem).start()
                return carry

            def drain(j, carry):
                pltpu.make_async_copy(zero_scr, xs_hbm.at[j], zsem).wait()
                return carry

            lax.fori_loop(pad_lo_ref[k], pad_hi_ref[k], fill, 0)
            lax.fori_loop(pad_lo_ref[k], pad_hi_ref[k], drain, 0)


def moe_dispatch(pos, pad_lo, pad_hi, xg, tm, n_slots):
    t = xg.shape[0]
    return pl.pallas_call(
        _dispatch_kernel,
        grid=(t // tm,),
        in_specs=[pl.BlockSpec((None, 1, tm), lambda i: (i, 0, 0), memory_space=pltpu.SMEM),
                  pl.BlockSpec(memory_space=pltpu.SMEM), pl.BlockSpec(memory_space=pltpu.SMEM),
                  pl.BlockSpec((tm, ROW_SUB, LANES), lambda i: (i, 0, 0))],
        out_specs=pl.BlockSpec(memory_space=pl.ANY),
        out_shape=jax.ShapeDtypeStruct((n_slots, ROW_SUB, LANES), F32),
        scratch_shapes=[pltpu.VMEM((ROW_SUB, LANES), F32), pltpu.SemaphoreType.DMA, pltpu.SemaphoreType.DMA],
        compiler_params=pltpu.CompilerParams(dimension_semantics=("arbitrary",), has_side_effects=True,
                                             vmem_limit_bytes=VMEM_LIMIT),
        name="moe_dispatch",
    )(pos, pad_lo, pad_hi, xg)


def _collect_kernel(pos_ref, os_hbm, out_ref, buf, sem):
    tm = out_ref.shape[0]

    def issue(r, carry):
        pltpu.make_async_copy(os_hbm.at[pos_ref[0, r]], buf.at[r], sem).start()
        return carry

    lax.fori_loop(0, tm, issue, 0, unroll=8)
    pltpu.make_async_copy(os_hbm.at[pl.ds(0, tm)], buf, sem).wait()
    out_ref[...] = _slabs_to_rows(buf[...])


def moe_collect(pos, os_sorted, t, tm):
    return pl.pallas_call(
        _collect_kernel,
        grid=(t // tm,),
        in_specs=[pl.BlockSpec((None, 1, tm), lambda i: (i, 0, 0), memory_space=pltpu.SMEM),
                  pl.BlockSpec(memory_space=pl.ANY)],
        out_specs=pl.BlockSpec((tm, D_MODEL), lambda i: (i, 0)),
        out_shape=jax.ShapeDtypeStruct((t, D_MODEL), F32),
        scratch_shapes=[pltpu.VMEM((tm, X_SUB, LANES), F32), pltpu.SemaphoreType.DMA],
        compiler_params=pltpu.CompilerParams(dimension_semantics=("arbitrary",), has_side_effects=True,
                                             vmem_limit_bytes=VMEM_LIMIT),
        name="moe_collect",
    )(pos, os_sorted)


def _group_experts_kernel(grp_ref, xg_ref, g_ref, wg_ref, wu_ref, wd_ref, o_ref):
    del grp_ref
    x = _slabs_to_rows(xg_ref[:, :X_SUB, :])
    gates = _slabs_to_rows(xg_ref[:, X_SUB:, :])[:, :LANES]
    h = _rms(x, g_ref[...]).astype(BF16)
    acc = x
    for e in range(EXPERTS_PER_GROUP):
        gate = jnp.dot(h, wg_ref[e], preferred_element_type=F32)
        up = jnp.dot(h, wu_ref[e], preferred_element_type=F32)
        act = (_silu(gate) * up).astype(BF16)
        acc = acc + gates[:, e:e + 1] * jnp.dot(act, wd_ref[e], preferred_element_type=F32)
    o_ref[...] = _rows_to_slabs(acc)


def moe_group_experts(tile_grp, xs, gain, wg, wu, wd, tile):
    n_steps = xs.shape[0] // tile
    rows = lambda i, grp: (i, 0, 0)
    wts = lambda i, grp: (grp[i], 0, 0, 0)
    return pl.pallas_call(
        _group_experts_kernel,
        grid_spec=pltpu.PrefetchScalarGridSpec(
            num_scalar_prefetch=1,
            grid=(n_steps,),
            in_specs=[pl.BlockSpec((tile, ROW_SUB, LANES), rows),
                      pl.BlockSpec((1, D_MODEL), lambda i, grp: (0, 0)),
                      pl.BlockSpec((None, EXPERTS_PER_GROUP, D_MODEL, EXPERT_FF), wts),
                      pl.BlockSpec((None, EXPERTS_PER_GROUP, D_MODEL, EXPERT_FF), wts),
                      pl.BlockSpec((None, EXPERTS_PER_GROUP, EXPERT_FF, D_MODEL), wts)],
            out_specs=pl.BlockSpec((tile, X_SUB, LANES), rows)),
        out_shape=jax.ShapeDtypeStruct((xs.shape[0], X_SUB, LANES), F32),
        compiler_params=_cparams(("arbitrary",)),
        name="moe_group_experts",
    )(tile_grp, xs, gain.reshape(1, D_MODEL), wg, wu, wd)


def moe_layer(x, gain, w_route, b_route, wg, wu, wd):
    t = x.shape[0]
    tile = MOE_TILE if t % MOE_TILE == 0 else t
    n_tiles = t // tile + N_GROUPS
    xg, grp, rank, cnt = moe_router(x, gain, w_route, b_route, tile)
    counts = cnt[:N_GROUPS, 0].astype(jnp.int32)
    tile_ends = jnp.cumsum((counts + tile - 1) // tile)
    offs = (tile_ends - (counts + tile - 1) // tile) * tile
    pos = rank + sum(jnp.where(grp == g, offs[g], 0) for g in range(N_GROUPS))
    pad_lo = jnp.concatenate([offs + counts, tile_ends[-1:] * tile])
    pad_hi = jnp.concatenate([tile_ends * tile, jnp.full((1,), n_tiles * tile, jnp.int32)])
    tile_grp = jnp.minimum(jnp.sum((jnp.arange(n_tiles, dtype=jnp.int32)[:, None] >= tile_ends[None, :])
                                   .astype(jnp.int32), axis=1), N_GROUPS - 1)
    xs = moe_dispatch(pos, pad_lo, pad_hi, xg, tile, n_tiles * tile)
    os_sorted = moe_group_experts(tile_grp, xs, gain, wg, wu, wd, tile)
    return moe_collect(pos, os_sorted, t, tile)


def _pad_cols(w, n_pad):
    return jnp.pad(w, ((0, 0), (0, n_pad - w.shape[1])))


def _pad_seq(u, rows):
    return jnp.pad(u[:, None, :], ((0, 0), (0, rows - 1), (0, 0)))


def _conv_tail(buf):
    return jnp.pad(buf, ((0, 0), (SUBLANES - (CONV_WIDTH - 1), 0), (0, 0)))


EVEN_SEGS = ((0, 512), (512, 640), (640, 768), (768, 1280), (1280, 2304), (2304, 2432))
ODD_SEGS = ((0, 3072), (3072, 4096), (4096, 4224))


def _trunk(x_seq, p, states, prompt):
    bsz, length = x_seq.shape[0], x_seq.shape[1]
    t = bsz * length
    tm = 512 if t % 512 == 0 else t
    x = x_seq.reshape(t, D_MODEL)
    k_win, v_win, ssm_h, ssm_conv, dn_s, dn_conv = states

    q, k, v, z, xbc, dt = prenorm_proj(x, p['ln_mix'][0], p['w_in_even'], EVEN_SEGS, tm)
    if prompt:
        seq = lambda u: u.reshape(bsz, length, u.shape[-1])
        q3, k3, v3, z3, xbc3, dt3 = seq(q), seq(k), seq(v), seq(z), seq(xbc), seq(dt)
        att, k_normed = swa_attention(q3, k3, v3, k3, v3, p['q_norm'], p['k_norm'], p['attn_sinks'], prompt=True)
        new_k = k_normed[:, -WINDOW:].reshape(bsz, WINDOW, SWA_KV_HEADS, HEAD_DIM)
        new_v = v3[:, -WINDOW:].reshape(bsz, WINDOW, SWA_KV_HEADS, HEAD_DIM)
        new_ssm_conv = xbc3[:, -(CONV_WIDTH - 1):]
        n_valid_ssm, n_valid_dn = SSM_CHUNK, DN_CHUNK
    else:
        q3, k3, v3 = _pad_seq(q, WINDOW), _pad_seq(k, WINDOW), _pad_seq(v, WINDOW)
        z3, xbc3, dt3 = _pad_seq(z, SSM_CHUNK), _pad_seq(xbc, SSM_CHUNK), _pad_seq(dt, SSM_CHUNK)
        kb = k_win.reshape(bsz, WINDOW, SWA_KV)
        vb = v_win.reshape(bsz, WINDOW, SWA_KV)
        att, k_normed = swa_attention(q3, k3, v3, kb, vb, p['q_norm'], p['k_norm'], p['attn_sinks'], prompt=False)
        att = att[:, :1]
        new_k = jnp.concatenate([kb[:, 1:], k_normed[:, :1]], axis=1).reshape(bsz, WINDOW, SWA_KV_HEADS, HEAD_DIM)
        new_v = jnp.concatenate([vb[:, 1:], v3[:, :1]], axis=1).reshape(bsz, WINDOW, SWA_KV_HEADS, HEAD_DIM)
        new_ssm_conv = jnp.concatenate([ssm_conv[:, 1:], xbc[:, None, :]], axis=1)
        n_valid_ssm, n_valid_dn = 1, 1
    y_ssm, new_h = ssd_mixer(xbc3, z3, dt3, p['ssm_conv_w'], p['ssm_conv_b'], p['ssm_dt_bias'], p['ssm_A_log'],
                             p['ssm_D'], p['ssm_norm'], _conv_tail(ssm_conv),
                             ssm_h.reshape(bsz, SSM_INNER, SSM_STATE), n_valid_ssm)
    if not prompt:
        y_ssm = y_ssm[:, :1]
    x = resid_proj(x, [att.reshape(t, SWA_Q), y_ssm.reshape(t, SSM_INNER)],
                   [p['w_out_even'][:SWA_Q], p['w_out_even'][SWA_Q:]], tm)
    x = moe_layer(x, p['ln_ffn'][0], p['w_route'][0], p['b_route'][0],
                  p['moe_w_gate'][0], p['moe_w_up'][0], p['moe_w_down'][0])

    qkv, zz, ba = prenorm_proj(x, p['ln_mix'][1], p['w_in_odd'], ODD_SEGS, tm // 2 if tm >= 512 else tm)
    if prompt:
        qkv3 = qkv.reshape(bsz, length, DN_CONV_CH)
        zz3 = zz.reshape(bsz, length, DN_VW)
        ba3 = ba.reshape(bsz, length, LANES)
        new_dn_conv = qkv3[:, -(CONV_WIDTH - 1):]
    else:
        qkv3, zz3, ba3 = _pad_seq(qkv, DN_CHUNK), _pad_seq(zz, DN_CHUNK), _pad_seq(ba, DN_CHUNK)
        new_dn_conv = jnp.concatenate([dn_conv[:, 1:], qkv[:, None, :]], axis=1)
    o_dn, new_s = dn_mixer(qkv3, zz3, ba3, p['dn_conv_w'], p['dn_dt_bias'], p['dn_A_log'], p['dn_norm'],
                           _conv_tail(dn_conv), dn_s.reshape(bsz, DN_QK, DN_DV), n_valid_dn, DN_SEQS_PER_STEP)
    if not prompt:
        o_dn = o_dn[:, :1]
    x = resid_proj(x, [o_dn.reshape(t, DN_VW)], [p['w_out_odd']], tm)
    x = moe_layer(x, p['ln_ffn'][1], p['w_route'][1], p['b_route'][1],
                  p['moe_w_gate'][1], p['moe_w_up'][1], p['moe_w_down'][1])

    return (x.reshape(bsz, length, D_MODEL), new_k[None], new_v[None],
            new_h.reshape(1, bsz, SSM_HEADS, SSM_HEAD_DIM, SSM_STATE), new_ssm_conv[None],
            new_s.reshape(1, bsz, DN_HEADS, DN_DK, DN_DV), new_dn_conv[None])


def kernel(x_prompt, x_sample, cache_k_win, cache_v_win, state_ssm, state_ssm_conv, state_dn, state_dn_conv,
           ln_mix, ln_ffn, w_in_even, q_norm, k_norm, attn_sinks, ssm_conv_w, ssm_conv_b, ssm_dt_bias,
           ssm_A_log, ssm_D, ssm_norm, w_out_even, w_in_odd, dn_conv_w, dn_dt_bias, dn_A_log, dn_norm,
           w_out_odd, moe_w_group, moe_b_group, moe_w_router, moe_b_router, moe_w_gate, moe_w_up, moe_w_down):
    w_route = _pad_cols(jnp.concatenate([moe_w_router, moe_w_group], axis=-1).reshape(-1, N_EXPERTS + N_GROUPS),
                        LANES).reshape(2, D_MODEL, LANES)
    b_route = _pad_cols(jnp.concatenate([moe_b_router, moe_b_group], axis=-1), LANES).reshape(2, 1, LANES)
    p = {
        'ln_mix': ln_mix, 'ln_ffn': ln_ffn,
        'w_in_even': _pad_cols(w_in_even[0], EVEN_SEGS[-1][1]).astype(BF16),
        'q_norm': q_norm[0], 'k_norm': k_norm[0], 'attn_sinks': attn_sinks[0],
        'ssm_conv_w': ssm_conv_w[0], 'ssm_conv_b': ssm_conv_b[0], 'ssm_dt_bias': ssm_dt_bias[0],
        'ssm_A_log': ssm_A_log[0], 'ssm_D': ssm_D[0], 'ssm_norm': ssm_norm[0],
        'w_out_even': w_out_even[0].astype(BF16),
        'w_in_odd': _pad_cols(w_in_odd[0], ODD_SEGS[-1][1]).astype(BF16),
        'dn_conv_w': dn_conv_w[0], 'dn_dt_bias': dn_dt_bias[0], 'dn_A_log': dn_A_log[0], 'dn_norm': dn_norm[0],
        'w_out_odd': w_out_odd[0].astype(BF16),
        'w_route': w_route, 'b_route': b_route,
        'moe_w_gate': moe_w_gate.astype(BF16).reshape(2, N_GROUPS, EXPERTS_PER_GROUP, D_MODEL, EXPERT_FF),
        'moe_w_up': moe_w_up.astype(BF16).reshape(2, N_GROUPS, EXPERTS_PER_GROUP, D_MODEL, EXPERT_FF),
        'moe_w_down': moe_w_down.astype(BF16).reshape(2, N_GROUPS, EXPERTS_PER_GROUP, EXPERT_FF, D_MODEL),
    }
    bp = x_prompt.shape[0]
    zeros = lambda *s: jnp.zeros(s, F32)
    prompt_states = (None, None, zeros(bp, SSM_INNER, SSM_STATE), zeros(bp, CONV_WIDTH - 1, SSM_CONV_CH),
                     zeros(bp, DN_QK, DN_DV), zeros(bp, CONV_WIDTH - 1, DN_CONV_CH))
    y_p, kp, vp, sp, scp, dnp, dncp = _trunk(x_prompt, p, prompt_states, True)
    sample_states = (cache_k_win[0], cache_v_win[0], state_ssm[0], state_ssm_conv[0], state_dn[0],
                     state_dn_conv[0])
    y_s, ks, vs, ss, scs, dns, dncs = _trunk(x_sample, p, sample_states, False)
    return (y_p, y_s, kp, ks, vp, vs, sp, ss, scp, scs, dnp, dns, dncp, dncs)
```

```python
import functools

import jax
import jax.numpy as jnp
from jax import lax
from jax.experimental import pallas as pl
from jax.experimental.pallas import tpu as pltpu

F32 = jnp.float32
BF16 = jnp.bfloat16
EPS = 1e-6

D_MODEL = 1024
SWA_HEADS = 8
SWA_KV_HEADS = 2
SWA_GROUP = SWA_HEADS // SWA_KV_HEADS
HEAD_DIM = 64
WINDOW = 128
SWA_Q = SWA_HEADS * HEAD_DIM
SWA_KV = SWA_KV_HEADS * HEAD_DIM
SSM_HEADS = 8
SSM_HEAD_DIM = 64
SSM_GROUPS = 2
SSM_STATE = 128
SSM_INNER = SSM_HEADS * SSM_HEAD_DIM
SSM_CHUNK = 128
SSM_CONV_CH = SSM_INNER + 2 * SSM_GROUPS * SSM_STATE
CONV_WIDTH = 4
DN_HEADS = 8
DN_DK = 128
DN_DV = 128
DN_CHUNK = 64
DN_QK = DN_HEADS * DN_DK
DN_VW = DN_HEADS * DN_DV
DN_CONV_CH = 2 * DN_QK + DN_VW
N_GROUPS = 4
EXPERTS_PER_GROUP = 8
N_EXPERTS = N_GROUPS * EXPERTS_PER_GROUP
EXPERT_FF = 256

LANES = 128
SUBLANES = 8
VMEM_LIMIT = 56 * 1024 * 1024

DN_SEQS_PER_STEP = 2
MOE_TILE = 512
DEC_TB = 8

NT_DIMS = (((1,), (1,)), ((), ()))
TN_DIMS = (((0,), (0,)), ((), ()))


def _cparams(sem):
    return pltpu.CompilerParams(dimension_semantics=sem, vmem_limit_bytes=VMEM_LIMIT)


def _const_spec(shape):
    nd = len(shape)
    return pl.BlockSpec(shape, lambda *_: (0,) * nd)


def _sigmoid(x):
    return 1.0 / (1.0 + jnp.exp(-x))


def _silu(x):
    return x * _sigmoid(x)


def _softplus(x):
    return jnp.maximum(x, 0.0) + jnp.log(1.0 + jnp.exp(-jnp.abs(x)))


def _rms(x, gain):
    return x * lax.rsqrt(jnp.mean(x * x, axis=-1, keepdims=True) + EPS) * gain


def _prenorm_proj_kernel(x_ref, g_ref, w_ref, *out_refs, segs):
    h = _rms(x_ref[...], g_ref[...]).astype(BF16)
    for o_ref, (a, b) in zip(out_refs, segs):
        o_ref[...] = jnp.dot(h, w_ref[:, a:b], preferred_element_type=F32)


def prenorm_proj(x, gain, w_bf16, segs, tm):
    t = x.shape[0]
    n_pad = w_bf16.shape[1]
    return pl.pallas_call(
        functools.partial(_prenorm_proj_kernel, segs=segs),
        grid=(t // tm,),
        in_specs=[pl.BlockSpec((tm, D_MODEL), lambda i: (i, 0)),
                  _const_spec((1, D_MODEL)),
                  _const_spec((D_MODEL, n_pad))],
        out_specs=[pl.BlockSpec((tm, b - a), lambda i: (i, 0)) for a, b in segs],
        out_shape=[jax.ShapeDtypeStruct((t, b - a), F32) for a, b in segs],
        compiler_params=_cparams(("parallel",)),
        name="prenorm_proj",
    )(x, gain.reshape(1, D_MODEL), w_bf16)


def _resid_proj_kernel(x_ref, *refs, n_in):
    acc = x_ref[...]
    for a_ref, w_ref in zip(refs[:n_in], refs[n_in:2 * n_in]):
        acc = acc + jnp.dot(a_ref[...].astype(BF16), w_ref[...], preferred_element_type=F32)
    refs[2 * n_in][...] = acc


def resid_proj(x, acts, ws, tm):
    t = x.shape[0]
    n_in = len(acts)
    return pl.pallas_call(
        functools.partial(_resid_proj_kernel, n_in=n_in),
        grid=(t // tm,),
        in_specs=([pl.BlockSpec((tm, D_MODEL), lambda i: (i, 0))]
                  + [pl.BlockSpec((tm, a.shape[1]), lambda i: (i, 0)) for a in acts]
                  + [_const_spec(w.shape) for w in ws]),
        out_specs=pl.BlockSpec((tm, D_MODEL), lambda i: (i, 0)),
        out_shape=jax.ShapeDtypeStruct((t, D_MODEL), F32),
        compiler_params=_cparams(("parallel",)),
        name="resid_proj",
    )(x, *acts, *ws)


def _swa_kernel(sink_ref, q_ref, kc_ref, kp_ref, vc_ref, vp_ref, qn_ref, kn_ref, o_ref, kout_ref, *,
                norm_prev, first_has_prev):
    n = pl.program_id(1)
    blk = q_ref.shape[0]
    q = q_ref[...]
    kc, kp, vc, vp = kc_ref[...], kp_ref[...], vc_ref[...], vp_ref[...]
    qn, kn = qn_ref[...], kn_ref[...]
    row = lax.broadcasted_iota(jnp.int32, (blk, 2 * blk), 0)
    col = lax.broadcasted_iota(jnp.int32, (blk, 2 * blk), 1)
    rel = row + blk - col
    mask = (rel >= 0) & (rel <= WINDOW)
    if not first_has_prev:
        mask = mask & ((n > 0) | (col >= blk))
    outs, kouts = [], []
    for j in range(SWA_KV_HEADS):
        sl = slice(j * HEAD_DIM, (j + 1) * HEAD_DIM)
        kcj = _rms(kc[:, sl], kn)
        kpj = _rms(kp[:, sl], kn) if norm_prev else kp[:, sl]
        kouts.append(kcj)
        kcat = jnp.concatenate([kpj, kcj], axis=0).astype(BF16)
        vcat = jnp.concatenate([vp[:, sl], vc[:, sl]], axis=0).astype(BF16)
        for g in range(SWA_GROUP):
            h = j * SWA_GROUP + g
            qh = _rms(q[:, h * HEAD_DIM:(h + 1) * HEAD_DIM], qn) * (HEAD_DIM ** -0.5)
            s = lax.dot_general(qh.astype(BF16), kcat, NT_DIMS, preferred_element_type=F32)
            s = jnp.where(mask, s, -jnp.inf)
            sink = sink_ref[h]
            m = jnp.maximum(jnp.max(s, axis=-1, keepdims=True), sink)
            p = jnp.exp(s - m)
            p = p / (jnp.sum(p, axis=-1, keepdims=True) + jnp.exp(sink - m))
            outs.append(jnp.dot(p.astype(BF16), vcat, preferred_element_type=F32))
    o_ref[...] = jnp.concatenate(outs, axis=1)
    kout_ref[...] = jnp.concatenate(kouts, axis=1)


def swa_attention(q, k, v, k_prev, v_prev, q_norm, k_norm, sinks, *, prompt):
    bsz, length = q.shape[0], q.shape[1]
    nb = length // WINDOW
    if prompt:
        prev_map = lambda b, n: (b, jnp.maximum(n - 1, 0), 0)
    else:
        prev_map = lambda b, n: (b, 0, 0)
    cur = lambda b, n: (b, n, 0)
    kv_blk = (None, WINDOW, SWA_KV)
    return pl.pallas_call(
        functools.partial(_swa_kernel, norm_prev=prompt, first_has_prev=not prompt),
        grid=(bsz, nb),
        in_specs=[pl.BlockSpec(memory_space=pltpu.SMEM),
                  pl.BlockSpec((None, WINDOW, SWA_Q), cur),
                  pl.BlockSpec(kv_blk, cur), pl.BlockSpec(kv_blk, prev_map),
                  pl.BlockSpec(kv_blk, cur), pl.BlockSpec(kv_blk, prev_map),
                  _const_spec((1, HEAD_DIM)), _const_spec((1, HEAD_DIM))],
        out_specs=[pl.BlockSpec((None, WINDOW, SWA_Q), cur), pl.BlockSpec(kv_blk, cur)],
        out_shape=[jax.ShapeDtypeStruct((bsz, length, SWA_Q), F32),
                   jax.ShapeDtypeStruct((bsz, length, SWA_KV), F32)],
        compiler_params=_cparams(("parallel", "arbitrary")),
        name="swa_attention",
    )(sinks, q, k, k_prev, v, v_prev, q_norm.reshape(1, HEAD_DIM), k_norm.reshape(1, HEAD_DIM))


def _chunk_conv(x_ref, xx_scr, w_ref, rows):
    xx_scr[SUBLANES:SUBLANES + rows, :] = x_ref[...]
    acc = None
    for tap in range(CONV_WIDTH):
        off = SUBLANES - (CONV_WIDTH - 1) + tap
        term = w_ref[tap:tap + 1, :] * xx_scr[off:off + rows, :]
        acc = term if acc is None else acc + term
    return acc


def _carry_conv_tail(xx_scr, rows):
    xx_scr[0:SUBLANES, :] = xx_scr[rows:rows + SUBLANES, :]


def _ssd_kernel(xbc_ref, z_ref, dt_ref, cw_ref, cb_ref, dtb_ref, alog_ref, dd_ref, nrm_ref, tail0_ref, h0_ref,
                y_ref, hout_ref, xx_scr, h_scr, *, n_valid):
    q_len = xbc_ref.shape[0]

    @pl.when(pl.program_id(1) == 0)
    def _():
        xx_scr[0:SUBLANES, :] = tail0_ref[...]
        h_scr[...] = h0_ref[...]

    act = _silu(_chunk_conv(xbc_ref, xx_scr, cw_ref, q_len) + cb_ref[...])
    _carry_conv_tail(xx_scr, q_len)
    xs = act[:, :SSM_INNER]
    bm = act[:, SSM_INNER:SSM_INNER + SSM_GROUPS * SSM_STATE]
    cm = act[:, SSM_INNER + SSM_GROUPS * SSM_STATE:]

    row = lax.broadcasted_iota(jnp.int32, (q_len, q_len), 0)
    col = lax.broadcasted_iota(jnp.int32, (q_len, q_len), 1)
    causal = row >= col
    dt = _softplus(dt_ref[...] + dtb_ref[...])
    if n_valid < q_len:
        dt = jnp.where(lax.broadcasted_iota(jnp.int32, dt.shape, 0) < n_valid, dt, 0.0)
    da = dt * (-jnp.exp(alog_ref[...]))
    cum = jnp.dot(causal.astype(F32), da, preferred_element_type=F32, precision=lax.Precision.HIGHEST)
    cum_t = cum.T
    dt_t = dt.T
    e_cum = jnp.exp(cum)
    hpg = SSM_HEADS // SSM_GROUPS
    gw = hpg * SSM_HEAD_DIM
    ys = []
    for g in range(SSM_GROUPS):
        bm_g = bm[:, g * SSM_STATE:(g + 1) * SSM_STATE].astype(BF16)
        cm_g = cm[:, g * SSM_STATE:(g + 1) * SSM_STATE].astype(BF16)
        cb = lax.dot_general(cm_g, bm_g, NT_DIMS, preferred_element_type=F32)
        h_g = h_scr[g * gw:(g + 1) * gw, :]
        y_state = lax.dot_general(cm_g, h_g.astype(BF16), NT_DIMS, preferred_element_type=F32)
        xt_parts, dec_parts = [], []
        for hh in range(hpg):
            h = g * hpg + hh
            x_h = xs[:, h * SSM_HEAD_DIM:(h + 1) * SSM_HEAD_DIM]
            cum_c = cum[:, h:h + 1]
            seg = jnp.exp(jnp.where(causal, cum_c - cum_t[h:h + 1, :], -jnp.inf))
            wgt = cb * seg * dt_t[h:h + 1, :]
            y = jnp.dot(wgt.astype(BF16), x_h.astype(BF16), preferred_element_type=F32)
            y = y + y_state[:, hh * SSM_HEAD_DIM:(hh + 1) * SSM_HEAD_DIM] * e_cum[:, h:h + 1]
            ys.append(y + dd_ref[0, h] * x_h)
            c_last = cum[q_len - 1:q_len, h:h + 1]
            xt_parts.append(x_h * (jnp.exp(c_last - cum_c) * dt[:, h:h + 1]))
            dec_parts.append(jnp.broadcast_to(jnp.exp(c_last), (SSM_HEAD_DIM, SSM_STATE)))
        xt = jnp.concatenate(xt_parts, axis=1).astype(BF16)
        upd = lax.dot_general(xt, bm_g, TN_DIMS, preferred_element_type=F32)
        h_scr[g * gw:(g + 1) * gw, :] = h_g * jnp.concatenate(dec_parts, axis=0) + upd
    y_all = jnp.concatenate(ys, axis=1) * _silu(z_ref[...])
    nrm = nrm_ref[...]
    y_ref[...] = jnp.concatenate(
        [_rms(y_all[:, g * gw:(g + 1) * gw], nrm[:, g * gw:(g + 1) * gw]) for g in range(SSM_GROUPS)], axis=1)
    hout_ref[...] = h_scr[...]


def ssd_mixer(xbc, z, dt, conv_w, conv_b, dt_bias, a_log, d_skip, norm_g, tail0, h0, n_valid):
    bsz, length = xbc.shape[0], xbc.shape[1]
    nc = length // SSM_CHUNK
    cur = lambda b, c: (b, c, 0)
    per_b = lambda b, c: (b, 0, 0)
    pad8 = lambda v: jnp.pad(v.reshape(1, SSM_HEADS), ((0, 0), (0, LANES - SSM_HEADS)))
    return pl.pallas_call(
        functools.partial(_ssd_kernel, n_valid=n_valid),
        grid=(bsz, nc),
        in_specs=[pl.BlockSpec((None, SSM_CHUNK, SSM_CONV_CH), cur),
                  pl.BlockSpec((None, SSM_CHUNK, SSM_INNER), cur),
                  pl.BlockSpec((None, SSM_CHUNK, LANES), cur),
                  _const_spec((CONV_WIDTH, SSM_CONV_CH)), _const_spec((1, SSM_CONV_CH)),
                  _const_spec((1, LANES)), _const_spec((1, LANES)),
                  pl.BlockSpec(memory_space=pltpu.SMEM),
                  _const_spec((1, SSM_INNER)),
                  pl.BlockSpec((None, SUBLANES, SSM_CONV_CH), per_b),
                  pl.BlockSpec((None, SSM_INNER, SSM_STATE), per_b)],
        out_specs=[pl.BlockSpec((None, SSM_CHUNK, SSM_INNER), cur),
                   pl.BlockSpec((None, SSM_INNER, SSM_STATE), per_b)],
        out_shape=[jax.ShapeDtypeStruct((bsz, length, SSM_INNER), F32),
                   jax.ShapeDtypeStruct((bsz, SSM_INNER, SSM_STATE), F32)],
        scratch_shapes=[pltpu.VMEM((SSM_CHUNK + SUBLANES, SSM_CONV_CH), F32),
                        pltpu.VMEM((SSM_INNER, SSM_STATE), F32)],
        compiler_params=_cparams(("parallel", "arbitrary")),
        name="ssd_mixer",
    )(xbc, z, dt, conv_w, conv_b.reshape(1, SSM_CONV_CH), pad8(dt_bias), pad8(a_log),
      d_skip.reshape(1, SSM_HEADS), norm_g.reshape(1, SSM_INNER), tail0, h0)


def _unit_lower_inverses(lmats, row, col):
    c = lmats[0].shape[0]
    mm = lambda a, b: jnp.dot(a.astype(BF16), b.astype(BF16), preferred_element_type=F32)
    eye = (row == col).astype(F32)
    blk = SUBLANES
    same = (row // blk) == (col // blk)
    xs = [jnp.where(same, -l, 0.0) for l in lmats]
    invs = [eye + x for x in xs]
    p = blk
    while p > 2:
        xs = [mm(x, x) for x in xs]
        invs = [i + mm(i, x) for i, x in zip(invs, xs)]
        p //= 2
    while blk < c:
        outer = ((row // (2 * blk)) == (col // (2 * blk))) & ((row // blk) != (col // blk))
        ts = [mm(i, jnp.where(outer, l, 0.0)) for i, l in zip(invs, lmats)]
        invs = [i - mm(t, i) for i, t in zip(invs, ts)]
        blk *= 2
    return invs


def _dn_kernel(qkv_ref, z_ref, ba_ref, cw_ref, dtb_ref, alog_ref, nrm_ref, tail0_ref, s0_ref,
               o_ref, sout_ref, xx_scr, s_scr, *, n_valid):
    nseq, c_len = qkv_ref.shape[0], qkv_ref.shape[1]
    seqs = range(nseq)
    chains = [(s, h) for s in seqs for h in range(DN_HEADS)]
    ids = range(len(chains))

    @pl.when(pl.program_id(1) == 0)
    def _():
        xx_scr[:, 0:SUBLANES, :] = tail0_ref[...]
        s_scr[...] = s0_ref[...]

    row = lax.broadcasted_iota(jnp.int32, (c_len, c_len), 0)
    col = lax.broadcasted_iota(jnp.int32, (c_len, c_len), 1)
    incl = row >= col
    strict = row > col
    tri = incl.astype(F32)
    nrm = nrm_ref[...]
    bf = lambda t: t.astype(BF16)
    mm = lambda a, b: jnp.dot(bf(a), bf(b), preferred_element_type=F32)

    act, beta, cum, cum_t, e_cum, e_rest, e_last = [], [], [], [], [], [], []
    for s in seqs:
        act.append(_silu(_chunk_conv(qkv_ref.at[s], xx_scr.at[s], cw_ref, c_len)))
        _carry_conv_tail(xx_scr.at[s], c_len)
        ba = ba_ref[s]
        beta_s = _sigmoid(ba)
        gate = -jnp.exp(alog_ref[...]) * _softplus(ba + dtb_ref[...])
        if n_valid < c_len:
            valid = lax.broadcasted_iota(jnp.int32, ba.shape, 0) < n_valid
            beta_s = jnp.where(valid, beta_s, 0.0)
            gate = jnp.where(valid, gate, 0.0)
        beta.append(beta_s)
        cum_s = jnp.dot(tri, gate, preferred_element_type=F32, precision=lax.Precision.HIGHEST)
        c_last = cum_s[c_len - 1:c_len, :]
        cum.append(cum_s)
        cum_t.append(cum_s.T)
        e_cum.append(jnp.exp(cum_s))
        e_rest.append(jnp.exp(c_last - cum_s))
        e_last.append(jnp.exp(c_last))

    q, k, kb, rhs, decay = [], [], [], [], []
    for i in ids:
        s, h = chains[i]
        gl = DN_HEADS + h
        q_h = act[s][:, h * DN_DK:(h + 1) * DN_DK]
        k_h = act[s][:, DN_QK + h * DN_DK:DN_QK + (h + 1) * DN_DK]
        v_h = act[s][:, 2 * DN_QK + h * DN_DV:2 * DN_QK + (h + 1) * DN_DV]
        q.append(q_h * lax.rsqrt(jnp.sum(q_h * q_h, axis=-1, keepdims=True) + EPS) * (DN_DK ** -0.5))
        k.append(k_h * lax.rsqrt(jnp.sum(k_h * k_h, axis=-1, keepdims=True) + EPS))
        beta_c = beta[s][:, h:h + 1]
        kb.append(k[i] * beta_c)
        rhs.append(jnp.concatenate([v_h * beta_c, kb[i] * e_cum[s][:, gl:gl + 1]], axis=1))
        decay.append(jnp.exp(jnp.where(incl, cum[s][:, gl:gl + 1] - cum_t[s][gl:gl + 1, :], -jnp.inf)))
    kq = [lax.dot_general(bf(jnp.concatenate([kb[i], q[i]], axis=0)), bf(k[i]), NT_DIMS,
                          preferred_element_type=F32) for i in ids]
    lmat = [jnp.where(strict, kq[i][:c_len] * decay[i], 0.0) for i in ids]
    attn = [kq[i][c_len:] * decay[i] for i in ids]
    tinv = _unit_lower_inverses(lmat, row, col)
    uw = [mm(tinv[i], rhs[i]) for i in ids]
    s_old = [s_scr[s, h * DN_DK:(h + 1) * DN_DK, :] for s, h in chains]
    qe = [q[i] * e_cum[s][:, DN_HEADS + h:DN_HEADS + h + 1] for i, (s, h) in enumerate(chains)]
    wq = [mm(jnp.concatenate([uw[i][:, DN_DV:], qe[i]], axis=0), s_old[i]) for i in ids]
    v_new = [uw[i][:, :DN_DV] - wq[i][:c_len] for i in ids]
    o = [wq[i][c_len:] + mm(attn[i], v_new[i]) for i in ids]
    for i in ids:
        s, h = chains[i]
        gl = DN_HEADS + h
        s_scr[s, h * DN_DK:(h + 1) * DN_DK, :] = s_old[i] * e_last[s][:, gl:gl + 1] + lax.dot_general(
            bf(k[i] * e_rest[s][:, gl:gl + 1]), bf(v_new[i]), TN_DIMS, preferred_element_type=F32)
    for s in seqs:
        z = z_ref[s]
        o_ref[s] = jnp.concatenate(
            [_rms(o[s * DN_HEADS + h], nrm) * _silu(z[:, h * DN_DV:(h + 1) * DN_DV]) for h in range(DN_HEADS)],
            axis=1)
    sout_ref[...] = s_scr[...]


def dn_mixer(qkv, z, ba, conv_w, dt_bias, a_log, norm_g, tail0, s0, n_valid, nseq):
    bsz, length = qkv.shape[0], qkv.shape[1]
    nc = length // DN_CHUNK
    cur = lambda b, c: (b, c, 0)
    per_b = lambda b, c: (b, 0, 0)
    pad_a = lambda v: jnp.pad(v.reshape(1, DN_HEADS), ((0, 0), (DN_HEADS, LANES - 2 * DN_HEADS)))
    return pl.pallas_call(
        functools.partial(_dn_kernel, n_valid=n_valid),
        grid=(bsz // nseq, nc),
        in_specs=[pl.BlockSpec((nseq, DN_CHUNK, DN_CONV_CH), cur),
                  pl.BlockSpec((nseq, DN_CHUNK, DN_VW), cur),
                  pl.BlockSpec((nseq, DN_CHUNK, LANES), cur),
                  _const_spec((CONV_WIDTH, DN_CONV_CH)),
                  _const_spec((1, LANES)), _const_spec((1, LANES)), _const_spec((1, DN_DV)),
                  pl.BlockSpec((nseq, SUBLANES, DN_CONV_CH), per_b),
                  pl.BlockSpec((nseq, DN_QK, DN_DV), per_b)],
        out_specs=[pl.BlockSpec((nseq, DN_CHUNK, DN_VW), cur),
                   pl.BlockSpec((nseq, DN_QK, DN_DV), per_b)],
        out_shape=[jax.ShapeDtypeStruct((bsz, length, DN_VW), F32),
                   jax.ShapeDtypeStruct((bsz, DN_QK, DN_DV), F32)],
        scratch_shapes=[pltpu.VMEM((nseq, DN_CHUNK + SUBLANES, DN_CONV_CH), F32),
                        pltpu.VMEM((nseq, DN_QK, DN_DV), F32)],
        compiler_params=_cparams(("parallel", "arbitrary")),
        name="dn_mixer",
    )(qkv, z, ba, conv_w, pad_a(dt_bias), pad_a(a_log), norm_g.reshape(1, DN_DV), tail0, s0)


X_SUB = D_MODEL // LANES
ROW_SUB = 2 * SUBLANES


def _rows_to_slabs(x):
    parts = jnp.stack([x[:, s * LANES:(s + 1) * LANES] for s in range(X_SUB)], axis=0)
    return jnp.transpose(parts, (1, 0, 2))


def _slabs_to_rows(slab):
    parts = jnp.transpose(slab, (1, 0, 2))
    return jnp.concatenate([parts[s] for s in range(X_SUB)], axis=1)


def _router_kernel(x_ref, g_ref, wr_ref, br_ref, xg_ref, grp_ref, rank_ref, cnt_ref, carry_scr):
    tm = x_ref.shape[0]

    @pl.when(pl.program_id(0) == 0)
    def _():
        carry_scr[...] = jnp.zeros_like(carry_scr)

    x = x_ref[...]
    h = _rms(x, g_ref[...])
    logits = jnp.dot(h, wr_ref[...], preferred_element_type=F32, precision=lax.Precision.HIGHEST) + br_ref[...]
    lt = logits.T
    sub = lax.broadcasted_iota(jnp.int32, (SUBLANES, tm), 0)
    neg = -jnp.inf
    glog = jnp.where(sub < N_GROUPS, lt[N_EXPERTS:N_EXPERTS + SUBLANES], neg)
    gmax = jnp.max(glog, axis=0, keepdims=True)
    g_p = 1.0 / jnp.sum(jnp.exp(glog - gmax), axis=0, keepdims=True)
    g_i = jnp.min(jnp.where(glog == gmax, sub, SUBLANES), axis=0, keepdims=True)
    sel = lt[0:EXPERTS_PER_GROUP]
    for g in range(1, N_GROUPS):
        sel = jnp.where(g_i == g, lt[g * EXPERTS_PER_GROUP:(g + 1) * EXPERTS_PER_GROUP], sel)
    m1 = jnp.max(sel, axis=0, keepdims=True)
    zsum = jnp.sum(jnp.exp(sel - m1), axis=0, keepdims=True)
    i1 = jnp.min(jnp.where(sel == m1, sub, SUBLANES), axis=0, keepdims=True)
    sel2 = jnp.where(sub == i1, neg, sel)
    m2 = jnp.max(sel2, axis=0, keepdims=True)
    i2 = jnp.min(jnp.where(sel2 == m2, sub, SUBLANES), axis=0, keepdims=True)
    p1 = 1.0 / zsum
    p2 = jnp.exp(m2 - m1) / zsum
    gate1 = g_p * p1 / (p1 + p2)
    gate2 = g_p * p2 / (p1 + p2)
    gates = jnp.where(sub == i1, gate1, 0.0) + jnp.where(sub == i2, gate2, 0.0)

    onehot = (sub == g_i).astype(F32)
    before = (lax.broadcasted_iota(jnp.int32, (tm, tm), 0) < lax.broadcasted_iota(jnp.int32, (tm, tm), 1))
    rank_in_tile = jnp.dot(onehot.astype(BF16), before.astype(BF16), preferred_element_type=F32)
    carry = carry_scr[...]
    rank = jnp.sum(onehot * (rank_in_tile + carry[:, 0:1]), axis=0, keepdims=True)
    grp_ref[...] = g_i
    rank_ref[...] = rank.astype(jnp.int32)
    new_carry = carry + jnp.sum(onehot, axis=1, keepdims=True)
    carry_scr[...] = new_carry
    cnt_ref[...] = new_carry

    gates_rows = jnp.concatenate([gates, jnp.zeros((LANES - SUBLANES, tm), F32)], axis=0).T
    xg_ref[:, :X_SUB, :] = _rows_to_slabs(x)
    xg_ref[:, X_SUB:, :] = _rows_to_slabs(jnp.concatenate(
        [gates_rows, jnp.zeros((tm, D_MODEL - LANES), F32)], axis=1))


def moe_router(x, gain, w_router_pad, b_router_pad, tm):
    t = x.shape[0]
    return pl.pallas_call(
        _router_kernel,
        grid=(t // tm,),
        in_specs=[pl.BlockSpec((tm, D_MODEL), lambda i: (i, 0)),
                  _const_spec((1, D_MODEL)), _const_spec((D_MODEL, LANES)), _const_spec((1, LANES))],
        out_specs=[pl.BlockSpec((tm, ROW_SUB, LANES), lambda i: (i, 0, 0)),
                   pl.BlockSpec((None, 1, tm), lambda i: (i, 0, 0)),
                   pl.BlockSpec((None, 1, tm), lambda i: (i, 0, 0)),
                   _const_spec((SUBLANES, LANES))],
        out_shape=[jax.ShapeDtypeStruct((t, ROW_SUB, LANES), F32),
                   jax.ShapeDtypeStruct((t // tm, 1, tm), jnp.int32),
                   jax.ShapeDtypeStruct((t // tm, 1, tm), jnp.int32),
                   jax.ShapeDtypeStruct((SUBLANES, LANES), F32)],
        scratch_shapes=[pltpu.VMEM((SUBLANES, LANES), F32)],
        compiler_params=_cparams(("arbitrary",)),
        name="moe_router",
    )(x, gain.reshape(1, D_MODEL), w_router_pad, b_router_pad)


def _dispatch_kernel(pos_ref, pad_lo_ref, pad_hi_ref, xg_ref, xs_hbm, zero_scr, sem, zsem):
    tm = xg_ref.shape[0]

    def issue(r, carry):
        pltpu.make_async_copy(xg_ref.at[r], xs_hbm.at[pos_ref[0, r]], sem).start()
        return carry

    lax.fori_loop(0, tm, issue, 0, unroll=8)
    pltpu.make_async_copy(xg_ref, xs_hbm.at[pl.ds(0, tm)], sem).wait()

    @pl.when(pl.program_id(0) == pl.num_programs(0) - 1)
    def _():
        zero_scr[...] = jnp.zeros_like(zero_scr)
        for k in range(pad_lo_ref.shape[0]):
            def fill(j, carry):
                pltpu.make_async_copy(zero_scr, xs_hbm.at[j], zsem).start()
                return carry

            def drain(j, carry):
                pltpu.make_async_copy(zero_scr, xs_hbm.at[j], zsem).wait()
                return carry

            lax.fori_loop(pad_lo_ref[k], pad_hi_ref[k], fill, 0)
            lax.fori_loop(pad_lo_ref[k], pad_hi_ref[k], drain, 0)


def moe_dispatch(pos, pad_lo, pad_hi, xg, tm, n_slots):
    t = xg.shape[0]
    return pl.pallas_call(
        _dispatch_kernel,
        grid=(t // tm,),
        in_specs=[pl.BlockSpec((None, 1, tm), lambda i: (i, 0, 0), memory_space=pltpu.SMEM),
                  pl.BlockSpec(memory_space=pltpu.SMEM), pl.BlockSpec(memory_space=pltpu.SMEM),
                  pl.BlockSpec((tm, ROW_SUB, LANES), lambda i: (i, 0, 0))],
        out_specs=pl.BlockSpec(memory_space=pl.ANY),
        out_shape=jax.ShapeDtypeStruct((n_slots, ROW_SUB, LANES), F32),
        scratch_shapes=[pltpu.VMEM((ROW_SUB, LANES), F32), pltpu.SemaphoreType.DMA, pltpu.SemaphoreType.DMA],
        compiler_params=pltpu.CompilerParams(dimension_semantics=("arbitrary",), has_side_effects=True,
                                             vmem_limit_bytes=VMEM_LIMIT),
        name="moe_dispatch",
    )(pos, pad_lo, pad_hi, xg)


def _collect_kernel(pos_ref, os_hbm, out_ref, buf, sem):
    tm = out_ref.shape[0]

    def issue(r, carry):
        pltpu.make_async_copy(os_hbm.at[pos_ref[0, r]], buf.at[r], sem).start()
        return carry

    lax.fori_loop(0, tm, issue, 0, unroll=8)
    pltpu.make_async_copy(os_hbm.at[pl.ds(0, tm)], buf, sem).wait()
    out_ref[...] = _slabs_to_rows(buf[...])


def moe_collect(pos, os_sorted, t, tm):
    return pl.pallas_call(
        _collect_kernel,
        grid=(t // tm,),
        in_specs=[pl.BlockSpec((None, 1, tm), lambda i: (i, 0, 0), memory_space=pltpu.SMEM),
                  pl.BlockSpec(memory_space=pl.ANY)],
        out_specs=pl.BlockSpec((tm, D_MODEL), lambda i: (i, 0)),
        out_shape=jax.ShapeDtypeStruct((t, D_MODEL), F32),
        scratch_shapes=[pltpu.VMEM((tm, X_SUB, LANES), F32), pltpu.SemaphoreType.DMA],
        compiler_params=pltpu.CompilerParams(dimension_semantics=("arbitrary",), has_side_effects=True,
                                             vmem_limit_bytes=VMEM_LIMIT),
        name="moe_collect",
    )(pos, os_sorted)


def _group_experts_kernel(grp_ref, xg_ref, g_ref, wg_ref, wu_ref, wd_ref, o_ref):
    del grp_ref
    x = _slabs_to_rows(xg_ref[:, :X_SUB, :])
    gates = _slabs_to_rows(xg_ref[:, X_SUB:, :])[:, :LANES]
    h = _rms(x, g_ref[...]).astype(BF16)
    acc = x
    for e in range(EXPERTS_PER_GROUP):
        gate = jnp.dot(h, wg_ref[e], preferred_element_type=F32)
        up = jnp.dot(h, wu_ref[e], preferred_element_type=F32)
        act = (_silu(gate) * up).astype(BF16)
        acc = acc + gates[:, e:e + 1] * jnp.dot(act, wd_ref[e], preferred_element_type=F32)
    o_ref[...] = _rows_to_slabs(acc)


def moe_group_experts(tile_grp, xs, gain, wg, wu, wd, tile):
    n_steps = xs.shape[0] // tile
    rows = lambda i, grp: (i, 0, 0)
    wts = lambda i, grp: (grp[i], 0, 0, 0)
    return pl.pallas_call(
        _group_experts_kernel,
        grid_spec=pltpu.PrefetchScalarGridSpec(
            num_scalar_prefetch=1,
            grid=(n_steps,),
            in_specs=[pl.BlockSpec((tile, ROW_SUB, LANES), rows),
                      pl.BlockSpec((1, D_MODEL), lambda i, grp: (0, 0)),
                      pl.BlockSpec((None, EXPERTS_PER_GROUP, D_MODEL, EXPERT_FF), wts),
                      pl.BlockSpec((None, EXPERTS_PER_GROUP, D_MODEL, EXPERT_FF), wts),
                      pl.BlockSpec((None, EXPERTS_PER_GROUP, EXPERT_FF, D_MODEL), wts)],
            out_specs=pl.BlockSpec((tile, X_SUB, LANES), rows)),
        out_shape=jax.ShapeDtypeStruct((xs.shape[0], X_SUB, LANES), F32),
        compiler_params=_cparams(("arbitrary",)),
        name="moe_group_experts",
    )(tile_grp, xs, gain.reshape(1, D_MODEL), wg, wu, wd)


def moe_layer(x, gain, w_route, b_route, wg, wu, wd):
    t = x.shape[0]
    tile = MOE_TILE if t % MOE_TILE == 0 else t
    n_tiles = t // tile + N_GROUPS
    xg, grp, rank, cnt = moe_router(x, gain, w_route, b_route, tile)
    counts = cnt[:N_GROUPS, 0].astype(jnp.int32)
    tile_ends = jnp.cumsum((counts + tile - 1) // tile)
    offs = (tile_ends - (counts + tile - 1) // tile) * tile
    pos = rank + sum(jnp.where(grp == g, offs[g], 0) for g in range(N_GROUPS))
    pad_lo = jnp.concatenate([offs + counts, tile_ends[-1:] * tile])
    pad_hi = jnp.concatenate([tile_ends * tile, jnp.full((1,), n_tiles * tile, jnp.int32)])
    tile_grp = jnp.minimum(jnp.sum((jnp.arange(n_tiles, dtype=jnp.int32)[:, None] >= tile_ends[None, :])
                                   .astype(jnp.int32), axis=1), N_GROUPS - 1)
    xs = moe_dispatch(pos, pad_lo, pad_hi, xg, tile, n_tiles * tile)
    os_sorted = moe_group_experts(tile_grp, xs, gain, wg, wu, wd, tile)
    return moe_collect(pos, os_sorted, t, tile)


BATCH_NT = (((2,), (2,)), ((0,), (0,)))
BATCH_NN = (((2,), (1,)), ((0,), (0,)))


def _token_major(parts):
    return jnp.transpose(jnp.stack(parts, axis=0), (1, 0, 2))


def _pad_rows(parts, tb, width):
    return parts + [jnp.zeros((tb, width), F32)] * (SUBLANES - len(parts))


def _spread(v, first, n, width):
    tb = v.shape[0]
    return jnp.concatenate([jnp.broadcast_to(v[:, first + i:first + i + 1], (tb, width)) for i in range(n)], axis=1)


def _columns(rows):
    tb, m = rows.shape
    return jnp.concatenate([rows, jnp.zeros((LANES - tb, m), F32)], axis=0).T


def _one_step_conv(x_ref, b0_ref, b1_ref, b2_ref, w_ref):
    return (w_ref[0:1, :] * b0_ref[...] + w_ref[1:2, :] * b1_ref[...] + w_ref[2:3, :] * b2_ref[...]
            + w_ref[3:4, :] * x_ref[...])


def _swa_decode_kernel(sink_ref, q_ref, k_ref, v_ref, kc_ref, vc_ref, qn_ref, kn_ref, o_ref, kout_ref):
    tb = q_ref.shape[0]
    lo = lax.broadcasted_iota(jnp.int32, (tb, LANES), 1) < HEAD_DIM

    def pair_rms(t, gain):
        sq = t * t
        s_lo = jnp.sum(jnp.where(lo, sq, 0.0), axis=-1, keepdims=True)
        s_hi = jnp.sum(jnp.where(lo, 0.0, sq), axis=-1, keepdims=True)
        return t * lax.rsqrt(jnp.where(lo, s_lo, s_hi) * (1.0 / HEAD_DIM) + EPS) * gain

    qn2 = jnp.concatenate([qn_ref[...], qn_ref[...]], axis=1)
    kn2 = jnp.concatenate([kn_ref[...], kn_ref[...]], axis=1)
    k_new = pair_rms(k_ref[...], kn2)
    kout_ref[...] = k_new
    v_new = v_ref[...]
    rows = []
    for h in range(SWA_HEADS):
        t = pair_rms(q_ref[:, (h // 2) * LANES:(h // 2 + 1) * LANES], qn2) * (HEAD_DIM ** -0.5)
        j = h // SWA_GROUP
        if h % 2 != j:
            t = pltpu.roll(t, HEAD_DIM, axis=1)
        rows.append(jnp.where(lo if j == 0 else jnp.logical_not(lo), t, 0.0))
    q8 = _token_major(rows)
    s = lax.dot_general(q8.astype(BF16), kc_ref[...].astype(BF16), BATCH_NT, preferred_element_type=F32)
    s_new = jnp.sum(q8 * k_new[:, None, :], axis=-1, keepdims=True)
    sub = lax.broadcasted_iota(jnp.int32, (1, SWA_HEADS, 1), 1)
    sink = jnp.zeros((1, SWA_HEADS, 1), F32)
    for h in range(SWA_HEADS):
        sink = jnp.where(sub == h, sink_ref[h], sink)
    m = jnp.maximum(jnp.maximum(jnp.max(s, axis=-1, keepdims=True), s_new), sink)
    p = jnp.exp(s - m)
    p_new = jnp.exp(s_new - m)
    den = jnp.sum(p, axis=-1, keepdims=True) + p_new + jnp.exp(sink - m)
    o8 = lax.dot_general((p / den).astype(BF16), vc_ref[...].astype(BF16), BATCH_NN,
                         preferred_element_type=F32) + (p_new / den) * v_new[:, None, :]
    o_h = jnp.transpose(o8, (1, 0, 2))
    tiles = []
    for t in range(SWA_HEADS // 2):
        halves = []
        for h in (2 * t, 2 * t + 1):
            piece = o_h[h]
            if h % 2 != h // SWA_GROUP:
                piece = pltpu.roll(piece, HEAD_DIM, axis=1)
            halves.append(piece)
        tiles.append(jnp.where(lo, halves[0], halves[1]))
    o_ref[...] = jnp.concatenate(tiles, axis=1)


def swa_decode(q, k, v, k_cache, v_cache, q_norm, k_norm, sinks):
    bsz = q.shape[0]
    tok = lambda i: (i, 0)
    tok3 = lambda i: (i, 0, 0)
    return pl.pallas_call(
        _swa_decode_kernel,
        grid=(bsz // DEC_TB,),
        in_specs=[pl.BlockSpec(memory_space=pltpu.SMEM),
                  pl.BlockSpec((DEC_TB, SWA_Q), tok), pl.BlockSpec((DEC_TB, SWA_KV), tok),
                  pl.BlockSpec((DEC_TB, SWA_KV), tok),
                  pl.BlockSpec((DEC_TB, WINDOW, SWA_KV), tok3), pl.BlockSpec((DEC_TB, WINDOW, SWA_KV), tok3),
                  _const_spec((1, HEAD_DIM)), _const_spec((1, HEAD_DIM))],
        out_specs=[pl.BlockSpec((DEC_TB, SWA_Q), tok), pl.BlockSpec((DEC_TB, SWA_KV), tok)],
        out_shape=[jax.ShapeDtypeStruct((bsz, SWA_Q), F32), jax.ShapeDtypeStruct((bsz, SWA_KV), F32)],
        compiler_params=_cparams(("parallel",)),
        name="swa_decode",
    )(sinks, q, k, v, k_cache, v_cache, q_norm.reshape(1, HEAD_DIM), k_norm.reshape(1, HEAD_DIM))


def _ssd_decode_kernel(x_ref, b0_ref, b1_ref, b2_ref, z_ref, dt_ref, cw_ref, cb_ref, dtb_ref, alog_ref, dd_ref,
                       nrm_ref, h0_ref, y_ref, hout_ref):
    tb = x_ref.shape[0]
    act = _silu(_one_step_conv(x_ref, b0_ref, b1_ref, b2_ref, cw_ref) + cb_ref[...])
    xs = act[:, :SSM_INNER]
    bm = act[:, SSM_INNER:SSM_INNER + SSM_GROUPS * SSM_STATE]
    cm = act[:, SSM_INNER + SSM_GROUPS * SSM_STATE:]
    dt = _softplus(dt_ref[...] + dtb_ref[...])
    dec = jnp.exp(dt * (-jnp.exp(alog_ref[...])))
    dt_x = _spread(dt, 0, SSM_HEADS, SSM_HEAD_DIM)
    dec_x = _spread(dec, 0, SSM_HEADS, SSM_HEAD_DIM)
    gw = SSM_INNER // SSM_GROUPS
    grp = lambda t, g: t[:, g * SSM_STATE:(g + 1) * SSM_STATE]
    cb_x = jnp.concatenate([jnp.broadcast_to(jnp.sum(grp(cm, g) * grp(bm, g), axis=-1, keepdims=True), (tb, gw))
                            for g in range(SSM_GROUPS)], axis=1)
    h0 = h0_ref[...]
    c8 = _token_major(_pad_rows([grp(cm, g) for g in range(SSM_GROUPS)], tb, SSM_STATE))
    ys = jnp.transpose(lax.dot_general(c8.astype(BF16), h0.astype(BF16), BATCH_NT, preferred_element_type=F32),
                       (1, 0, 2))
    y_state = jnp.concatenate([ys[g][:, g * gw:(g + 1) * gw] for g in range(SSM_GROUPS)], axis=1)
    y = cb_x * dt_x * xs + y_state * dec_x + dd_ref[...] * xs
    y = y * _silu(z_ref[...])
    nrm = nrm_ref[...]
    y_ref[...] = jnp.concatenate(
        [_rms(y[:, g * gw:(g + 1) * gw], nrm[:, g * gw:(g + 1) * gw]) for g in range(SSM_GROUPS)], axis=1)
    u_cols = _columns(dt_x * xs)
    hpg = SSM_HEADS // SSM_GROUPS
    for b in range(tb):
        for g in range(SSM_GROUPS):
            rows = slice(g * gw, (g + 1) * gw)
            decay = jnp.concatenate([jnp.broadcast_to(dec[b:b + 1, h:h + 1], (SSM_HEAD_DIM, SSM_STATE))
                                     for h in range(g * hpg, (g + 1) * hpg)], axis=0)
            hout_ref[b, rows, :] = h0[b, rows, :] * decay + u_cols[rows, b:b + 1] * grp(bm, g)[b:b + 1, :]


def ssd_decode(xbc, bufs, z, dt, conv_w, conv_b, dt_bias, a_log, d_skip, norm_g, h0):
    bsz = xbc.shape[0]
    tok = lambda i: (i, 0)
    tok3 = lambda i: (i, 0, 0)
    pad8 = lambda v: jnp.pad(v.reshape(1, SSM_HEADS), ((0, 0), (0, LANES - SSM_HEADS)))
    row = lambda w: pl.BlockSpec((DEC_TB, w), tok)
    return pl.pallas_call(
        _ssd_decode_kernel,
        grid=(bsz // DEC_TB,),
        in_specs=[row(SSM_CONV_CH)] * 4 + [row(SSM_INNER), row(LANES),
                  _const_spec((CONV_WIDTH, SSM_CONV_CH)), _const_spec((1, SSM_CONV_CH)),
                  _const_spec((1, LANES)), _const_spec((1, LANES)), _const_spec((1, SSM_INNER)),
                  _const_spec((1, SSM_INNER)),
                  pl.BlockSpec((DEC_TB, SSM_INNER, SSM_STATE), tok3)],
        out_specs=[row(SSM_INNER), pl.BlockSpec((DEC_TB, SSM_INNER, SSM_STATE), tok3)],
        out_shape=[jax.ShapeDtypeStruct((bsz, SSM_INNER), F32),
                   jax.ShapeDtypeStruct((bsz, SSM_INNER, SSM_STATE), F32)],
        compiler_params=_cparams(("parallel",)),
        name="ssd_decode",
    )(xbc, *bufs, z, dt, conv_w, conv_b.reshape(1, SSM_CONV_CH), pad8(dt_bias), pad8(a_log),
      jnp.repeat(d_skip, SSM_HEAD_DIM).reshape(1, SSM_INNER), norm_g.reshape(1, SSM_INNER), h0)


def _dn_decode_kernel(x_ref, b0_ref, b1_ref, b2_ref, z_ref, ba_ref, cw_ref, dtb_ref, alog_ref, nrm_ref, s0_ref,
                      o_ref, sout_ref):
    tb = x_ref.shape[0]
    act = _silu(_one_step_conv(x_ref, b0_ref, b1_ref, b2_ref, cw_ref))
    ba = ba_ref[...]
    beta = _sigmoid(ba)
    eg = jnp.exp(-jnp.exp(alog_ref[...]) * _softplus(ba + dtb_ref[...]))
    z = z_ref[...]
    nrm = nrm_ref[...]
    outs = []
    for h in range(DN_HEADS):
        q_h = act[:, h * DN_DK:(h + 1) * DN_DK]
        k_h = act[:, DN_QK + h * DN_DK:DN_QK + (h + 1) * DN_DK]
        v_h = act[:, 2 * DN_QK + h * DN_DV:2 * DN_QK + (h + 1) * DN_DV]
        q_h = q_h * lax.rsqrt(jnp.sum(q_h * q_h, axis=-1, keepdims=True) + EPS) * (DN_DK ** -0.5)
        k_h = k_h * lax.rsqrt(jnp.sum(k_h * k_h, axis=-1, keepdims=True) + EPS)
        beta_c = beta[:, h:h + 1]
        eg_c = eg[:, DN_HEADS + h:DN_HEADS + h + 1]
        s_h = s0_ref[:, h * DN_DK:(h + 1) * DN_DK, :]
        kq8 = _token_major(_pad_rows([k_h, q_h], tb, DN_DK))
        r = jnp.transpose(lax.dot_general(kq8.astype(BF16), s_h.astype(BF16), BATCH_NN,
                                          preferred_element_type=F32), (1, 0, 2))
        v_new = beta_c * v_h - (beta_c * eg_c) * r[0]
        o_h = eg_c * r[1] + jnp.sum(q_h * k_h, axis=-1, keepdims=True) * v_new
        outs.append(_rms(o_h, nrm) * _silu(z[:, h * DN_DV:(h + 1) * DN_DV]))
        k_cols = _columns(k_h)
        for b in range(tb):
            sout_ref[b, h * DN_DK:(h + 1) * DN_DK, :] = (
                s_h[b] * jnp.broadcast_to(eg_c[b:b + 1, :], (DN_DK, DN_DV)) + k_cols[:, b:b + 1] * v_new[b:b + 1, :])
    o_ref[...] = jnp.concatenate(outs, axis=1)


def dn_decode(qkv, bufs, z, ba, conv_w, dt_bias, a_log, norm_g, s0):
    bsz = qkv.shape[0]
    tok = lambda i: (i, 0)
    tok3 = lambda i: (i, 0, 0)
    pad_a = lambda v: jnp.pad(v.reshape(1, DN_HEADS), ((0, 0), (DN_HEADS, LANES - 2 * DN_HEADS)))
    row = lambda w: pl.BlockSpec((DEC_TB, w), tok)
    return pl.pallas_call(
        _dn_decode_kernel,
        grid=(bsz // DEC_TB,),
        in_specs=[row(DN_CONV_CH)] * 4 + [row(DN_VW), row(LANES),
                  _const_spec((CONV_WIDTH, DN_CONV_CH)),
                  _const_spec((1, LANES)), _const_spec((1, LANES)), _const_spec((1, DN_DV)),
                  pl.BlockSpec((DEC_TB, DN_QK, DN_DV), tok3)],
        out_specs=[row(DN_VW), pl.BlockSpec((DEC_TB, DN_QK, DN_DV), tok3)],
        out_shape=[jax.ShapeDtypeStruct((bsz, DN_VW), F32), jax.ShapeDtypeStruct((bsz, DN_QK, DN_DV), F32)],
        compiler_params=_cparams(("parallel",)),
        name="dn_decode",
    )(qkv, *bufs, z, ba, conv_w, pad_a(dt_bias), pad_a(a_log), norm_g.reshape(1, DN_DV), s0)


def _pad_cols(w, n_pad):
    return jnp.pad(w, ((0, 0), (0, n_pad - w.shape[1])))


def _pad_seq(u, rows):
    return jnp.pad(u[:, None, :], ((0, 0), (0, rows - 1), (0, 0)))


def _conv_tail(buf):
    return jnp.pad(buf, ((0, 0), (SUBLANES - (CONV_WIDTH - 1), 0), (0, 0)))


EVEN_SEGS = ((0, 512), (512, 640), (640, 768), (768, 1280), (1280, 2304), (2304, 2432))
ODD_SEGS = ((0, 3072), (3072, 4096), (4096, 4224))


def _trunk(x_seq, p, states, prompt):
    bsz, length = x_seq.shape[0], x_seq.shape[1]
    t = bsz * length
    tm = 512 if t % 512 == 0 else t
    x = x_seq.reshape(t, D_MODEL)
    k_win, v_win, ssm_h, ssm_conv, dn_s, dn_conv = states

    q, k, v, z, xbc, dt = prenorm_proj(x, p['ln_mix'][0], p['w_in_even'], EVEN_SEGS, tm)
    if prompt:
        seq = lambda u: u.reshape(bsz, length, u.shape[-1])
        q3, k3, v3, z3, xbc3, dt3 = seq(q), seq(k), seq(v), seq(z), seq(xbc), seq(dt)
        att, k_normed = swa_attention(q3, k3, v3, k3, v3, p['q_norm'], p['k_norm'], p['attn_sinks'], prompt=True)
        new_k = k_normed[:, -WINDOW:].reshape(bsz, WINDOW, SWA_KV_HEADS, HEAD_DIM)
        new_v = v3[:, -WINDOW:].reshape(bsz, WINDOW, SWA_KV_HEADS, HEAD_DIM)
        new_ssm_conv = xbc3[:, -(CONV_WIDTH - 1):]
        y_ssm, new_h = ssd_mixer(xbc3, z3, dt3, p['ssm_conv_w'], p['ssm_conv_b'], p['ssm_dt_bias'],
                                 p['ssm_A_log'], p['ssm_D'], p['ssm_norm'], _conv_tail(ssm_conv),
                                 ssm_h.reshape(bsz, SSM_INNER, SSM_STATE), SSM_CHUNK)
    else:
        kb = k_win.reshape(bsz, WINDOW, SWA_KV)
        vb = v_win.reshape(bsz, WINDOW, SWA_KV)
        att, k_normed = swa_decode(q, k, v, kb, vb, p['q_norm'], p['k_norm'], p['attn_sinks'])
        new_k = jnp.concatenate([kb[:, 1:], k_normed[:, None]], axis=1).reshape(bsz, WINDOW, SWA_KV_HEADS, HEAD_DIM)
        new_v = jnp.concatenate([vb[:, 1:], v[:, None]], axis=1).reshape(bsz, WINDOW, SWA_KV_HEADS, HEAD_DIM)
        new_ssm_conv = jnp.concatenate([ssm_conv[:, 1:], xbc[:, None, :]], axis=1)
        y_ssm, new_h = ssd_decode(xbc, [ssm_conv[:, i] for i in range(CONV_WIDTH - 1)], z, dt, p['ssm_conv_w'],
                                  p['ssm_conv_b'], p['ssm_dt_bias'], p['ssm_A_log'], p['ssm_D'], p['ssm_norm'],
                                  ssm_h.reshape(bsz, SSM_INNER, SSM_STATE))
    x = resid_proj(x, [att.reshape(t, SWA_Q), y_ssm.reshape(t, SSM_INNER)],
                   [p['w_out_even'][:SWA_Q], p['w_out_even'][SWA_Q:]], tm)
    x = moe_layer(x, p['ln_ffn'][0], p['w_route'][0], p['b_route'][0],
                  p['moe_w_gate'][0], p['moe_w_up'][0], p['moe_w_down'][0])

    qkv, zz, ba = prenorm_proj(x, p['ln_mix'][1], p['w_in_odd'], ODD_SEGS, tm // 2 if tm >= 512 else tm)
    if prompt:
        qkv3 = qkv.reshape(bsz, length, DN_CONV_CH)
        zz3 = zz.reshape(bsz, length, DN_VW)
        ba3 = ba.reshape(bsz, length, LANES)
        new_dn_conv = qkv3[:, -(CONV_WIDTH - 1):]
        o_dn, new_s = dn_mixer(qkv3, zz3, ba3, p['dn_conv_w'], p['dn_dt_bias'], p['dn_A_log'], p['dn_norm'],
                               _conv_tail(dn_conv), dn_s.reshape(bsz, DN_QK, DN_DV), DN_CHUNK, DN_SEQS_PER_STEP)
    else:
        new_dn_conv = jnp.concatenate([dn_conv[:, 1:], qkv[:, None, :]], axis=1)
        o_dn, new_s = dn_decode(qkv, [dn_conv[:, i] for i in range(CONV_WIDTH - 1)], zz, ba, p['dn_conv_w'],
                                p['dn_dt_bias'], p['dn_A_log'], p['dn_norm'], dn_s.reshape(bsz, DN_QK, DN_DV))
    x = resid_proj(x, [o_dn.reshape(t, DN_VW)], [p['w_out_odd']], tm)
    x = moe_layer(x, p['ln_ffn'][1], p['w_route'][1], p['b_route'][1],
                  p['moe_w_gate'][1], p['moe_w_up'][1], p['moe_w_down'][1])

    return (x.reshape(bsz, length, D_MODEL), new_k[None], new_v[None],
            new_h.reshape(1, bsz, SSM_HEADS, SSM_HEAD_DIM, SSM_STATE), new_ssm_conv[None],
            new_s.reshape(1, bsz, DN_HEADS, DN_DK, DN_DV), new_dn_conv[None])


def kernel(x_prompt, x_sample, cache_k_win, cache_v_win, state_ssm, state_ssm_conv, state_dn, state_dn_conv,
           ln_mix, ln_ffn, w_in_even, q_norm, k_norm, attn_sinks, ssm_conv_w, ssm_conv_b, ssm_dt_bias,
           ssm_A_log, ssm_D, ssm_norm, w_out_even, w_in_odd, dn_conv_w, dn_dt_bias, dn_A_log, dn_norm,
           w_out_odd, moe_w_group, moe_b_group, moe_w_router, moe_b_router, moe_w_gate, moe_w_up, moe_w_down):
    w_route = _pad_cols(jnp.concatenate([moe_w_router, moe_w_group], axis=-1).reshape(-1, N_EXPERTS + N_GROUPS),
                        LANES).reshape(2, D_MODEL, LANES)
    b_route = _pad_cols(jnp.concatenate([moe_b_router, moe_b_group], axis=-1), LANES).reshape(2, 1, LANES)
    p = {
        'ln_mix': ln_mix, 'ln_ffn': ln_ffn,
        'w_in_even': _pad_cols(w_in_even[0], EVEN_SEGS[-1][1]).astype(BF16),
        'q_norm': q_norm[0], 'k_norm': k_norm[0], 'attn_sinks': attn_sinks[0],
        'ssm_conv_w': ssm_conv_w[0], 'ssm_conv_b': ssm_conv_b[0], 'ssm_dt_bias': ssm_dt_bias[0],
        'ssm_A_log': ssm_A_log[0], 'ssm_D': ssm_D[0], 'ssm_norm': ssm_norm[0],
        'w_out_even': w_out_even[0].astype(BF16),
        'w_in_odd': _pad_cols(w_in_odd[0], ODD_SEGS[-1][1]).astype(BF16),
        'dn_conv_w': dn_conv_w[0], 'dn_dt_bias': dn_dt_bias[0], 'dn_A_log': dn_A_log[0], 'dn_norm': dn_norm[0],
        'w_out_odd': w_out_odd[0].astype(BF16),
        'w_route': w_route, 'b_route': b_route,
        'moe_w_gate': moe_w_gate.astype(BF16).reshape(2, N_GROUPS, EXPERTS_PER_GROUP, D_MODEL, EXPERT_FF),
        'moe_w_up': moe_w_up.astype(BF16).reshape(2, N_GROUPS, EXPERTS_PER_GROUP, D_MODEL, EXPERT_FF),
        'moe_w_down': moe_w_down.astype(BF16).reshape(2, N_GROUPS, EXPERTS_PER_GROUP, EXPERT_FF, D_MODEL),
    }
    bp = x_prompt.shape[0]
    zeros = lambda *s: jnp.zeros(s, F32)
    prompt_states = (None, None, zeros(bp, SSM_INNER, SSM_STATE), zeros(bp, CONV_WIDTH - 1, SSM_CONV_CH),
                     zeros(bp, DN_QK, DN_DV), zeros(bp, CONV_WIDTH - 1, DN_CONV_CH))
    y_p, kp, vp, sp, scp, dnp, dncp = _trunk(x_prompt, p, prompt_states, True)
    sample_states = (cache_k_win[0], cache_v_win[0], state_ssm[0], state_ssm_conv[0], state_dn[0],
                     state_dn_conv[0])
    y_s, ks, vs, ss, scs, dns, dncs = _trunk(x_sample, p, sample_states, False)
    return (y_p, y_s, kp, ks, vp, vs, sp, ss, scp, scs, dnp, dns, dncp, dncs)
```

```python
import functools

import jax
import jax.numpy as jnp
from jax import lax
from jax.experimental import pallas as pl
from jax.experimental.pallas import tpu as pltpu

F32 = jnp.float32
BF16 = jnp.bfloat16
EPS = 1e-6

D_MODEL = 1024
SWA_HEADS = 8
SWA_KV_HEADS = 2
SWA_GROUP = SWA_HEADS // SWA_KV_HEADS
HEAD_DIM = 64
WINDOW = 128
SWA_Q = SWA_HEADS * HEAD_DIM
SWA_KV = SWA_KV_HEADS * HEAD_DIM
SSM_HEADS = 8
SSM_HEAD_DIM = 64
SSM_GROUPS = 2
SSM_STATE = 128
SSM_INNER = SSM_HEADS * SSM_HEAD_DIM
SSM_CHUNK = 128
SSM_CONV_CH = SSM_INNER + 2 * SSM_GROUPS * SSM_STATE
CONV_WIDTH = 4
DN_HEADS = 8
DN_DK = 128
DN_DV = 128
DN_CHUNK = 64
DN_QK = DN_HEADS * DN_DK
DN_VW = DN_HEADS * DN_DV
DN_CONV_CH = 2 * DN_QK + DN_VW
N_GROUPS = 4
EXPERTS_PER_GROUP = 8
N_EXPERTS = N_GROUPS * EXPERTS_PER_GROUP
EXPERT_FF = 256

LANES = 128
SUBLANES = 8
VMEM_LIMIT = 56 * 1024 * 1024
DMA_PRIORITIES = 2

DN_SEQS_PER_STEP = 2
MOE_TILE = 512
DEC_TB = 8
ROW_DMA_UNROLL = 8

NT_DIMS = (((1,), (1,)), ((), ()))
TN_DIMS = (((0,), (0,)), ((), ()))


def _cparams(sem, **kw):
    return pltpu.CompilerParams(dimension_semantics=sem, vmem_limit_bytes=VMEM_LIMIT, **kw)


def _const_spec(shape):
    nd = len(shape)
    return pl.BlockSpec(shape, lambda *_: (0,) * nd)


def _sigmoid(x):
    return 1.0 / (1.0 + jnp.exp(-x))


def _silu(x):
    return x * _sigmoid(x)


def _softplus(x):
    return jnp.maximum(x, 0.0) + jnp.log(1.0 + jnp.exp(-jnp.abs(x)))


def _rms(x, gain):
    return x * lax.rsqrt(jnp.mean(x * x, axis=-1, keepdims=True) + EPS) * gain


def _prenorm_proj_kernel(x_ref, g_ref, w_ref, *out_refs, segs):
    h = _rms(x_ref[...], g_ref[...]).astype(BF16)
    for o_ref, (a, b) in zip(out_refs, segs):
        o_ref[...] = jnp.dot(h, w_ref[:, a:b], preferred_element_type=F32)


def prenorm_proj(x, gain, w_bf16, segs, tm):
    t = x.shape[0]
    n_pad = w_bf16.shape[1]
    return pl.pallas_call(
        functools.partial(_prenorm_proj_kernel, segs=segs),
        grid=(t // tm,),
        in_specs=[pl.BlockSpec((tm, D_MODEL), lambda i: (i, 0)),
                  _const_spec((1, D_MODEL)),
                  _const_spec((D_MODEL, n_pad))],
        out_specs=[pl.BlockSpec((tm, b - a), lambda i: (i, 0)) for a, b in segs],
        out_shape=[jax.ShapeDtypeStruct((t, b - a), F32) for a, b in segs],
        compiler_params=_cparams(("parallel",)),
        name="prenorm_proj",
    )(x, gain.reshape(1, D_MODEL), w_bf16)


def _swa_kernel(sink_ref, q_ref, kc_ref, kp_ref, vc_ref, vp_ref, qn_ref, kn_ref, o_ref, kout_ref):
    n = pl.program_id(1)
    blk = q_ref.shape[0]
    q = q_ref[...]
    kc, kp, vc, vp = kc_ref[...], kp_ref[...], vc_ref[...], vp_ref[...]
    qn, kn = qn_ref[...], kn_ref[...]
    row = lax.broadcasted_iota(jnp.int32, (blk, 2 * blk), 0)
    col = lax.broadcasted_iota(jnp.int32, (blk, 2 * blk), 1)
    rel = row + blk - col
    mask = (rel >= 0) & (rel <= WINDOW) & ((n > 0) | (col >= blk))
    outs, kouts = [], []
    for j in range(SWA_KV_HEADS):
        sl = slice(j * HEAD_DIM, (j + 1) * HEAD_DIM)
        kcj = _rms(kc[:, sl], kn)
        kpj = _rms(kp[:, sl], kn)
        kouts.append(kcj)
        kcat = jnp.concatenate([kpj, kcj], axis=0).astype(BF16)
        vcat = jnp.concatenate([vp[:, sl], vc[:, sl]], axis=0).astype(BF16)
        for g in range(SWA_GROUP):
            h = j * SWA_GROUP + g
            qh = _rms(q[:, h * HEAD_DIM:(h + 1) * HEAD_DIM], qn) * (HEAD_DIM ** -0.5)
            s = lax.dot_general(qh.astype(BF16), kcat, NT_DIMS, preferred_element_type=F32)
            s = jnp.where(mask, s, -jnp.inf)
            sink = sink_ref[h]
            m = jnp.maximum(jnp.max(s, axis=-1, keepdims=True), sink)
            p = jnp.exp(s - m)
            p = p / (jnp.sum(p, axis=-1, keepdims=True) + jnp.exp(sink - m))
            outs.append(jnp.dot(p.astype(BF16), vcat, preferred_element_type=F32))
    o_ref[...] = jnp.concatenate(outs, axis=1)
    kout_ref[...] = jnp.concatenate(kouts, axis=1)


def swa_attention(q, k, v, q_norm, k_norm, sinks):
    bsz, length = q.shape[0], q.shape[1]
    nb = length // WINDOW
    prev = lambda b, n: (b, jnp.maximum(n - 1, 0), 0)
    cur = lambda b, n: (b, n, 0)
    kv_blk = (None, WINDOW, SWA_KV)
    return pl.pallas_call(
        _swa_kernel,
        grid=(bsz, nb),
        in_specs=[pl.BlockSpec(memory_space=pltpu.SMEM),
                  pl.BlockSpec((None, WINDOW, SWA_Q), cur),
                  pl.BlockSpec(kv_blk, cur), pl.BlockSpec(kv_blk, prev),
                  pl.BlockSpec(kv_blk, cur), pl.BlockSpec(kv_blk, prev),
                  _const_spec((1, HEAD_DIM)), _const_spec((1, HEAD_DIM))],
        out_specs=[pl.BlockSpec((None, WINDOW, SWA_Q), cur), pl.BlockSpec(kv_blk, cur)],
        out_shape=[jax.ShapeDtypeStruct((bsz, length, SWA_Q), F32),
                   jax.ShapeDtypeStruct((bsz, length, SWA_KV), F32)],
        compiler_params=_cparams(("parallel", "arbitrary")),
        name="swa_attention",
    )(sinks, q, k, k, v, v, q_norm.reshape(1, HEAD_DIM), k_norm.reshape(1, HEAD_DIM))


def _chunk_conv(x_ref, xx_scr, w_ref, rows):
    xx_scr[SUBLANES:SUBLANES + rows, :] = x_ref[...]
    acc = None
    for tap in range(CONV_WIDTH):
        off = SUBLANES - (CONV_WIDTH - 1) + tap
        term = w_ref[tap:tap + 1, :] * xx_scr[off:off + rows, :]
        acc = term if acc is None else acc + term
    return acc


def _carry_conv_tail(xx_scr, rows):
    xx_scr[0:SUBLANES, :] = xx_scr[rows:rows + SUBLANES, :]


def _ssd_kernel(xbc_ref, z_ref, dt_ref, cw_ref, cb_ref, dtb_ref, alog_ref, dd_ref, nrm_ref,
                y_ref, hout_ref, xx_scr, h_scr):
    q_len = xbc_ref.shape[0]

    @pl.when(pl.program_id(1) == 0)
    def _():
        xx_scr[0:SUBLANES, :] = jnp.zeros((SUBLANES, xx_scr.shape[1]), F32)
        h_scr[...] = jnp.zeros_like(h_scr)

    act = _silu(_chunk_conv(xbc_ref, xx_scr, cw_ref, q_len) + cb_ref[...])
    _carry_conv_tail(xx_scr, q_len)
    xs = act[:, :SSM_INNER]
    bm = act[:, SSM_INNER:SSM_INNER + SSM_GROUPS * SSM_STATE]
    cm = act[:, SSM_INNER + SSM_GROUPS * SSM_STATE:]

    row = lax.broadcasted_iota(jnp.int32, (q_len, q_len), 0)
    col = lax.broadcasted_iota(jnp.int32, (q_len, q_len), 1)
    causal = row >= col
    dt = _softplus(dt_ref[...] + dtb_ref[...])
    da = dt * (-jnp.exp(alog_ref[...]))
    cum = jnp.dot(causal.astype(F32), da, preferred_element_type=F32, precision=lax.Precision.HIGHEST)
    cum_t = cum.T
    dt_t = dt.T
    e_cum = jnp.exp(cum)
    hpg = SSM_HEADS // SSM_GROUPS
    gw = hpg * SSM_HEAD_DIM
    ys = []
    for g in range(SSM_GROUPS):
        bm_g = bm[:, g * SSM_STATE:(g + 1) * SSM_STATE].astype(BF16)
        cm_g = cm[:, g * SSM_STATE:(g + 1) * SSM_STATE].astype(BF16)
        cb = lax.dot_general(cm_g, bm_g, NT_DIMS, preferred_element_type=F32)
        h_g = h_scr[g * gw:(g + 1) * gw, :]
        y_state = lax.dot_general(cm_g, h_g.astype(BF16), NT_DIMS, preferred_element_type=F32)
        xt_parts, dec_parts = [], []
        for hh in range(hpg):
            h = g * hpg + hh
            x_h = xs[:, h * SSM_HEAD_DIM:(h + 1) * SSM_HEAD_DIM]
            cum_c = cum[:, h:h + 1]
            seg = jnp.exp(jnp.where(causal, cum_c - cum_t[h:h + 1, :], -jnp.inf))
            wgt = cb * seg * dt_t[h:h + 1, :]
            y = jnp.dot(wgt.astype(BF16), x_h.astype(BF16), preferred_element_type=F32)
            y = y + y_state[:, hh * SSM_HEAD_DIM:(hh + 1) * SSM_HEAD_DIM] * e_cum[:, h:h + 1]
            ys.append(y + dd_ref[0, h] * x_h)
            c_last = cum[q_len - 1:q_len, h:h + 1]
            xt_parts.append(x_h * (jnp.exp(c_last - cum_c) * dt[:, h:h + 1]))
            dec_parts.append(jnp.broadcast_to(jnp.exp(c_last), (SSM_HEAD_DIM, SSM_STATE)))
        xt = jnp.concatenate(xt_parts, axis=1).astype(BF16)
        upd = lax.dot_general(xt, bm_g, TN_DIMS, preferred_element_type=F32)
        h_scr[g * gw:(g + 1) * gw, :] = h_g * jnp.concatenate(dec_parts, axis=0) + upd
    y_all = jnp.concatenate(ys, axis=1) * _silu(z_ref[...])
    nrm = nrm_ref[...]
    y_ref[...] = jnp.concatenate(
        [_rms(y_all[:, g * gw:(g + 1) * gw], nrm[:, g * gw:(g + 1) * gw]) for g in range(SSM_GROUPS)], axis=1)
    hout_ref[...] = h_scr[...]


def ssd_mixer(xbc, z, dt, conv_w, conv_b, dt_bias, a_log, d_skip, norm_g):
    bsz, length = xbc.shape[0], xbc.shape[1]
    nc = length // SSM_CHUNK
    cur = lambda b, c: (b, c, 0)
    per_b = lambda b, c: (b, 0, 0)
    pad8 = lambda v: jnp.pad(v.reshape(1, SSM_HEADS), ((0, 0), (0, LANES - SSM_HEADS)))
    return pl.pallas_call(
        _ssd_kernel,
        grid=(bsz, nc),
        in_specs=[pl.BlockSpec((None, SSM_CHUNK, SSM_CONV_CH), cur),
                  pl.BlockSpec((None, SSM_CHUNK, SSM_INNER), cur),
                  pl.BlockSpec((None, SSM_CHUNK, LANES), cur),
                  _const_spec((CONV_WIDTH, SSM_CONV_CH)), _const_spec((1, SSM_CONV_CH)),
                  _const_spec((1, LANES)), _const_spec((1, LANES)),
                  pl.BlockSpec(memory_space=pltpu.SMEM),
                  _const_spec((1, SSM_INNER))],
        out_specs=[pl.BlockSpec((None, SSM_CHUNK, SSM_INNER), cur),
                   pl.BlockSpec((None, SSM_INNER, SSM_STATE), per_b)],
        out_shape=[jax.ShapeDtypeStruct((bsz, length, SSM_INNER), F32),
                   jax.ShapeDtypeStruct((bsz, SSM_INNER, SSM_STATE), F32)],
        scratch_shapes=[pltpu.VMEM((SSM_CHUNK + SUBLANES, SSM_CONV_CH), F32),
                        pltpu.VMEM((SSM_INNER, SSM_STATE), F32)],
        compiler_params=_cparams(("parallel", "arbitrary")),
        name="ssd_mixer",
    )(xbc, z, dt, conv_w, conv_b.reshape(1, SSM_CONV_CH), pad8(dt_bias), pad8(a_log),
      d_skip.reshape(1, SSM_HEADS), norm_g.reshape(1, SSM_INNER))


def _unit_lower_inverses(lmats, row, col):
    c = lmats[0].shape[0]
    mm = lambda a, b: jnp.dot(a.astype(BF16), b.astype(BF16), preferred_element_type=F32)
    eye = (row == col).astype(F32)
    blk = SUBLANES
    same = (row // blk) == (col // blk)
    xs = [jnp.where(same, -l, 0.0) for l in lmats]
    invs = [eye + x for x in xs]
    p = blk
    while p > 2:
        xs = [mm(x, x) for x in xs]
        invs = [i + mm(i, x) for i, x in zip(invs, xs)]
        p //= 2
    while blk < c:
        outer = ((row // (2 * blk)) == (col // (2 * blk))) & ((row // blk) != (col // blk))
        ts = [mm(i, jnp.where(outer, l, 0.0)) for i, l in zip(invs, lmats)]
        invs = [i - mm(t, i) for i, t in zip(invs, ts)]
        blk *= 2
    return invs


def _dn_kernel(qkv_ref, z_ref, ba_ref, cw_ref, dtb_ref, alog_ref, nrm_ref, o_ref, sout_ref, xx_scr, s_scr):
    nseq, c_len = qkv_ref.shape[0], qkv_ref.shape[1]
    seqs = range(nseq)
    chains = [(s, h) for s in seqs for h in range(DN_HEADS)]
    ids = range(len(chains))

    @pl.when(pl.program_id(1) == 0)
    def _():
        xx_scr[:, 0:SUBLANES, :] = jnp.zeros((nseq, SUBLANES, xx_scr.shape[2]), F32)
        s_scr[...] = jnp.zeros_like(s_scr)

    row = lax.broadcasted_iota(jnp.int32, (c_len, c_len), 0)
    col = lax.broadcasted_iota(jnp.int32, (c_len, c_len), 1)
    incl = row >= col
    strict = row > col
    tri = incl.astype(F32)
    nrm = nrm_ref[...]
    bf = lambda t: t.astype(BF16)
    mm = lambda a, b: jnp.dot(bf(a), bf(b), preferred_element_type=F32)

    act, beta, cum, cum_t, e_cum, e_rest, e_last = [], [], [], [], [], [], []
    for s in seqs:
        act.append(_silu(_chunk_conv(qkv_ref.at[s], xx_scr.at[s], cw_ref, c_len)))
        _carry_conv_tail(xx_scr.at[s], c_len)
        ba = ba_ref[s]
        beta.append(_sigmoid(ba))
        gate = -jnp.exp(alog_ref[...]) * _softplus(ba + dtb_ref[...])
        cum_s = jnp.dot(tri, gate, preferred_element_type=F32, precision=lax.Precision.HIGHEST)
        c_last = cum_s[c_len - 1:c_len, :]
        cum.append(cum_s)
        cum_t.append(cum_s.T)
        e_cum.append(jnp.exp(cum_s))
        e_rest.append(jnp.exp(c_last - cum_s))
        e_last.append(jnp.exp(c_last))

    q, k, kb, rhs, decay = [], [], [], [], []
    for i in ids:
        s, h = chains[i]
        gl = DN_HEADS + h
        q_h = act[s][:, h * DN_DK:(h + 1) * DN_DK]
        k_h = act[s][:, DN_QK + h * DN_DK:DN_QK + (h + 1) * DN_DK]
        v_h = act[s][:, 2 * DN_QK + h * DN_DV:2 * DN_QK + (h + 1) * DN_DV]
        q.append(q_h * lax.rsqrt(jnp.sum(q_h * q_h, axis=-1, keepdims=True) + EPS) * (DN_DK ** -0.5))
        k.append(k_h * lax.rsqrt(jnp.sum(k_h * k_h, axis=-1, keepdims=True) + EPS))
        beta_c = beta[s][:, h:h + 1]
        kb.append(k[i] * beta_c)
        rhs.append(jnp.concatenate([v_h * beta_c, kb[i] * e_cum[s][:, gl:gl + 1]], axis=1))
        decay.append(jnp.exp(jnp.where(incl, cum[s][:, gl:gl + 1] - cum_t[s][gl:gl + 1, :], -jnp.inf)))
    kq = [lax.dot_general(bf(jnp.concatenate([kb[i], q[i]], axis=0)), bf(k[i]), NT_DIMS,
                          preferred_element_type=F32) for i in ids]
    lmat = [jnp.where(strict, kq[i][:c_len] * decay[i], 0.0) for i in ids]
    attn = [kq[i][c_len:] * decay[i] for i in ids]
    tinv = _unit_lower_inverses(lmat, row, col)
    uw = [mm(tinv[i], rhs[i]) for i in ids]
    s_old = [s_scr[s, h * DN_DK:(h + 1) * DN_DK, :] for s, h in chains]
    qe = [q[i] * e_cum[s][:, DN_HEADS + h:DN_HEADS + h + 1] for i, (s, h) in enumerate(chains)]
    wq = [mm(jnp.concatenate([uw[i][:, DN_DV:], qe[i]], axis=0), s_old[i]) for i in ids]
    v_new = [uw[i][:, :DN_DV] - wq[i][:c_len] for i in ids]
    o = [wq[i][c_len:] + mm(attn[i], v_new[i]) for i in ids]
    for i in ids:
        s, h = chains[i]
        gl = DN_HEADS + h
        s_scr[s, h * DN_DK:(h + 1) * DN_DK, :] = s_old[i] * e_last[s][:, gl:gl + 1] + lax.dot_general(
            bf(k[i] * e_rest[s][:, gl:gl + 1]), bf(v_new[i]), TN_DIMS, preferred_element_type=F32)
    for s in seqs:
        z = z_ref[s]
        o_ref[s] = jnp.concatenate(
            [_rms(o[s * DN_HEADS + h], nrm) * _silu(z[:, h * DN_DV:(h + 1) * DN_DV]) for h in range(DN_HEADS)],
            axis=1)
    sout_ref[...] = s_scr[...]


def dn_mixer(qkv, z, ba, conv_w, dt_bias, a_log, norm_g, nseq):
    bsz, length = qkv.shape[0], qkv.shape[1]
    nc = length // DN_CHUNK
    cur = lambda b, c: (b, c, 0)
    per_b = lambda b, c: (b, 0, 0)
    pad_a = lambda v: jnp.pad(v.reshape(1, DN_HEADS), ((0, 0), (DN_HEADS, LANES - 2 * DN_HEADS)))
    return pl.pallas_call(
        _dn_kernel,
        grid=(bsz // nseq, nc),
        in_specs=[pl.BlockSpec((nseq, DN_CHUNK, DN_CONV_CH), cur),
                  pl.BlockSpec((nseq, DN_CHUNK, DN_VW), cur),
                  pl.BlockSpec((nseq, DN_CHUNK, LANES), cur),
                  _const_spec((CONV_WIDTH, DN_CONV_CH)),
                  _const_spec((1, LANES)), _const_spec((1, LANES)), _const_spec((1, DN_DV))],
        out_specs=[pl.BlockSpec((nseq, DN_CHUNK, DN_VW), cur),
                   pl.BlockSpec((nseq, DN_QK, DN_DV), per_b)],
        out_shape=[jax.ShapeDtypeStruct((bsz, length, DN_VW), F32),
                   jax.ShapeDtypeStruct((bsz, DN_QK, DN_DV), F32)],
        scratch_shapes=[pltpu.VMEM((nseq, DN_CHUNK + SUBLANES, DN_CONV_CH), F32),
                        pltpu.VMEM((nseq, DN_QK, DN_DV), F32)],
        compiler_params=_cparams(("parallel", "arbitrary")),
        name="dn_mixer",
    )(qkv, z, ba, conv_w, pad_a(dt_bias), pad_a(a_log), norm_g.reshape(1, DN_DV))


X_SUB = D_MODEL // LANES
ROW_SUB = 2 * SUBLANES


def _rows_to_slabs(x):
    parts = jnp.stack([x[:, s * LANES:(s + 1) * LANES] for s in range(X_SUB)], axis=0)
    return jnp.transpose(parts, (1, 0, 2))


def _slabs_to_rows(slab):
    parts = jnp.transpose(slab, (1, 0, 2))
    return jnp.concatenate([parts[s] for s in range(X_SUB)], axis=1)


def _resid_route_kernel(x_ref, *refs, n_in):
    act_refs, w_refs = refs[:n_in], refs[n_in:2 * n_in]
    g_ref, wr_ref, br_ref, xg_ref, grp_ref, rank_ref, cnt_ref, carry_scr = refs[2 * n_in:]
    tm = x_ref.shape[0]

    @pl.when(pl.program_id(0) == 0)
    def _():
        carry_scr[...] = jnp.zeros_like(carry_scr)

    x = x_ref[...]
    for a_ref, w_ref in zip(act_refs, w_refs):
        x = x + jnp.dot(a_ref[...].astype(BF16), w_ref[...], preferred_element_type=F32)
    h = _rms(x, g_ref[...])
    logits = jnp.dot(h, wr_ref[...], preferred_element_type=F32, precision=lax.Precision.HIGHEST) + br_ref[...]
    lt = logits.T
    sub = lax.broadcasted_iota(jnp.int32, (SUBLANES, tm), 0)
    neg = -jnp.inf
    glog = jnp.where(sub < N_GROUPS, lt[N_EXPERTS:N_EXPERTS + SUBLANES], neg)
    gmax = jnp.max(glog, axis=0, keepdims=True)
    g_p = 1.0 / jnp.sum(jnp.exp(glog - gmax), axis=0, keepdims=True)
    g_i = jnp.min(jnp.where(glog == gmax, sub, SUBLANES), axis=0, keepdims=True)
    sel = lt[0:EXPERTS_PER_GROUP]
    for g in range(1, N_GROUPS):
        sel = jnp.where(g_i == g, lt[g * EXPERTS_PER_GROUP:(g + 1) * EXPERTS_PER_GROUP], sel)
    m1 = jnp.max(sel, axis=0, keepdims=True)
    zsum = jnp.sum(jnp.exp(sel - m1), axis=0, keepdims=True)
    i1 = jnp.min(jnp.where(sel == m1, sub, SUBLANES), axis=0, keepdims=True)
    sel2 = jnp.where(sub == i1, neg, sel)
    m2 = jnp.max(sel2, axis=0, keepdims=True)
    i2 = jnp.min(jnp.where(sel2 == m2, sub, SUBLANES), axis=0, keepdims=True)
    p1 = 1.0 / zsum
    p2 = jnp.exp(m2 - m1) / zsum
    gate1 = g_p * p1 / (p1 + p2)
    gate2 = g_p * p2 / (p1 + p2)
    gates = jnp.where(sub == i1, gate1, 0.0) + jnp.where(sub == i2, gate2, 0.0)

    onehot = (sub == g_i).astype(F32)
    before = (lax.broadcasted_iota(jnp.int32, (tm, tm), 0) < lax.broadcasted_iota(jnp.int32, (tm, tm), 1))
    rank_in_tile = jnp.dot(onehot.astype(BF16), before.astype(BF16), preferred_element_type=F32)
    carry = carry_scr[...]
    rank = jnp.sum(onehot * (rank_in_tile + carry[:, 0:1]), axis=0, keepdims=True)
    grp_ref[...] = g_i
    rank_ref[...] = rank.astype(jnp.int32)
    new_carry = carry + jnp.sum(onehot, axis=1, keepdims=True)
    carry_scr[...] = new_carry
    cnt_ref[...] = new_carry

    gates_rows = jnp.concatenate([gates, jnp.zeros((LANES - SUBLANES, tm), F32)], axis=0).T
    xg_ref[:, :X_SUB, :] = _rows_to_slabs(x)
    xg_ref[:, X_SUB:, :] = _rows_to_slabs(jnp.concatenate(
        [gates_rows, jnp.zeros((tm, D_MODEL - LANES), F32)], axis=1))


def resid_route(x, acts, ws, gain, w_router_pad, b_router_pad, tm):
    t = x.shape[0]
    n_in = len(acts)
    tok = lambda i: (i, 0)
    return pl.pallas_call(
        functools.partial(_resid_route_kernel, n_in=n_in),
        grid=(t // tm,),
        in_specs=([pl.BlockSpec((tm, D_MODEL), tok)]
                  + [pl.BlockSpec((tm, a.shape[1]), tok) for a in acts]
                  + [_const_spec(w.shape) for w in ws]
                  + [_const_spec((1, D_MODEL)), _const_spec((D_MODEL, LANES)), _const_spec((1, LANES))]),
        out_specs=[pl.BlockSpec((tm, ROW_SUB, LANES), lambda i: (i, 0, 0)),
                   pl.BlockSpec((None, 1, tm), lambda i: (i, 0, 0)),
                   pl.BlockSpec((None, 1, tm), lambda i: (i, 0, 0)),
                   _const_spec((SUBLANES, LANES))],
        out_shape=[jax.ShapeDtypeStruct((t, ROW_SUB, LANES), F32),
                   jax.ShapeDtypeStruct((t // tm, 1, tm), jnp.int32),
                   jax.ShapeDtypeStruct((t // tm, 1, tm), jnp.int32),
                   jax.ShapeDtypeStruct((SUBLANES, LANES), F32)],
        scratch_shapes=[pltpu.VMEM((SUBLANES, LANES), F32)],
        compiler_params=_cparams(("arbitrary",)),
        name="resid_route",
    )(x, *acts, *ws, gain.reshape(1, D_MODEL), w_router_pad, b_router_pad)


def _issue_row_copies(n_rows, make_copy):
    def trip(i, carry):
        for u in range(ROW_DMA_UNROLL):
            make_copy(i * ROW_DMA_UNROLL + u).start(priority=u % DMA_PRIORITIES)
        return carry

    lax.fori_loop(0, n_rows // ROW_DMA_UNROLL, trip, 0)


def _dispatch_kernel(pos_ref, pad_lo_ref, pad_hi_ref, xg_ref, xs_hbm, zero_scr, sem, zsem):
    tm = xg_ref.shape[0]
    _issue_row_copies(tm, lambda r: pltpu.make_async_copy(xg_ref.at[r], xs_hbm.at[pos_ref[0, r]], sem))
    pltpu.make_async_copy(xg_ref, xs_hbm.at[pl.ds(0, tm)], sem).wait()

    @pl.when(pl.program_id(0) == pl.num_programs(0) - 1)
    def _():
        zero_scr[...] = jnp.zeros_like(zero_scr)
        for k in range(pad_lo_ref.shape[0]):
            def fill(j, carry):
                pltpu.make_async_copy(zero_scr, xs_hbm.at[j], zsem).start()
                return carry

            def drain(j, carry):
                pltpu.make_async_copy(zero_scr, xs_hbm.at[j], zsem).wait()
                return carry

            lax.fori_loop(pad_lo_ref[k], pad_hi_ref[k], fill, 0)
            lax.fori_loop(pad_lo_ref[k], pad_hi_ref[k], drain, 0)


def moe_dispatch(pos, pad_lo, pad_hi, xg, tm, n_slots):
    t = xg.shape[0]
    return pl.pallas_call(
        _dispatch_kernel,
        grid=(t // tm,),
        in_specs=[pl.BlockSpec((None, 1, tm), lambda i: (i, 0, 0), memory_space=pltpu.SMEM),
                  pl.BlockSpec(memory_space=pltpu.SMEM), pl.BlockSpec(memory_space=pltpu.SMEM),
                  pl.BlockSpec((tm, ROW_SUB, LANES), lambda i: (i, 0, 0))],
        out_specs=pl.BlockSpec(memory_space=pl.ANY),
        out_shape=jax.ShapeDtypeStruct((n_slots, ROW_SUB, LANES), F32),
        scratch_shapes=[pltpu.VMEM((ROW_SUB, LANES), F32), pltpu.SemaphoreType.DMA, pltpu.SemaphoreType.DMA],
        compiler_params=_cparams(("arbitrary",), has_side_effects=True),
        name="moe_dispatch",
    )(pos, pad_lo, pad_hi, xg)


def _collect_kernel(pos_ref, os_hbm, out_ref, buf, sem):
    tm = out_ref.shape[0]
    _issue_row_copies(tm, lambda r: pltpu.make_async_copy(os_hbm.at[pos_ref[0, r]], buf.at[r], sem))
    pltpu.make_async_copy(os_hbm.at[pl.ds(0, tm)], buf, sem).wait()
    out_ref[...] = _slabs_to_rows(buf[...])


def moe_collect(pos, os_sorted, t, tm):
    return pl.pallas_call(
        _collect_kernel,
        grid=(t // tm,),
        in_specs=[pl.BlockSpec((None, 1, tm), lambda i: (i, 0, 0), memory_space=pltpu.SMEM),
                  pl.BlockSpec(memory_space=pl.ANY)],
        out_specs=pl.BlockSpec((tm, D_MODEL), lambda i: (i, 0)),
        out_shape=jax.ShapeDtypeStruct((t, D_MODEL), F32),
        scratch_shapes=[pltpu.VMEM((tm, X_SUB, LANES), F32), pltpu.SemaphoreType.DMA],
        compiler_params=_cparams(("arbitrary",), has_side_effects=True),
        name="moe_collect",
    )(pos, os_sorted)


def _group_experts_kernel(grp_ref, xg_ref, g_ref, wg_ref, wu_ref, wd_ref, o_ref):
    del grp_ref
    x = _slabs_to_rows(xg_ref[:, :X_SUB, :])
    gates = _slabs_to_rows(xg_ref[:, X_SUB:, :])[:, :LANES]
    h = _rms(x, g_ref[...]).astype(BF16)
    acc = x
    for e in range(EXPERTS_PER_GROUP):
        gate = jnp.dot(h, wg_ref[e], preferred_element_type=F32)
        up = jnp.dot(h, wu_ref[e], preferred_element_type=F32)
        act = (_silu(gate) * up).astype(BF16)
        acc = acc + gates[:, e:e + 1] * jnp.dot(act, wd_ref[e], preferred_element_type=F32)
    o_ref[...] = _rows_to_slabs(acc)


def moe_group_experts(tile_grp, xs, gain, wg, wu, wd, tile):
    n_steps = xs.shape[0] // tile
    rows = lambda i, grp: (i, 0, 0)
    wts = lambda i, grp: (grp[i], 0, 0, 0)
    return pl.pallas_call(
        _group_experts_kernel,
        grid_spec=pltpu.PrefetchScalarGridSpec(
            num_scalar_prefetch=1,
            grid=(n_steps,),
            in_specs=[pl.BlockSpec((tile, ROW_SUB, LANES), rows),
                      pl.BlockSpec((1, D_MODEL), lambda i, grp: (0, 0)),
                      pl.BlockSpec((None, EXPERTS_PER_GROUP, D_MODEL, EXPERT_FF), wts),
                      pl.BlockSpec((None, EXPERTS_PER_GROUP, D_MODEL, EXPERT_FF), wts),
                      pl.BlockSpec((None, EXPERTS_PER_GROUP, EXPERT_FF, D_MODEL), wts)],
            out_specs=pl.BlockSpec((tile, X_SUB, LANES), rows)),
        out_shape=jax.ShapeDtypeStruct((xs.shape[0], X_SUB, LANES), F32),
        compiler_params=_cparams(("arbitrary",)),
        name="moe_group_experts",
    )(tile_grp, xs, gain.reshape(1, D_MODEL), wg, wu, wd)


def mix_out_and_moe(x, acts, ws, gain, w_route, b_route, wg, wu, wd):
    t = x.shape[0]
    tile = MOE_TILE if t % MOE_TILE == 0 else t
    n_tiles = t // tile + N_GROUPS
    xg, grp, rank, cnt = resid_route(x, acts, ws, gain, w_route, b_route, tile)
    counts = cnt[:N_GROUPS, 0].astype(jnp.int32)
    tile_ends = jnp.cumsum((counts + tile - 1) // tile)
    offs = (tile_ends - (counts + tile - 1) // tile) * tile
    pos = rank + sum(jnp.where(grp == g, offs[g], 0) for g in range(N_GROUPS))
    pad_lo = jnp.concatenate([offs + counts, tile_ends[-1:] * tile])
    pad_hi = jnp.concatenate([tile_ends * tile, jnp.full((1,), n_tiles * tile, jnp.int32)])
    tile_grp = jnp.minimum(jnp.sum((jnp.arange(n_tiles, dtype=jnp.int32)[:, None] >= tile_ends[None, :])
                                   .astype(jnp.int32), axis=1), N_GROUPS - 1)
    xs = moe_dispatch(pos, pad_lo, pad_hi, xg, tile, n_tiles * tile)
    os_sorted = moe_group_experts(tile_grp, xs, gain, wg, wu, wd, tile)
    return moe_collect(pos, os_sorted, t, tile)


BATCH_NT = (((2,), (2,)), ((0,), (0,)))
BATCH_NN = (((2,), (1,)), ((0,), (0,)))


def _token_major(parts):
    return jnp.transpose(jnp.stack(parts, axis=0), (1, 0, 2))


def _pad_rows(parts, tb, width):
    return parts + [jnp.zeros((tb, width), F32)] * (SUBLANES - len(parts))


def _spread(v, first, n, width):
    tb = v.shape[0]
    return jnp.concatenate([jnp.broadcast_to(v[:, first + i:first + i + 1], (tb, width)) for i in range(n)], axis=1)


def _columns(rows):
    tb, m = rows.shape
    return jnp.concatenate([rows, jnp.zeros((LANES - tb, m), F32)], axis=0).T


def _one_step_conv(x_ref, b0_ref, b1_ref, b2_ref, w_ref):
    return (w_ref[0:1, :] * b0_ref[...] + w_ref[1:2, :] * b1_ref[...] + w_ref[2:3, :] * b2_ref[...]
            + w_ref[3:4, :] * x_ref[...])


def _swa_decode_kernel(sink_ref, q_ref, k_ref, v_ref, kc_ref, vc_ref, qn_ref, kn_ref, o_ref, kout_ref):
    tb = q_ref.shape[0]
    lo = lax.broadcasted_iota(jnp.int32, (tb, LANES), 1) < HEAD_DIM

    def pair_rms(t, gain):
        sq = t * t
        s_lo = jnp.sum(jnp.where(lo, sq, 0.0), axis=-1, keepdims=True)
        s_hi = jnp.sum(jnp.where(lo, 0.0, sq), axis=-1, keepdims=True)
        return t * lax.rsqrt(jnp.where(lo, s_lo, s_hi) * (1.0 / HEAD_DIM) + EPS) * gain

    qn2 = jnp.concatenate([qn_ref[...], qn_ref[...]], axis=1)
    kn2 = jnp.concatenate([kn_ref[...], kn_ref[...]], axis=1)
    k_new = pair_rms(k_ref[...], kn2)
    kout_ref[...] = k_new
    v_new = v_ref[...]
    rows = []
    for h in range(SWA_HEADS):
        t = pair_rms(q_ref[:, (h // 2) * LANES:(h // 2 + 1) * LANES], qn2) * (HEAD_DIM ** -0.5)
        j = h // SWA_GROUP
        if h % 2 != j:
            t = pltpu.roll(t, HEAD_DIM, axis=1)
        rows.append(jnp.where(lo if j == 0 else jnp.logical_not(lo), t, 0.0))
    q8 = _token_major(rows)
    s = lax.dot_general(q8.astype(BF16), kc_ref[...].astype(BF16), BATCH_NT, preferred_element_type=F32)
    s_new = jnp.sum(q8 * k_new[:, None, :], axis=-1, keepdims=True)
    sub = lax.broadcasted_iota(jnp.int32, (1, SWA_HEADS, 1), 1)
    sink = jnp.zeros((1, SWA_HEADS, 1), F32)
    for h in range(SWA_HEADS):
        sink = jnp.where(sub == h, sink_ref[h], sink)
    m = jnp.maximum(jnp.maximum(jnp.max(s, axis=-1, keepdims=True), s_new), sink)
    p = jnp.exp(s - m)
    p_new = jnp.exp(s_new - m)
    den = jnp.sum(p, axis=-1, keepdims=True) + p_new + jnp.exp(sink - m)
    o8 = lax.dot_general((p / den).astype(BF16), vc_ref[...].astype(BF16), BATCH_NN,
                         preferred_element_type=F32) + (p_new / den) * v_new[:, None, :]
    o_h = jnp.transpose(o8, (1, 0, 2))
    tiles = []
    for t in range(SWA_HEADS // 2):
        halves = []
        for h in (2 * t, 2 * t + 1):
            piece = o_h[h]
            if h % 2 != h // SWA_GROUP:
                piece = pltpu.roll(piece, HEAD_DIM, axis=1)
            halves.append(piece)
        tiles.append(jnp.where(lo, halves[0], halves[1]))
    o_ref[...] = jnp.concatenate(tiles, axis=1)


def swa_decode(q, k, v, k_cache, v_cache, q_norm, k_norm, sinks):
    bsz = q.shape[0]
    tok = lambda i: (i, 0)
    tok3 = lambda i: (i, 0, 0)
    return pl.pallas_call(
        _swa_decode_kernel,
        grid=(bsz // DEC_TB,),
        in_specs=[pl.BlockSpec(memory_space=pltpu.SMEM),
                  pl.BlockSpec((DEC_TB, SWA_Q), tok), pl.BlockSpec((DEC_TB, SWA_KV), tok),
                  pl.BlockSpec((DEC_TB, SWA_KV), tok),
                  pl.BlockSpec((DEC_TB, WINDOW, SWA_KV), tok3), pl.BlockSpec((DEC_TB, WINDOW, SWA_KV), tok3),
                  _const_spec((1, HEAD_DIM)), _const_spec((1, HEAD_DIM))],
        out_specs=[pl.BlockSpec((DEC_TB, SWA_Q), tok), pl.BlockSpec((DEC_TB, SWA_KV), tok)],
        out_shape=[jax.ShapeDtypeStruct((bsz, SWA_Q), F32), jax.ShapeDtypeStruct((bsz, SWA_KV), F32)],
        compiler_params=_cparams(("parallel",)),
        name="swa_decode",
    )(sinks, q, k, v, k_cache, v_cache, q_norm.reshape(1, HEAD_DIM), k_norm.reshape(1, HEAD_DIM))


def _ssd_decode_kernel(x_ref, b0_ref, b1_ref, b2_ref, z_ref, dt_ref, cw_ref, cb_ref, dtb_ref, alog_ref, dd_ref,
                       nrm_ref, h0_ref, y_ref, hout_ref):
    tb = x_ref.shape[0]
    act = _silu(_one_step_conv(x_ref, b0_ref, b1_ref, b2_ref, cw_ref) + cb_ref[...])
    xs = act[:, :SSM_INNER]
    bm = act[:, SSM_INNER:SSM_INNER + SSM_GROUPS * SSM_STATE]
    cm = act[:, SSM_INNER + SSM_GROUPS * SSM_STATE:]
    dt = _softplus(dt_ref[...] + dtb_ref[...])
    dec = jnp.exp(dt * (-jnp.exp(alog_ref[...])))
    dt_x = _spread(dt, 0, SSM_HEADS, SSM_HEAD_DIM)
    dec_x = _spread(dec, 0, SSM_HEADS, SSM_HEAD_DIM)
    gw = SSM_INNER // SSM_GROUPS
    grp = lambda t, g: t[:, g * SSM_STATE:(g + 1) * SSM_STATE]
    cb_x = jnp.concatenate([jnp.broadcast_to(jnp.sum(grp(cm, g) * grp(bm, g), axis=-1, keepdims=True), (tb, gw))
                            for g in range(SSM_GROUPS)], axis=1)
    h0 = h0_ref[...]
    c8 = _token_major(_pad_rows([grp(cm, g) for g in range(SSM_GROUPS)], tb, SSM_STATE))
    ys = jnp.transpose(lax.dot_general(c8.astype(BF16), h0.astype(BF16), BATCH_NT, preferred_element_type=F32),
                       (1, 0, 2))
    y_state = jnp.concatenate([ys[g][:, g * gw:(g + 1) * gw] for g in range(SSM_GROUPS)], axis=1)
    y = cb_x * dt_x * xs + y_state * dec_x + dd_ref[...] * xs
    y = y * _silu(z_ref[...])
    nrm = nrm_ref[...]
    y_ref[...] = jnp.concatenate(
        [_rms(y[:, g * gw:(g + 1) * gw], nrm[:, g * gw:(g + 1) * gw]) for g in range(SSM_GROUPS)], axis=1)
    u_cols = _columns(dt_x * xs)
    hpg = SSM_HEADS // SSM_GROUPS
    for b in range(tb):
        for g in range(SSM_GROUPS):
            rows = slice(g * gw, (g + 1) * gw)
            decay = jnp.concatenate([jnp.broadcast_to(dec[b:b + 1, h:h + 1], (SSM_HEAD_DIM, SSM_STATE))
                                     for h in range(g * hpg, (g + 1) * hpg)], axis=0)
            hout_ref[b, rows, :] = h0[b, rows, :] * decay + u_cols[rows, b:b + 1] * grp(bm, g)[b:b + 1, :]


def ssd_decode(xbc, bufs, z, dt, conv_w, conv_b, dt_bias, a_log, d_skip, norm_g, h0):
    bsz = xbc.shape[0]
    tok = lambda i: (i, 0)
    tok3 = lambda i: (i, 0, 0)
    pad8 = lambda v: jnp.pad(v.reshape(1, SSM_HEADS), ((0, 0), (0, LANES - SSM_HEADS)))
    row = lambda w: pl.BlockSpec((DEC_TB, w), tok)
    return pl.pallas_call(
        _ssd_decode_kernel,
        grid=(bsz // DEC_TB,),
        in_specs=[row(SSM_CONV_CH)] * 4 + [row(SSM_INNER), row(LANES),
                  _const_spec((CONV_WIDTH, SSM_CONV_CH)), _const_spec((1, SSM_CONV_CH)),
                  _const_spec((1, LANES)), _const_spec((1, LANES)), _const_spec((1, SSM_INNER)),
                  _const_spec((1, SSM_INNER)),
                  pl.BlockSpec((DEC_TB, SSM_INNER, SSM_STATE), tok3)],
        out_specs=[row(SSM_INNER), pl.BlockSpec((DEC_TB, SSM_INNER, SSM_STATE), tok3)],
        out_shape=[jax.ShapeDtypeStruct((bsz, SSM_INNER), F32),
                   jax.ShapeDtypeStruct((bsz, SSM_INNER, SSM_STATE), F32)],
        compiler_params=_cparams(("parallel",)),
        name="ssd_decode",
    )(xbc, *bufs, z, dt, conv_w, conv_b.reshape(1, SSM_CONV_CH), pad8(dt_bias), pad8(a_log),
      jnp.repeat(d_skip, SSM_HEAD_DIM).reshape(1, SSM_INNER), norm_g.reshape(1, SSM_INNER), h0)


def _dn_decode_kernel(x_ref, b0_ref, b1_ref, b2_ref, z_ref, ba_ref, cw_ref, dtb_ref, alog_ref, nrm_ref, s0_ref,
                      o_ref, sout_ref):
    tb = x_ref.shape[0]
    act = _silu(_one_step_conv(x_ref, b0_ref, b1_ref, b2_ref, cw_ref))
    ba = ba_ref[...]
    beta = _sigmoid(ba)
    eg = jnp.exp(-jnp.exp(alog_ref[...]) * _softplus(ba + dtb_ref[...]))
    z = z_ref[...]
    nrm = nrm_ref[...]
    outs = []
    for h in range(DN_HEADS):
        q_h = act[:, h * DN_DK:(h + 1) * DN_DK]
        k_h = act[:, DN_QK + h * DN_DK:DN_QK + (h + 1) * DN_DK]
        v_h = act[:, 2 * DN_QK + h * DN_DV:2 * DN_QK + (h + 1) * DN_DV]
        q_h = q_h * lax.rsqrt(jnp.sum(q_h * q_h, axis=-1, keepdims=True) + EPS) * (DN_DK ** -0.5)
        k_h = k_h * lax.rsqrt(jnp.sum(k_h * k_h, axis=-1, keepdims=True) + EPS)
        beta_c = beta[:, h:h + 1]
        eg_c = eg[:, DN_HEADS + h:DN_HEADS + h + 1]
        s_h = s0_ref[:, h * DN_DK:(h + 1) * DN_DK, :]
        kq8 = _token_major(_pad_rows([k_h, q_h], tb, DN_DK))
        r = jnp.transpose(lax.dot_general(kq8.astype(BF16), s_h.astype(BF16), BATCH_NN,
                                          preferred_element_type=F32), (1, 0, 2))
        v_new = beta_c * v_h - (beta_c * eg_c) * r[0]
        o_h = eg_c * r[1] + jnp.sum(q_h * k_h, axis=-1, keepdims=True) * v_new
        outs.append(_rms(o_h, nrm) * _silu(z[:, h * DN_DV:(h + 1) * DN_DV]))
        k_cols = _columns(k_h)
        for b in range(tb):
            sout_ref[b, h * DN_DK:(h + 1) * DN_DK, :] = (
                s_h[b] * jnp.broadcast_to(eg_c[b:b + 1, :], (DN_DK, DN_DV)) + k_cols[:, b:b + 1] * v_new[b:b + 1, :])
    o_ref[...] = jnp.concatenate(outs, axis=1)


def dn_decode(qkv, bufs, z, ba, conv_w, dt_bias, a_log, norm_g, s0):
    bsz = qkv.shape[0]
    tok = lambda i: (i, 0)
    tok3 = lambda i: (i, 0, 0)
    pad_a = lambda v: jnp.pad(v.reshape(1, DN_HEADS), ((0, 0), (DN_HEADS, LANES - 2 * DN_HEADS)))
    row = lambda w: pl.BlockSpec((DEC_TB, w), tok)
    return pl.pallas_call(
        _dn_decode_kernel,
        grid=(bsz // DEC_TB,),
        in_specs=[row(DN_CONV_CH)] * 4 + [row(DN_VW), row(LANES),
                  _const_spec((CONV_WIDTH, DN_CONV_CH)),
                  _const_spec((1, LANES)), _const_spec((1, LANES)), _const_spec((1, DN_DV)),
                  pl.BlockSpec((DEC_TB, DN_QK, DN_DV), tok3)],
        out_specs=[row(DN_VW), pl.BlockSpec((DEC_TB, DN_QK, DN_DV), tok3)],
        out_shape=[jax.ShapeDtypeStruct((bsz, DN_VW), F32), jax.ShapeDtypeStruct((bsz, DN_QK, DN_DV), F32)],
        compiler_params=_cparams(("parallel",)),
        name="dn_decode",
    )(qkv, *bufs, z, ba, conv_w, pad_a(dt_bias), pad_a(a_log), norm_g.reshape(1, DN_DV), s0)


def _pad_cols(w, n_pad):
    return jnp.pad(w, ((0, 0), (0, n_pad - w.shape[1])))


EVEN_SEGS = ((0, 512), (512, 640), (640, 768), (768, 1280), (1280, 2304), (2304, 2432))
ODD_SEGS = ((0, 3072), (3072, 4096), (4096, 4224))


def _trunk(x_seq, p, states):
    prompt = states is None
    bsz, length = x_seq.shape[0], x_seq.shape[1]
    t = bsz * length
    tm = 512 if t % 512 == 0 else t
    x = x_seq.reshape(t, D_MODEL)
    seq = lambda u: u.reshape(bsz, length, u.shape[-1])

    q, k, v, z, xbc, dt = prenorm_proj(x, p['ln_mix'][0], p['w_in_even'], EVEN_SEGS, tm)
    if prompt:
        v3, xbc3 = seq(v), seq(xbc)
        att, k_normed = swa_attention(seq(q), seq(k), v3, p['q_norm'], p['k_norm'], p['attn_sinks'])
        new_k = k_normed[:, -WINDOW:].reshape(bsz, WINDOW, SWA_KV_HEADS, HEAD_DIM)
        new_v = v3[:, -WINDOW:].reshape(bsz, WINDOW, SWA_KV_HEADS, HEAD_DIM)
        new_ssm_conv = xbc3[:, -(CONV_WIDTH - 1):]
        y_ssm, new_h = ssd_mixer(xbc3, seq(z), seq(dt), p['ssm_conv_w'], p['ssm_conv_b'], p['ssm_dt_bias'],
                                 p['ssm_A_log'], p['ssm_D'], p['ssm_norm'])
    else:
        k_win, v_win, ssm_h, ssm_conv, dn_s, dn_conv = states
        kb = k_win.reshape(bsz, WINDOW, SWA_KV)
        vb = v_win.reshape(bsz, WINDOW, SWA_KV)
        att, k_normed = swa_decode(q, k, v, kb, vb, p['q_norm'], p['k_norm'], p['attn_sinks'])
        new_k = jnp.concatenate([kb[:, 1:], k_normed[:, None]], axis=1).reshape(bsz, WINDOW, SWA_KV_HEADS, HEAD_DIM)
        new_v = jnp.concatenate([vb[:, 1:], v[:, None]], axis=1).reshape(bsz, WINDOW, SWA_KV_HEADS, HEAD_DIM)
        new_ssm_conv = jnp.concatenate([ssm_conv[:, 1:], xbc[:, None, :]], axis=1)
        y_ssm, new_h = ssd_decode(xbc, [ssm_conv[:, i] for i in range(CONV_WIDTH - 1)], z, dt, p['ssm_conv_w'],
                                  p['ssm_conv_b'], p['ssm_dt_bias'], p['ssm_A_log'], p['ssm_D'], p['ssm_norm'],
                                  ssm_h.reshape(bsz, SSM_INNER, SSM_STATE))
    x = mix_out_and_moe(x, [att.reshape(t, SWA_Q), y_ssm.reshape(t, SSM_INNER)],
                        [p['w_out_even'][:SWA_Q], p['w_out_even'][SWA_Q:]],
                        p['ln_ffn'][0], p['w_route'][0], p['b_route'][0],
                        p['moe_w_gate'][0], p['moe_w_up'][0], p['moe_w_down'][0])

    qkv, zz, ba = prenorm_proj(x, p['ln_mix'][1], p['w_in_odd'], ODD_SEGS, tm // 2 if tm >= 512 else tm)
    if prompt:
        qkv3 = seq(qkv)
        new_dn_conv = qkv3[:, -(CONV_WIDTH - 1):]
        o_dn, new_s = dn_mixer(qkv3, seq(zz), seq(ba), p['dn_conv_w'], p['dn_dt_bias'], p['dn_A_log'],
                               p['dn_norm'], DN_SEQS_PER_STEP)
    else:
        new_dn_conv = jnp.concatenate([dn_conv[:, 1:], qkv[:, None, :]], axis=1)
        o_dn, new_s = dn_decode(qkv, [dn_conv[:, i] for i in range(CONV_WIDTH - 1)], zz, ba, p['dn_conv_w'],
                                p['dn_dt_bias'], p['dn_A_log'], p['dn_norm'], dn_s.reshape(bsz, DN_QK, DN_DV))
    x = mix_out_and_moe(x, [o_dn.reshape(t, DN_VW)], [p['w_out_odd']],
                        p['ln_ffn'][1], p['w_route'][1], p['b_route'][1],
                        p['moe_w_gate'][1], p['moe_w_up'][1], p['moe_w_down'][1])

    return (x.reshape(bsz, length, D_MODEL), new_k[None], new_v[None],
            new_h.reshape(1, bsz, SSM_HEADS, SSM_HEAD_DIM, SSM_STATE), new_ssm_conv[None],
            new_s.reshape(1, bsz, DN_HEADS, DN_DK, DN_DV), new_dn_conv[None])


def kernel(x_prompt, x_sample, cache_k_win, cache_v_win, state_ssm, state_ssm_conv, state_dn, state_dn_conv,
           ln_mix, ln_ffn, w_in_even, q_norm, k_norm, attn_sinks, ssm_conv_w, ssm_conv_b, ssm_dt_bias,
           ssm_A_log, ssm_D, ssm_norm, w_out_even, w_in_odd, dn_conv_w, dn_dt_bias, dn_A_log, dn_norm,
           w_out_odd, moe_w_group, moe_b_group, moe_w_router, moe_b_router, moe_w_gate, moe_w_up, moe_w_down):
    w_route = _pad_cols(jnp.concatenate([moe_w_router, moe_w_group], axis=-1).reshape(-1, N_EXPERTS + N_GROUPS),
                        LANES).reshape(2, D_MODEL, LANES)
    b_route = _pad_cols(jnp.concatenate([moe_b_router, moe_b_group], axis=-1), LANES).reshape(2, 1, LANES)
    p = {
        'ln_mix': ln_mix, 'ln_ffn': ln_ffn,
        'w_in_even': _pad_cols(w_in_even[0], EVEN_SEGS[-1][1]).astype(BF16),
        'q_norm': q_norm[0], 'k_norm': k_norm[0], 'attn_sinks': attn_sinks[0],
        'ssm_conv_w': ssm_conv_w[0], 'ssm_conv_b': ssm_conv_b[0], 'ssm_dt_bias': ssm_dt_bias[0],
        'ssm_A_log': ssm_A_log[0], 'ssm_D': ssm_D[0], 'ssm_norm': ssm_norm[0],
        'w_out_even': w_out_even[0].astype(BF16),
        'w_in_odd': _pad_cols(w_in_odd[0], ODD_SEGS[-1][1]).astype(BF16),
        'dn_conv_w': dn_conv_w[0], 'dn_dt_bias': dn_dt_bias[0], 'dn_A_log': dn_A_log[0], 'dn_norm': dn_norm[0],
        'w_out_odd': w_out_odd[0].astype(BF16),
        'w_route': w_route, 'b_route': b_route,
        'moe_w_gate': moe_w_gate.astype(BF16).reshape(2, N_GROUPS, EXPERTS_PER_GROUP, D_MODEL, EXPERT_FF),
        'moe_w_up': moe_w_up.astype(BF16).reshape(2, N_GROUPS, EXPERTS_PER_GROUP, D_MODEL, EXPERT_FF),
        'moe_w_down': moe_w_down.astype(BF16).reshape(2, N_GROUPS, EXPERTS_PER_GROUP, EXPERT_FF, D_MODEL),
    }
    y_p, kp, vp, sp, scp, dnp, dncp = _trunk(x_prompt, p, None)
    sample_states = (cache_k_win[0], cache_v_win[0], state_ssm[0], state_ssm_conv[0], state_dn[0],
                     state_dn_conv[0])
    y_s, ks, vs, ss, scs, dns, dncs = _trunk(x_sample, p, sample_states)
    return (y_p, y_s, kp, ks, vp, vs, sp, ss, scp, scs, dnp, dns, dncp, dncs)
```

```python
import functools

import jax
import jax.numpy as jnp
from jax import lax
from jax.experimental import pallas as pl
from jax.experimental.pallas import tpu as pltpu

F32 = jnp.float32
BF16 = jnp.bfloat16
EPS = 1e-6

D_MODEL = 1024
SWA_HEADS = 8
SWA_KV_HEADS = 2
SWA_GROUP = SWA_HEADS // SWA_KV_HEADS
HEAD_DIM = 64
WINDOW = 128
SWA_Q = SWA_HEADS * HEAD_DIM
SWA_KV = SWA_KV_HEADS * HEAD_DIM
SSM_HEADS = 8
SSM_HEAD_DIM = 64
SSM_GROUPS = 2
SSM_STATE = 128
SSM_INNER = SSM_HEADS * SSM_HEAD_DIM
SSM_CHUNK = 128
SSM_CONV_CH = SSM_INNER + 2 * SSM_GROUPS * SSM_STATE
CONV_WIDTH = 4
DN_HEADS = 8
DN_DK = 128
DN_DV = 128
DN_CHUNK = 64
DN_QK = DN_HEADS * DN_DK
DN_VW = DN_HEADS * DN_DV
DN_CONV_CH = 2 * DN_QK + DN_VW
N_GROUPS = 4
EXPERTS_PER_GROUP = 8
N_EXPERTS = N_GROUPS * EXPERTS_PER_GROUP
EXPERT_FF = 256

LANES = 128
SUBLANES = 8
VMEM_LIMIT = 56 * 1024 * 1024
DMA_PRIORITIES = 2

DN_SEQS_PER_STEP = 2
MOE_TILE = 512
DEC_TB = 8
ROW_DMA_UNROLL = 8

NT_DIMS = (((1,), (1,)), ((), ()))
TN_DIMS = (((0,), (0,)), ((), ()))


def _cparams(sem, **kw):
    return pltpu.CompilerParams(dimension_semantics=sem, vmem_limit_bytes=VMEM_LIMIT, **kw)


def _const_spec(shape):
    nd = len(shape)
    return pl.BlockSpec(shape, lambda *_: (0,) * nd)


def _sigmoid(x):
    return 1.0 / (1.0 + jnp.exp(-x))


def _silu(x):
    return x * _sigmoid(x)


def _softplus(x):
    return jnp.maximum(x, 0.0) + jnp.log(1.0 + jnp.exp(-jnp.abs(x)))


def _rms(x, gain):
    return x * lax.rsqrt(jnp.mean(x * x, axis=-1, keepdims=True) + EPS) * gain


def _prenorm_proj_kernel(x_ref, g_ref, w_ref, *out_refs, segs):
    h = _rms(x_ref[...], g_ref[...]).astype(BF16)
    for o_ref, (a, b) in zip(out_refs, segs):
        o_ref[...] = jnp.dot(h, w_ref[:, a:b], preferred_element_type=F32)


def prenorm_proj(x, gain, w_bf16, segs, tm):
    t = x.shape[0]
    n_pad = w_bf16.shape[1]
    return pl.pallas_call(
        functools.partial(_prenorm_proj_kernel, segs=segs),
        grid=(t // tm,),
        in_specs=[pl.BlockSpec((tm, D_MODEL), lambda i: (i, 0)),
                  _const_spec((1, D_MODEL)),
                  _const_spec((D_MODEL, n_pad))],
        out_specs=[pl.BlockSpec((tm, b - a), lambda i: (i, 0)) for a, b in segs],
        out_shape=[jax.ShapeDtypeStruct((t, b - a), F32) for a, b in segs],
        compiler_params=_cparams(("parallel",)),
        name="prenorm_proj",
    )(x, gain.reshape(1, D_MODEL), w_bf16)


def _swa_kernel(sink_ref, q_ref, kc_ref, kp_ref, vc_ref, vp_ref, qn_ref, kn_ref, o_ref, kout_ref):
    n = pl.program_id(1)
    blk = q_ref.shape[0]
    q = q_ref[...]
    kc, kp, vc, vp = kc_ref[...], kp_ref[...], vc_ref[...], vp_ref[...]
    qn, kn = qn_ref[...], kn_ref[...]
    row = lax.broadcasted_iota(jnp.int32, (blk, 2 * blk), 0)
    col = lax.broadcasted_iota(jnp.int32, (blk, 2 * blk), 1)
    rel = row + blk - col
    mask = (rel >= 0) & (rel <= WINDOW) & ((n > 0) | (col >= blk))
    outs, kouts = [], []
    for j in range(SWA_KV_HEADS):
        sl = slice(j * HEAD_DIM, (j + 1) * HEAD_DIM)
        kcj = _rms(kc[:, sl], kn)
        kpj = _rms(kp[:, sl], kn)
        kouts.append(kcj)
        kcat = jnp.concatenate([kpj, kcj], axis=0).astype(BF16)
        vcat = jnp.concatenate([vp[:, sl], vc[:, sl]], axis=0).astype(BF16)
        for g in range(SWA_GROUP):
            h = j * SWA_GROUP + g
            qh = _rms(q[:, h * HEAD_DIM:(h + 1) * HEAD_DIM], qn) * (HEAD_DIM ** -0.5)
            s = lax.dot_general(qh.astype(BF16), kcat, NT_DIMS, preferred_element_type=F32)
            s = jnp.where(mask, s, -jnp.inf)
            sink = sink_ref[h]
            m = jnp.maximum(jnp.max(s, axis=-1, keepdims=True), sink)
            p = jnp.exp(s - m)
            p = p / (jnp.sum(p, axis=-1, keepdims=True) + jnp.exp(sink - m))
            outs.append(jnp.dot(p.astype(BF16), vcat, preferred_element_type=F32))
    o_ref[...] = jnp.concatenate(outs, axis=1)
    kout_ref[...] = jnp.concatenate(kouts, axis=1)


def swa_attention(q, k, v, q_norm, k_norm, sinks):
    bsz, length = q.shape[0], q.shape[1]
    nb = length // WINDOW
    prev = lambda b, n: (b, jnp.maximum(n - 1, 0), 0)
    cur = lambda b, n: (b, n, 0)
    kv_blk = (None, WINDOW, SWA_KV)
    return pl.pallas_call(
        _swa_kernel,
        grid=(bsz, nb),
        in_specs=[pl.BlockSpec(memory_space=pltpu.SMEM),
                  pl.BlockSpec((None, WINDOW, SWA_Q), cur),
                  pl.BlockSpec(kv_blk, cur), pl.BlockSpec(kv_blk, prev),
                  pl.BlockSpec(kv_blk, cur), pl.BlockSpec(kv_blk, prev),
                  _const_spec((1, HEAD_DIM)), _const_spec((1, HEAD_DIM))],
        out_specs=[pl.BlockSpec((None, WINDOW, SWA_Q), cur), pl.BlockSpec(kv_blk, cur)],
        out_shape=[jax.ShapeDtypeStruct((bsz, length, SWA_Q), F32),
                   jax.ShapeDtypeStruct((bsz, length, SWA_KV), F32)],
        compiler_params=_cparams(("parallel", "arbitrary")),
        name="swa_attention",
    )(sinks, q, k, k, v, v, q_norm.reshape(1, HEAD_DIM), k_norm.reshape(1, HEAD_DIM))


def _chunk_conv(x_ref, xx_scr, w_ref, rows):
    xx_scr[SUBLANES:SUBLANES + rows, :] = x_ref[...]
    acc = None
    for tap in range(CONV_WIDTH):
        off = SUBLANES - (CONV_WIDTH - 1) + tap
        term = w_ref[tap:tap + 1, :] * xx_scr[off:off + rows, :]
        acc = term if acc is None else acc + term
    return acc


def _carry_conv_tail(xx_scr, rows):
    xx_scr[0:SUBLANES, :] = xx_scr[rows:rows + SUBLANES, :]


def _ssd_kernel(xbc_ref, z_ref, dt_ref, cw_ref, cb_ref, dtb_ref, alog_ref, dd_ref, nrm_ref,
                y_ref, hout_ref, xx_scr, h_scr):
    q_len = xbc_ref.shape[0]

    @pl.when(pl.program_id(1) == 0)
    def _():
        xx_scr[0:SUBLANES, :] = jnp.zeros((SUBLANES, xx_scr.shape[1]), F32)
        h_scr[...] = jnp.zeros_like(h_scr)

    act = _silu(_chunk_conv(xbc_ref, xx_scr, cw_ref, q_len) + cb_ref[...])
    _carry_conv_tail(xx_scr, q_len)
    xs = act[:, :SSM_INNER]
    bm = act[:, SSM_INNER:SSM_INNER + SSM_GROUPS * SSM_STATE]
    cm = act[:, SSM_INNER + SSM_GROUPS * SSM_STATE:]

    row = lax.broadcasted_iota(jnp.int32, (q_len, q_len), 0)
    col = lax.broadcasted_iota(jnp.int32, (q_len, q_len), 1)
    causal = row >= col
    dt = _softplus(dt_ref[...] + dtb_ref[...])
    da = dt * (-jnp.exp(alog_ref[...]))
    cum = jnp.dot(causal.astype(F32), da, preferred_element_type=F32, precision=lax.Precision.HIGHEST)
    cum_t = cum.T
    dt_t = dt.T
    e_cum = jnp.exp(cum)
    hpg = SSM_HEADS // SSM_GROUPS
    gw = hpg * SSM_HEAD_DIM
    ys = []
    for g in range(SSM_GROUPS):
        bm_g = bm[:, g * SSM_STATE:(g + 1) * SSM_STATE].astype(BF16)
        cm_g = cm[:, g * SSM_STATE:(g + 1) * SSM_STATE].astype(BF16)
        cb = lax.dot_general(cm_g, bm_g, NT_DIMS, preferred_element_type=F32)
        h_g = h_scr[g * gw:(g + 1) * gw, :]
        y_state = lax.dot_general(cm_g, h_g.astype(BF16), NT_DIMS, preferred_element_type=F32)
        xt_parts, dec_parts = [], []
        for hh in range(hpg):
            h = g * hpg + hh
            x_h = xs[:, h * SSM_HEAD_DIM:(h + 1) * SSM_HEAD_DIM]
            cum_c = cum[:, h:h + 1]
            seg = jnp.exp(jnp.where(causal, cum_c - cum_t[h:h + 1, :], -jnp.inf))
            wgt = cb * seg * dt_t[h:h + 1, :]
            y = jnp.dot(wgt.astype(BF16), x_h.astype(BF16), preferred_element_type=F32)
            y = y + y_state[:, hh * SSM_HEAD_DIM:(hh + 1) * SSM_HEAD_DIM] * e_cum[:, h:h + 1]
            ys.append(y + dd_ref[0, h] * x_h)
            c_last = cum[q_len - 1:q_len, h:h + 1]
            xt_parts.append(x_h * (jnp.exp(c_last - cum_c) * dt[:, h:h + 1]))
            dec_parts.append(jnp.broadcast_to(jnp.exp(c_last), (SSM_HEAD_DIM, SSM_STATE)))
        xt = jnp.concatenate(xt_parts, axis=1).astype(BF16)
        upd = lax.dot_general(xt, bm_g, TN_DIMS, preferred_element_type=F32)
        h_scr[g * gw:(g + 1) * gw, :] = h_g * jnp.concatenate(dec_parts, axis=0) + upd
    y_all = jnp.concatenate(ys, axis=1) * _silu(z_ref[...])
    nrm = nrm_ref[...]
    y_ref[...] = jnp.concatenate(
        [_rms(y_all[:, g * gw:(g + 1) * gw], nrm[:, g * gw:(g + 1) * gw]) for g in range(SSM_GROUPS)], axis=1)
    hout_ref[...] = h_scr[...]


def ssd_mixer(xbc, z, dt, conv_w, conv_b, dt_bias, a_log, d_skip, norm_g):
    bsz, length = xbc.shape[0], xbc.shape[1]
    nc = length // SSM_CHUNK
    cur = lambda b, c: (b, c, 0)
    per_b = lambda b, c: (b, 0, 0)
    pad8 = lambda v: jnp.pad(v.reshape(1, SSM_HEADS), ((0, 0), (0, LANES - SSM_HEADS)))
    return pl.pallas_call(
        _ssd_kernel,
        grid=(bsz, nc),
        in_specs=[pl.BlockSpec((None, SSM_CHUNK, SSM_CONV_CH), cur),
                  pl.BlockSpec((None, SSM_CHUNK, SSM_INNER), cur),
                  pl.BlockSpec((None, SSM_CHUNK, LANES), cur),
                  _const_spec((CONV_WIDTH, SSM_CONV_CH)), _const_spec((1, SSM_CONV_CH)),
                  _const_spec((1, LANES)), _const_spec((1, LANES)),
                  pl.BlockSpec(memory_space=pltpu.SMEM),
                  _const_spec((1, SSM_INNER))],
        out_specs=[pl.BlockSpec((None, SSM_CHUNK, SSM_INNER), cur),
                   pl.BlockSpec((None, SSM_INNER, SSM_STATE), per_b)],
        out_shape=[jax.ShapeDtypeStruct((bsz, length, SSM_INNER), F32),
                   jax.ShapeDtypeStruct((bsz, SSM_INNER, SSM_STATE), F32)],
        scratch_shapes=[pltpu.VMEM((SSM_CHUNK + SUBLANES, SSM_CONV_CH), F32),
                        pltpu.VMEM((SSM_INNER, SSM_STATE), F32)],
        compiler_params=_cparams(("parallel", "arbitrary")),
        name="ssd_mixer",
    )(xbc, z, dt, conv_w, conv_b.reshape(1, SSM_CONV_CH), pad8(dt_bias), pad8(a_log),
      d_skip.reshape(1, SSM_HEADS), norm_g.reshape(1, SSM_INNER))


def _unit_lower_inverses(lmats, row, col):
    c = lmats[0].shape[0]
    mm = lambda a, b: jnp.dot(a.astype(BF16), b.astype(BF16), preferred_element_type=F32)
    eye = (row == col).astype(F32)
    blk = SUBLANES
    same = (row // blk) == (col // blk)
    xs = [jnp.where(same, -l, 0.0) for l in lmats]
    invs = [eye + x for x in xs]
    p = blk
    while p > 2:
        xs = [mm(x, x) for x in xs]
        invs = [i + mm(i, x) for i, x in zip(invs, xs)]
        p //= 2
    while blk < c:
        outer = ((row // (2 * blk)) == (col // (2 * blk))) & ((row // blk) != (col // blk))
        ts = [mm(i, jnp.where(outer, l, 0.0)) for i, l in zip(invs, lmats)]
        invs = [i - mm(t, i) for i, t in zip(invs, ts)]
        blk *= 2
    return invs


def _dn_kernel(qkv_ref, z_ref, ba_ref, cw_ref, dtb_ref, alog_ref, nrm_ref, o_ref, sout_ref, xx_scr, s_scr):
    nseq, c_len = qkv_ref.shape[0], qkv_ref.shape[1]
    seqs = range(nseq)
    chains = [(s, h) for s in seqs for h in range(DN_HEADS)]
    ids = range(len(chains))

    @pl.when(pl.program_id(1) == 0)
    def _():
        xx_scr[:, 0:SUBLANES, :] = jnp.zeros((nseq, SUBLANES, xx_scr.shape[2]), F32)
        s_scr[...] = jnp.zeros_like(s_scr)

    row = lax.broadcasted_iota(jnp.int32, (c_len, c_len), 0)
    col = lax.broadcasted_iota(jnp.int32, (c_len, c_len), 1)
    incl = row >= col
    strict = row > col
    tri = incl.astype(F32)
    nrm = nrm_ref[...]
    bf = lambda t: t.astype(BF16)
    mm = lambda a, b: jnp.dot(bf(a), bf(b), preferred_element_type=F32)

    act, beta, cum, cum_t, e_cum, e_rest, e_last = [], [], [], [], [], [], []
    for s in seqs:
        act.append(_silu(_chunk_conv(qkv_ref.at[s], xx_scr.at[s], cw_ref, c_len)))
        _carry_conv_tail(xx_scr.at[s], c_len)
        ba = ba_ref[s]
        beta.append(_sigmoid(ba))
        gate = -jnp.exp(alog_ref[...]) * _softplus(ba + dtb_ref[...])
        cum_s = jnp.dot(tri, gate, preferred_element_type=F32, precision=lax.Precision.HIGHEST)
        c_last = cum_s[c_len - 1:c_len, :]
        cum.append(cum_s)
        cum_t.append(cum_s.T)
        e_cum.append(jnp.exp(cum_s))
        e_rest.append(jnp.exp(c_last - cum_s))
        e_last.append(jnp.exp(c_last))

    q, k, kb, rhs, decay = [], [], [], [], []
    for i in ids:
        s, h = chains[i]
        gl = DN_HEADS + h
        q_h = act[s][:, h * DN_DK:(h + 1) * DN_DK]
        k_h = act[s][:, DN_QK + h * DN_DK:DN_QK + (h + 1) * DN_DK]
        v_h = act[s][:, 2 * DN_QK + h * DN_DV:2 * DN_QK + (h + 1) * DN_DV]
        q.append(q_h * lax.rsqrt(jnp.sum(q_h * q_h, axis=-1, keepdims=True) + EPS) * (DN_DK ** -0.5))
        k.append(k_h * lax.rsqrt(jnp.sum(k_h * k_h, axis=-1, keepdims=True) + EPS))
        beta_c = beta[s][:, h:h + 1]
        kb.append(k[i] * beta_c)
        rhs.append(jnp.concatenate([v_h * beta_c, kb[i] * e_cum[s][:, gl:gl + 1]], axis=1))
        decay.append(jnp.exp(jnp.where(incl, cum[s][:, gl:gl + 1] - cum_t[s][gl:gl + 1, :], -jnp.inf)))
    kq = [lax.dot_general(bf(jnp.concatenate([kb[i], q[i]], axis=0)), bf(k[i]), NT_DIMS,
                          preferred_element_type=F32) for i in ids]
    lmat = [jnp.where(strict, kq[i][:c_len] * decay[i], 0.0) for i in ids]
    attn = [kq[i][c_len:] * decay[i] for i in ids]
    tinv = _unit_lower_inverses(lmat, row, col)
    uw = [mm(tinv[i], rhs[i]) for i in ids]
    s_old = [s_scr[s, h * DN_DK:(h + 1) * DN_DK, :] for s, h in chains]
    qe = [q[i] * e_cum[s][:, DN_HEADS + h:DN_HEADS + h + 1] for i, (s, h) in enumerate(chains)]
    wq = [mm(jnp.concatenate([uw[i][:, DN_DV:], qe[i]], axis=0), s_old[i]) for i in ids]
    v_new = [uw[i][:, :DN_DV] - wq[i][:c_len] for i in ids]
    o = [wq[i][c_len:] + mm(attn[i], v_new[i]) for i in ids]
    for i in ids:
        s, h = chains[i]
        gl = DN_HEADS + h
        s_scr[s, h * DN_DK:(h + 1) * DN_DK, :] = s_old[i] * e_last[s][:, gl:gl + 1] + lax.dot_general(
            bf(k[i] * e_rest[s][:, gl:gl + 1]), bf(v_new[i]), TN_DIMS, preferred_element_type=F32)
    for s in seqs:
        z = z_ref[s]
        o_ref[s] = jnp.concatenate(
            [_rms(o[s * DN_HEADS + h], nrm) * _silu(z[:, h * DN_DV:(h + 1) * DN_DV]) for h in range(DN_HEADS)],
            axis=1)
    sout_ref[...] = s_scr[...]


def dn_mixer(qkv, z, ba, conv_w, dt_bias, a_log, norm_g, nseq):
    bsz, length = qkv.shape[0], qkv.shape[1]
    nc = length // DN_CHUNK
    cur = lambda b, c: (b, c, 0)
    per_b = lambda b, c: (b, 0, 0)
    pad_a = lambda v: jnp.pad(v.reshape(1, DN_HEADS), ((0, 0), (DN_HEADS, LANES - 2 * DN_HEADS)))
    return pl.pallas_call(
        _dn_kernel,
        grid=(bsz // nseq, nc),
        in_specs=[pl.BlockSpec((nseq, DN_CHUNK, DN_CONV_CH), cur),
                  pl.BlockSpec((nseq, DN_CHUNK, DN_VW), cur),
                  pl.BlockSpec((nseq, DN_CHUNK, LANES), cur),
                  _const_spec((CONV_WIDTH, DN_CONV_CH)),
                  _const_spec((1, LANES)), _const_spec((1, LANES)), _const_spec((1, DN_DV))],
        out_specs=[pl.BlockSpec((nseq, DN_CHUNK, DN_VW), cur),
                   pl.BlockSpec((nseq, DN_QK, DN_DV), per_b)],
        out_shape=[jax.ShapeDtypeStruct((bsz, length, DN_VW), F32),
                   jax.ShapeDtypeStruct((bsz, DN_QK, DN_DV), F32)],
        scratch_shapes=[pltpu.VMEM((nseq, DN_CHUNK + SUBLANES, DN_CONV_CH), F32),
                        pltpu.VMEM((nseq, DN_QK, DN_DV), F32)],
        compiler_params=_cparams(("parallel", "arbitrary")),
        name="dn_mixer",
    )(qkv, z, ba, conv_w, pad_a(dt_bias), pad_a(a_log), norm_g.reshape(1, DN_DV))


X_SUB = D_MODEL // LANES
ROW_SUB = 2 * SUBLANES


def _rows_to_slabs(x):
    parts = jnp.stack([x[:, s * LANES:(s + 1) * LANES] for s in range(X_SUB)], axis=0)
    return jnp.transpose(parts, (1, 0, 2))


def _slabs_to_rows(slab):
    parts = jnp.transpose(slab, (1, 0, 2))
    return jnp.concatenate([parts[s] for s in range(X_SUB)], axis=1)


def _resid_proj_kernel(x_ref, *refs, n_in):
    acc = x_ref[...]
    for a_ref, w_ref in zip(refs[:n_in], refs[n_in:2 * n_in]):
        acc = acc + jnp.dot(a_ref[...].astype(BF16), w_ref[...], preferred_element_type=F32)
    refs[2 * n_in][...] = acc


def resid_proj(x, acts, ws, tm):
    t = x.shape[0]
    n_in = len(acts)
    tok = lambda i: (i, 0)
    return pl.pallas_call(
        functools.partial(_resid_proj_kernel, n_in=n_in),
        grid=(t // tm,),
        in_specs=([pl.BlockSpec((tm, D_MODEL), tok)] + [pl.BlockSpec((tm, a.shape[1]), tok) for a in acts]
                  + [_const_spec(w.shape) for w in ws]),
        out_specs=pl.BlockSpec((tm, D_MODEL), tok),
        out_shape=jax.ShapeDtypeStruct((t, D_MODEL), F32),
        compiler_params=_cparams(("parallel",)),
        name="resid_proj",
    )(x, *acts, *ws)


def _split_bf16(v):
    hi = v.astype(BF16)
    return hi, (v - hi.astype(F32)).astype(BF16)


def _router_kernel(x_ref, g_ref, wr_hi_ref, wr_lo_ref, br_ref, xg_ref, grp_ref, rank_ref, cnt_ref, carry_scr):
    tm = x_ref.shape[0]

    @pl.when(pl.program_id(0) == 0)
    def _():
        carry_scr[...] = jnp.zeros_like(carry_scr)

    x = x_ref[...]
    h_hi, h_lo = _split_bf16(_rms(x, g_ref[...]))
    w_hi, w_lo = wr_hi_ref[...], wr_lo_ref[...]
    logits = (jnp.dot(h_hi, w_hi, preferred_element_type=F32) + jnp.dot(h_lo, w_hi, preferred_element_type=F32)
              + jnp.dot(h_hi, w_lo, preferred_element_type=F32)) + br_ref[...]
    lt = logits.T
    sub = lax.broadcasted_iota(jnp.int32, (SUBLANES, tm), 0)
    neg = -jnp.inf
    glog = jnp.where(sub < N_GROUPS, lt[N_EXPERTS:N_EXPERTS + SUBLANES], neg)
    gmax = jnp.max(glog, axis=0, keepdims=True)
    g_p = 1.0 / jnp.sum(jnp.exp(glog - gmax), axis=0, keepdims=True)
    g_i = jnp.min(jnp.where(glog == gmax, sub, SUBLANES), axis=0, keepdims=True)
    sel = lt[0:EXPERTS_PER_GROUP]
    for g in range(1, N_GROUPS):
        sel = jnp.where(g_i == g, lt[g * EXPERTS_PER_GROUP:(g + 1) * EXPERTS_PER_GROUP], sel)
    m1 = jnp.max(sel, axis=0, keepdims=True)
    zsum = jnp.sum(jnp.exp(sel - m1), axis=0, keepdims=True)
    i1 = jnp.min(jnp.where(sel == m1, sub, SUBLANES), axis=0, keepdims=True)
    sel2 = jnp.where(sub == i1, neg, sel)
    m2 = jnp.max(sel2, axis=0, keepdims=True)
    i2 = jnp.min(jnp.where(sel2 == m2, sub, SUBLANES), axis=0, keepdims=True)
    p1 = 1.0 / zsum
    p2 = jnp.exp(m2 - m1) / zsum
    gate1 = g_p * p1 / (p1 + p2)
    gate2 = g_p * p2 / (p1 + p2)
    gates = jnp.where(sub == i1, gate1, 0.0) + jnp.where(sub == i2, gate2, 0.0)

    onehot = (sub == g_i).astype(F32)
    before = (lax.broadcasted_iota(jnp.int32, (tm, tm), 0) < lax.broadcasted_iota(jnp.int32, (tm, tm), 1))
    rank_in_tile = jnp.dot(onehot.astype(BF16), before.astype(BF16), preferred_element_type=F32)
    carry = carry_scr[...]
    rank = jnp.sum(onehot * (rank_in_tile + carry[:, 0:1]), axis=0, keepdims=True)
    grp_ref[...] = g_i
    rank_ref[...] = rank.astype(jnp.int32)
    new_carry = carry + jnp.sum(onehot, axis=1, keepdims=True)
    carry_scr[...] = new_carry
    cnt_ref[...] = new_carry

    gates_rows = jnp.concatenate([gates, jnp.zeros((LANES - SUBLANES, tm), F32)], axis=0).T
    xg_ref[:, :X_SUB, :] = _rows_to_slabs(x)
    xg_ref[:, X_SUB:, :] = _rows_to_slabs(jnp.concatenate(
        [gates_rows, jnp.zeros((tm, D_MODEL - LANES), F32)], axis=1))


def moe_router(x, gain, w_router_pad, b_router_pad, tm):
    t = x.shape[0]
    w_hi = w_router_pad.astype(BF16)
    w_lo = (w_router_pad - w_hi.astype(F32)).astype(BF16)
    return pl.pallas_call(
        _router_kernel,
        grid=(t // tm,),
        in_specs=[pl.BlockSpec((tm, D_MODEL), lambda i: (i, 0)), _const_spec((1, D_MODEL)),
                  _const_spec((D_MODEL, LANES)), _const_spec((D_MODEL, LANES)), _const_spec((1, LANES))],
        out_specs=[pl.BlockSpec((tm, ROW_SUB, LANES), lambda i: (i, 0, 0)),
                   pl.BlockSpec((None, 1, tm), lambda i: (i, 0, 0)),
                   pl.BlockSpec((None, 1, tm), lambda i: (i, 0, 0)),
                   _const_spec((SUBLANES, LANES))],
        out_shape=[jax.ShapeDtypeStruct((t, ROW_SUB, LANES), F32),
                   jax.ShapeDtypeStruct((t // tm, 1, tm), jnp.int32),
                   jax.ShapeDtypeStruct((t // tm, 1, tm), jnp.int32),
                   jax.ShapeDtypeStruct((SUBLANES, LANES), F32)],
        scratch_shapes=[pltpu.VMEM((SUBLANES, LANES), F32)],
        compiler_params=_cparams(("arbitrary",)),
        name="moe_router",
    )(x, gain.reshape(1, D_MODEL), w_hi, w_lo, b_router_pad)


def _issue_row_copies(n_rows, make_copy):
    def trip(i, carry):
        for u in range(ROW_DMA_UNROLL):
            make_copy(i * ROW_DMA_UNROLL + u).start(priority=u % DMA_PRIORITIES)
        return carry

    lax.fori_loop(0, n_rows // ROW_DMA_UNROLL, trip, 0)


def _dispatch_kernel(pos_ref, pad_lo_ref, pad_hi_ref, xg_ref, xs_hbm, zero_scr, sem, zsem):
    tm = xg_ref.shape[0]
    _issue_row_copies(tm, lambda r: pltpu.make_async_copy(xg_ref.at[r], xs_hbm.at[pos_ref[0, r]], sem))
    pltpu.make_async_copy(xg_ref, xs_hbm.at[pl.ds(0, tm)], sem).wait()

    @pl.when(pl.program_id(0) == pl.num_programs(0) - 1)
    def _():
        zero_scr[...] = jnp.zeros_like(zero_scr)
        for k in range(pad_lo_ref.shape[0]):
            def fill(j, carry):
                pltpu.make_async_copy(zero_scr, xs_hbm.at[j], zsem).start()
                return carry

            def drain(j, carry):
                pltpu.make_async_copy(zero_scr, xs_hbm.at[j], zsem).wait()
                return carry

            lax.fori_loop(pad_lo_ref[k], pad_hi_ref[k], fill, 0)
            lax.fori_loop(pad_lo_ref[k], pad_hi_ref[k], drain, 0)


def moe_dispatch(pos, pad_lo, pad_hi, xg, tm, n_slots):
    t = xg.shape[0]
    return pl.pallas_call(
        _dispatch_kernel,
        grid=(t // tm,),
        in_specs=[pl.BlockSpec((None, 1, tm), lambda i: (i, 0, 0), memory_space=pltpu.SMEM),
                  pl.BlockSpec(memory_space=pltpu.SMEM), pl.BlockSpec(memory_space=pltpu.SMEM),
                  pl.BlockSpec((tm, ROW_SUB, LANES), lambda i: (i, 0, 0))],
        out_specs=pl.BlockSpec(memory_space=pl.ANY),
        out_shape=jax.ShapeDtypeStruct((n_slots, ROW_SUB, LANES), F32),
        scratch_shapes=[pltpu.VMEM((ROW_SUB, LANES), F32), pltpu.SemaphoreType.DMA, pltpu.SemaphoreType.DMA],
        compiler_params=_cparams(("arbitrary",), has_side_effects=True),
        name="moe_dispatch",
    )(pos, pad_lo, pad_hi, xg)


def _collect_kernel(pos_ref, os_hbm, out_ref, buf, sem):
    tm = out_ref.shape[0]
    _issue_row_copies(tm, lambda r: pltpu.make_async_copy(os_hbm.at[pos_ref[0, r]], buf.at[r], sem))
    pltpu.make_async_copy(os_hbm.at[pl.ds(0, tm)], buf, sem).wait()
    out_ref[...] = _slabs_to_rows(buf[...])


def moe_collect(pos, os_sorted, t, tm):
    return pl.pallas_call(
        _collect_kernel,
        grid=(t // tm,),
        in_specs=[pl.BlockSpec((None, 1, tm), lambda i: (i, 0, 0), memory_space=pltpu.SMEM),
                  pl.BlockSpec(memory_space=pl.ANY)],
        out_specs=pl.BlockSpec((tm, D_MODEL), lambda i: (i, 0)),
        out_shape=jax.ShapeDtypeStruct((t, D_MODEL), F32),
        scratch_shapes=[pltpu.VMEM((tm, X_SUB, LANES), F32), pltpu.SemaphoreType.DMA],
        compiler_params=_cparams(("arbitrary",), has_side_effects=True),
        name="moe_collect",
    )(pos, os_sorted)


def _group_experts_kernel(grp_ref, xg_ref, g_ref, wg_ref, wu_ref, wd_ref, o_ref):
    del grp_ref
    x = _slabs_to_rows(xg_ref[:, :X_SUB, :])
    gates = _slabs_to_rows(xg_ref[:, X_SUB:, :])[:, :LANES]
    h = _rms(x, g_ref[...]).astype(BF16)
    acc = x
    for e in range(EXPERTS_PER_GROUP):
        gate = jnp.dot(h, wg_ref[e], preferred_element_type=F32)
        up = jnp.dot(h, wu_ref[e], preferred_element_type=F32)
        act = (_silu(gate) * up).astype(BF16)
        acc = acc + gates[:, e:e + 1] * jnp.dot(act, wd_ref[e], preferred_element_type=F32)
    o_ref[...] = _rows_to_slabs(acc)


def moe_group_experts(tile_grp, xs, gain, wg, wu, wd, tile):
    n_steps = xs.shape[0] // tile
    rows = lambda i, grp: (i, 0, 0)
    wts = lambda i, grp: (grp[i], 0, 0, 0)
    return pl.pallas_call(
        _group_experts_kernel,
        grid_spec=pltpu.PrefetchScalarGridSpec(
            num_scalar_prefetch=1,
            grid=(n_steps,),
            in_specs=[pl.BlockSpec((tile, ROW_SUB, LANES), rows),
                      pl.BlockSpec((1, D_MODEL), lambda i, grp: (0, 0)),
                      pl.BlockSpec((None, EXPERTS_PER_GROUP, D_MODEL, EXPERT_FF), wts),
                      pl.BlockSpec((None, EXPERTS_PER_GROUP, D_MODEL, EXPERT_FF), wts),
                      pl.BlockSpec((None, EXPERTS_PER_GROUP, EXPERT_FF, D_MODEL), wts)],
            out_specs=pl.BlockSpec((tile, X_SUB, LANES), rows)),
        out_shape=jax.ShapeDtypeStruct((xs.shape[0], X_SUB, LANES), F32),
        compiler_params=_cparams(("arbitrary",)),
        name="moe_group_experts",
    )(tile_grp, xs, gain.reshape(1, D_MODEL), wg, wu, wd)


def mix_out_and_moe(x, acts, ws, gain, w_route, b_route, wg, wu, wd):
    t = x.shape[0]
    tile = MOE_TILE if t % MOE_TILE == 0 else t
    n_tiles = t // tile + N_GROUPS
    x = resid_proj(x, acts, ws, tile)
    xg, grp, rank, cnt = moe_router(x, gain, w_route, b_route, tile)
    counts = cnt[:N_GROUPS, 0].astype(jnp.int32)
    tile_ends = jnp.cumsum((counts + tile - 1) // tile)
    offs = (tile_ends - (counts + tile - 1) // tile) * tile
    pos = rank + sum(jnp.where(grp == g, offs[g], 0) for g in range(N_GROUPS))
    pad_lo = jnp.concatenate([offs + counts, tile_ends[-1:] * tile])
    pad_hi = jnp.concatenate([tile_ends * tile, jnp.full((1,), n_tiles * tile, jnp.int32)])
    tile_grp = jnp.minimum(jnp.sum((jnp.arange(n_tiles, dtype=jnp.int32)[:, None] >= tile_ends[None, :])
                                   .astype(jnp.int32), axis=1), N_GROUPS - 1)
    xs = moe_dispatch(pos, pad_lo, pad_hi, xg, tile, n_tiles * tile)
    os_sorted = moe_group_experts(tile_grp, xs, gain, wg, wu, wd, tile)
    return moe_collect(pos, os_sorted, t, tile)


BATCH_NT = (((2,), (2,)), ((0,), (0,)))
BATCH_NN = (((2,), (1,)), ((0,), (0,)))


def _token_major(parts):
    return jnp.transpose(jnp.stack(parts, axis=0), (1, 0, 2))


def _pad_rows(parts, tb, width):
    return parts + [jnp.zeros((tb, width), F32)] * (SUBLANES - len(parts))


def _spread(v, first, n, width):
    tb = v.shape[0]
    return jnp.concatenate([jnp.broadcast_to(v[:, first + i:first + i + 1], (tb, width)) for i in range(n)], axis=1)


def _columns(rows):
    tb, m = rows.shape
    return jnp.concatenate([rows, jnp.zeros((LANES - tb, m), F32)], axis=0).T


def _one_step_conv(x_ref, b0_ref, b1_ref, b2_ref, w_ref):
    return (w_ref[0:1, :] * b0_ref[...] + w_ref[1:2, :] * b1_ref[...] + w_ref[2:3, :] * b2_ref[...]
            + w_ref[3:4, :] * x_ref[...])


def _swa_decode_kernel(sink_ref, q_ref, k_ref, v_ref, kc_ref, vc_ref, qn_ref, kn_ref, o_ref, kout_ref):
    tb = q_ref.shape[0]
    lo = lax.broadcasted_iota(jnp.int32, (tb, LANES), 1) < HEAD_DIM

    def pair_rms(t, gain):
        sq = t * t
        s_lo = jnp.sum(jnp.where(lo, sq, 0.0), axis=-1, keepdims=True)
        s_hi = jnp.sum(jnp.where(lo, 0.0, sq), axis=-1, keepdims=True)
        return t * lax.rsqrt(jnp.where(lo, s_lo, s_hi) * (1.0 / HEAD_DIM) + EPS) * gain

    qn2 = jnp.concatenate([qn_ref[...], qn_ref[...]], axis=1)
    kn2 = jnp.concatenate([kn_ref[...], kn_ref[...]], axis=1)
    k_new = pair_rms(k_ref[...], kn2)
    kout_ref[...] = k_new
    v_new = v_ref[...]
    rows = []
    for h in range(SWA_HEADS):
        t = pair_rms(q_ref[:, (h // 2) * LANES:(h // 2 + 1) * LANES], qn2) * (HEAD_DIM ** -0.5)
        j = h // SWA_GROUP
        if h % 2 != j:
            t = pltpu.roll(t, HEAD_DIM, axis=1)
        rows.append(jnp.where(lo if j == 0 else jnp.logical_not(lo), t, 0.0))
    q8 = _token_major(rows)
    s = lax.dot_general(q8.astype(BF16), kc_ref[...].astype(BF16), BATCH_NT, preferred_element_type=F32)
    s_new = jnp.sum(q8 * k_new[:, None, :], axis=-1, keepdims=True)
    sub = lax.broadcasted_iota(jnp.int32, (1, SWA_HEADS, 1), 1)
    sink = jnp.zeros((1, SWA_HEADS, 1), F32)
    for h in range(SWA_HEADS):
        sink = jnp.where(sub == h, sink_ref[h], sink)
    m = jnp.maximum(jnp.maximum(jnp.max(s, axis=-1, keepdims=True), s_new), sink)
    p = jnp.exp(s - m)
    p_new = jnp.exp(s_new - m)
    den = jnp.sum(p, axis=-1, keepdims=True) + p_new + jnp.exp(sink - m)
    o8 = lax.dot_general((p / den).astype(BF16), vc_ref[...].astype(BF16), BATCH_NN,
                         preferred_element_type=F32) + (p_new / den) * v_new[:, None, :]
    o_h = jnp.transpose(o8, (1, 0, 2))
    tiles = []
    for t in range(SWA_HEADS // 2):
        halves = []
        for h in (2 * t, 2 * t + 1):
            piece = o_h[h]
            if h % 2 != h // SWA_GROUP:
                piece = pltpu.roll(piece, HEAD_DIM, axis=1)
            halves.append(piece)
        tiles.append(jnp.where(lo, halves[0], halves[1]))
    o_ref[...] = jnp.concatenate(tiles, axis=1)


def swa_decode(q, k, v, k_cache, v_cache, q_norm, k_norm, sinks):
    bsz = q.shape[0]
    tok = lambda i: (i, 0)
    tok3 = lambda i: (i, 0, 0)
    return pl.pallas_call(
        _swa_decode_kernel,
        grid=(bsz // DEC_TB,),
        in_specs=[pl.BlockSpec(memory_space=pltpu.SMEM),
                  pl.BlockSpec((DEC_TB, SWA_Q), tok), pl.BlockSpec((DEC_TB, SWA_KV), tok),
                  pl.BlockSpec((DEC_TB, SWA_KV), tok),
                  pl.BlockSpec((DEC_TB, WINDOW, SWA_KV), tok3), pl.BlockSpec((DEC_TB, WINDOW, SWA_KV), tok3),
                  _const_spec((1, HEAD_DIM)), _const_spec((1, HEAD_DIM))],
        out_specs=[pl.BlockSpec((DEC_TB, SWA_Q), tok), pl.BlockSpec((DEC_TB, SWA_KV), tok)],
        out_shape=[jax.ShapeDtypeStruct((bsz, SWA_Q), F32), jax.ShapeDtypeStruct((bsz, SWA_KV), F32)],
        compiler_params=_cparams(("parallel",)),
        name="swa_decode",
    )(sinks, q, k, v, k_cache, v_cache, q_norm.reshape(1, HEAD_DIM), k_norm.reshape(1, HEAD_DIM))


def _ssd_decode_kernel(x_ref, b0_ref, b1_ref, b2_ref, z_ref, dt_ref, cw_ref, cb_ref, dtb_ref, alog_ref, dd_ref,
                       nrm_ref, h0_ref, y_ref, hout_ref):
    tb = x_ref.shape[0]
    act = _silu(_one_step_conv(x_ref, b0_ref, b1_ref, b2_ref, cw_ref) + cb_ref[...])
    xs = act[:, :SSM_INNER]
    bm = act[:, SSM_INNER:SSM_INNER + SSM_GROUPS * SSM_STATE]
    cm = act[:, SSM_INNER + SSM_GROUPS * SSM_STATE:]
    dt = _softplus(dt_ref[...] + dtb_ref[...])
    dec = jnp.exp(dt * (-jnp.exp(alog_ref[...])))
    dt_x = _spread(dt, 0, SSM_HEADS, SSM_HEAD_DIM)
    dec_x = _spread(dec, 0, SSM_HEADS, SSM_HEAD_DIM)
    gw = SSM_INNER // SSM_GROUPS
    grp = lambda t, g: t[:, g * SSM_STATE:(g + 1) * SSM_STATE]
    cb_x = jnp.concatenate([jnp.broadcast_to(jnp.sum(grp(cm, g) * grp(bm, g), axis=-1, keepdims=True), (tb, gw))
                            for g in range(SSM_GROUPS)], axis=1)
    h0 = h0_ref[...]
    c8 = _token_major(_pad_rows([grp(cm, g) for g in range(SSM_GROUPS)], tb, SSM_STATE))
    ys = jnp.transpose(lax.dot_general(c8.astype(BF16), h0.astype(BF16), BATCH_NT, preferred_element_type=F32),
                       (1, 0, 2))
    y_state = jnp.concatenate([ys[g][:, g * gw:(g + 1) * gw] for g in range(SSM_GROUPS)], axis=1)
    y = cb_x * dt_x * xs + y_state * dec_x + dd_ref[...] * xs
    y = y * _silu(z_ref[...])
    nrm = nrm_ref[...]
    y_ref[...] = jnp.concatenate(
        [_rms(y[:, g * gw:(g + 1) * gw], nrm[:, g * gw:(g + 1) * gw]) for g in range(SSM_GROUPS)], axis=1)
    u_cols = _columns(dt_x * xs)
    hpg = SSM_HEADS // SSM_GROUPS
    for b in range(tb):
        for g in range(SSM_GROUPS):
            rows = slice(g * gw, (g + 1) * gw)
            decay = jnp.concatenate([jnp.broadcast_to(dec[b:b + 1, h:h + 1], (SSM_HEAD_DIM, SSM_STATE))
                                     for h in range(g * hpg, (g + 1) * hpg)], axis=0)
            hout_ref[b, rows, :] = h0[b, rows, :] * decay + u_cols[rows, b:b + 1] * grp(bm, g)[b:b + 1, :]


def ssd_decode(xbc, bufs, z, dt, conv_w, conv_b, dt_bias, a_log, d_skip, norm_g, h0):
    bsz = xbc.shape[0]
    tok = lambda i: (i, 0)
    tok3 = lambda i: (i, 0, 0)
    pad8 = lambda v: jnp.pad(v.reshape(1, SSM_HEADS), ((0, 0), (0, LANES - SSM_HEADS)))
    row = lambda w: pl.BlockSpec((DEC_TB, w), tok)
    return pl.pallas_call(
        _ssd_decode_kernel,
        grid=(bsz // DEC_TB,),
        in_specs=[row(SSM_CONV_CH)] * 4 + [row(SSM_INNER), row(LANES),
                  _const_spec((CONV_WIDTH, SSM_CONV_CH)), _const_spec((1, SSM_CONV_CH)),
                  _const_spec((1, LANES)), _const_spec((1, LANES)), _const_spec((1, SSM_INNER)),
                  _const_spec((1, SSM_INNER)),
                  pl.BlockSpec((DEC_TB, SSM_INNER, SSM_STATE), tok3)],
        out_specs=[row(SSM_INNER), pl.BlockSpec((DEC_TB, SSM_INNER, SSM_STATE), tok3)],
        out_shape=[jax.ShapeDtypeStruct((bsz, SSM_INNER), F32),
                   jax.ShapeDtypeStruct((bsz, SSM_INNER, SSM_STATE), F32)],
        compiler_params=_cparams(("parallel",)),
        name="ssd_decode",
    )(xbc, *bufs, z, dt, conv_w, conv_b.reshape(1, SSM_CONV_CH), pad8(dt_bias), pad8(a_log),
      jnp.repeat(d_skip, SSM_HEAD_DIM).reshape(1, SSM_INNER), norm_g.reshape(1, SSM_INNER), h0)


def _dn_decode_kernel(x_ref, b0_ref, b1_ref, b2_ref, z_ref, ba_ref, cw_ref, dtb_ref, alog_ref, nrm_ref, s0_ref,
                      o_ref, sout_ref):
    tb = x_ref.shape[0]
    act = _silu(_one_step_conv(x_ref, b0_ref, b1_ref, b2_ref, cw_ref))
    ba = ba_ref[...]
    beta = _sigmoid(ba)
    eg = jnp.exp(-jnp.exp(alog_ref[...]) * _softplus(ba + dtb_ref[...]))
    z = z_ref[...]
    nrm = nrm_ref[...]
    outs = []
    for h in range(DN_HEADS):
        q_h = act[:, h * DN_DK:(h + 1) * DN_DK]
        k_h = act[:, DN_QK + h * DN_DK:DN_QK + (h + 1) * DN_DK]
        v_h = act[:, 2 * DN_QK + h * DN_DV:2 * DN_QK + (h + 1) * DN_DV]
        q_h = q_h * lax.rsqrt(jnp.sum(q_h * q_h, axis=-1, keepdims=True) + EPS) * (DN_DK ** -0.5)
        k_h = k_h * lax.rsqrt(jnp.sum(k_h * k_h, axis=-1, keepdims=True) + EPS)
        beta_c = beta[:, h:h + 1]
        eg_c = eg[:, DN_HEADS + h:DN_HEADS + h + 1]
        s_h = s0_ref[:, h * DN_DK:(h + 1) * DN_DK, :]
        kq8 = _token_major(_pad_rows([k_h, q_h], tb, DN_DK))
        r = jnp.transpose(lax.dot_general(kq8.astype(BF16), s_h.astype(BF16), BATCH_NN,
                                          preferred_element_type=F32), (1, 0, 2))
        v_new = beta_c * v_h - (beta_c * eg_c) * r[0]
        o_h = eg_c * r[1] + jnp.sum(q_h * k_h, axis=-1, keepdims=True) * v_new
        outs.append(_rms(o_h, nrm) * _silu(z[:, h * DN_DV:(h + 1) * DN_DV]))
        k_cols = _columns(k_h)
        for b in range(tb):
            sout_ref[b, h * DN_DK:(h + 1) * DN_DK, :] = (
                s_h[b] * jnp.broadcast_to(eg_c[b:b + 1, :], (DN_DK, DN_DV)) + k_cols[:, b:b + 1] * v_new[b:b + 1, :])
    o_ref[...] = jnp.concatenate(outs, axis=1)


def dn_decode(qkv, bufs, z, ba, conv_w, dt_bias, a_log, norm_g, s0):
    bsz = qkv.shape[0]
    tok = lambda i: (i, 0)
    tok3 = lambda i: (i, 0, 0)
    pad_a = lambda v: jnp.pad(v.reshape(1, DN_HEADS), ((0, 0), (DN_HEADS, LANES - 2 * DN_HEADS)))
    row = lambda w: pl.BlockSpec((DEC_TB, w), tok)
    return pl.pallas_call(
        _dn_decode_kernel,
        grid=(bsz // DEC_TB,),
        in_specs=[row(DN_CONV_CH)] * 4 + [row(DN_VW), row(LANES),
                  _const_spec((CONV_WIDTH, DN_CONV_CH)),
                  _const_spec((1, LANES)), _const_spec((1, LANES)), _const_spec((1, DN_DV)),
                  pl.BlockSpec((DEC_TB, DN_QK, DN_DV), tok3)],
        out_specs=[row(DN_VW), pl.BlockSpec((DEC_TB, DN_QK, DN_DV), tok3)],
        out_shape=[jax.ShapeDtypeStruct((bsz, DN_VW), F32), jax.ShapeDtypeStruct((bsz, DN_QK, DN_DV), F32)],
        compiler_params=_cparams(("parallel",)),
        name="dn_decode",
    )(qkv, *bufs, z, ba, conv_w, pad_a(dt_bias), pad_a(a_log), norm_g.reshape(1, DN_DV), s0)


def _pad_cols(w, n_pad):
    return jnp.pad(w, ((0, 0), (0, n_pad - w.shape[1])))


EVEN_SEGS = ((0, 512), (512, 640), (640, 768), (768, 1280), (1280, 2304), (2304, 2432))
ODD_SEGS = ((0, 3072), (3072, 4096), (4096, 4224))


def _trunk(x_seq, p, states):
    prompt = states is None
    bsz, length = x_seq.shape[0], x_seq.shape[1]
    t = bsz * length
    tm = 512 if t % 512 == 0 else t
    x = x_seq.reshape(t, D_MODEL)
    seq = lambda u: u.reshape(bsz, length, u.shape[-1])

    q, k, v, z, xbc, dt = prenorm_proj(x, p['ln_mix'][0], p['w_in_even'], EVEN_SEGS, tm)
    if prompt:
        v3, xbc3 = seq(v), seq(xbc)
        att, k_normed = swa_attention(seq(q), seq(k), v3, p['q_norm'], p['k_norm'], p['attn_sinks'])
        new_k = k_normed[:, -WINDOW:].reshape(bsz, WINDOW, SWA_KV_HEADS, HEAD_DIM)
        new_v = v3[:, -WINDOW:].reshape(bsz, WINDOW, SWA_KV_HEADS, HEAD_DIM)
        new_ssm_conv = xbc3[:, -(CONV_WIDTH - 1):]
        y_ssm, new_h = ssd_mixer(xbc3, seq(z), seq(dt), p['ssm_conv_w'], p['ssm_conv_b'], p['ssm_dt_bias'],
                                 p['ssm_A_log'], p['ssm_D'], p['ssm_norm'])
    else:
        k_win, v_win, ssm_h, ssm_conv, dn_s, dn_conv = states
        kb = k_win.reshape(bsz, WINDOW, SWA_KV)
        vb = v_win.reshape(bsz, WINDOW, SWA_KV)
        att, k_normed = swa_decode(q, k, v, kb, vb, p['q_norm'], p['k_norm'], p['attn_sinks'])
        new_k = jnp.concatenate([kb[:, 1:], k_normed[:, None]], axis=1).reshape(bsz, WINDOW, SWA_KV_HEADS, HEAD_DIM)
        new_v = jnp.concatenate([vb[:, 1:], v[:, None]], axis=1).reshape(bsz, WINDOW, SWA_KV_HEADS, HEAD_DIM)
        new_ssm_conv = jnp.concatenate([ssm_conv[:, 1:], xbc[:, None, :]], axis=1)
        y_ssm, new_h = ssd_decode(xbc, [ssm_conv[:, i] for i in range(CONV_WIDTH - 1)], z, dt, p['ssm_conv_w'],
                                  p['ssm_conv_b'], p['ssm_dt_bias'], p['ssm_A_log'], p['ssm_D'], p['ssm_norm'],
                                  ssm_h.reshape(bsz, SSM_INNER, SSM_STATE))
    x = mix_out_and_moe(x, [att.reshape(t, SWA_Q), y_ssm.reshape(t, SSM_INNER)],
                        [p['w_out_even'][:SWA_Q], p['w_out_even'][SWA_Q:]],
                        p['ln_ffn'][0], p['w_route'][0], p['b_route'][0],
                        p['moe_w_gate'][0], p['moe_w_up'][0], p['moe_w_down'][0])

    qkv, zz, ba = prenorm_proj(x, p['ln_mix'][1], p['w_in_odd'], ODD_SEGS, tm // 2 if tm >= 512 else tm)
    if prompt:
        qkv3 = seq(qkv)
        new_dn_conv = qkv3[:, -(CONV_WIDTH - 1):]
        o_dn, new_s = dn_mixer(qkv3, seq(zz), seq(ba), p['dn_conv_w'], p['dn_dt_bias'], p['dn_A_log'],
                               p['dn_norm'], DN_SEQS_PER_STEP)
    else:
        new_dn_conv = jnp.concatenate([dn_conv[:, 1:], qkv[:, None, :]], axis=1)
        o_dn, new_s = dn_decode(qkv, [dn_conv[:, i] for i in range(CONV_WIDTH - 1)], zz, ba, p['dn_conv_w'],
                                p['dn_dt_bias'], p['dn_A_log'], p['dn_norm'], dn_s.reshape(bsz, DN_QK, DN_DV))
    x = mix_out_and_moe(x, [o_dn.reshape(t, DN_VW)], [p['w_out_odd']],
                        p['ln_ffn'][1], p['w_route'][1], p['b_route'][1],
                        p['moe_w_gate'][1], p['moe_w_up'][1], p['moe_w_down'][1])

    return (x.reshape(bsz, length, D_MODEL), new_k[None], new_v[None],
            new_h.reshape(1, bsz, SSM_HEADS, SSM_HEAD_DIM, SSM_STATE), new_ssm_conv[None],
            new_s.reshape(1, bsz, DN_HEADS, DN_DK, DN_DV), new_dn_conv[None])


def kernel(x_prompt, x_sample, cache_k_win, cache_v_win, state_ssm, state_ssm_conv, state_dn, state_dn_conv,
           ln_mix, ln_ffn, w_in_even, q_norm, k_norm, attn_sinks, ssm_conv_w, ssm_conv_b, ssm_dt_bias,
           ssm_A_log, ssm_D, ssm_norm, w_out_even, w_in_odd, dn_conv_w, dn_dt_bias, dn_A_log, dn_norm,
           w_out_odd, moe_w_group, moe_b_group, moe_w_router, moe_b_router, moe_w_gate, moe_w_up, moe_w_down):
    w_route = _pad_cols(jnp.concatenate([moe_w_router, moe_w_group], axis=-1).reshape(-1, N_EXPERTS + N_GROUPS),
                        LANES).reshape(2, D_MODEL, LANES)
    b_route = _pad_cols(jnp.concatenate([moe_b_router, moe_b_group], axis=-1), LANES).reshape(2, 1, LANES)
    p = {
        'ln_mix': ln_mix, 'ln_ffn': ln_ffn,
        'w_in_even': _pad_cols(w_in_even[0], EVEN_SEGS[-1][1]).astype(BF16),
        'q_norm': q_norm[0], 'k_norm': k_norm[0], 'attn_sinks': attn_sinks[0],
        'ssm_conv_w': ssm_conv_w[0], 'ssm_conv_b': ssm_conv_b[0], 'ssm_dt_bias': ssm_dt_bias[0],
        'ssm_A_log': ssm_A_log[0], 'ssm_D': ssm_D[0], 'ssm_norm': ssm_norm[0],
        'w_out_even': w_out_even[0].astype(BF16),
        'w_in_odd': _pad_cols(w_in_odd[0], ODD_SEGS[-1][1]).astype(BF16),
        'dn_conv_w': dn_conv_w[0], 'dn_dt_bias': dn_dt_bias[0], 'dn_A_log': dn_A_log[0], 'dn_norm': dn_norm[0],
        'w_out_odd': w_out_odd[0].astype(BF16),
        'w_route': w_route, 'b_route': b_route,
        'moe_w_gate': moe_w_gate.astype(BF16).reshape(2, N_GROUPS, EXPERTS_PER_GROUP, D_MODEL, EXPERT_FF),
        'moe_w_up': moe_w_up.astype(BF16).reshape(2, N_GROUPS, EXPERTS_PER_GROUP, D_MODEL, EXPERT_FF),
        'moe_w_down': moe_w_down.astype(BF16).reshape(2, N_GROUPS, EXPERTS_PER_GROUP, EXPERT_FF, D_MODEL),
    }
    y_p, kp, vp, sp, scp, dnp, dncp = _trunk(x_prompt, p, None)
    sample_states = (cache_k_win[0], cache_v_win[0], state_ssm[0], state_ssm_conv[0], state_dn[0],
                     state_dn_conv[0])
    y_s, ks, vs, ss, scs, dns, dncs = _trunk(x_sample, p, sample_states)
    return (y_p, y_s, kp, ks, vp, vs, sp, ss, scp, scs, dnp, dns, dncp, dncs)
```

```python
import functools

import jax
import jax.numpy as jnp
from jax import lax
from jax.experimental import pallas as pl
from jax.experimental.pallas import tpu as pltpu

F32 = jnp.float32
BF16 = jnp.bfloat16
EPS = 1e-6

D_MODEL = 1024
SWA_HEADS = 8
SWA_KV_HEADS = 2
SWA_GROUP = SWA_HEADS // SWA_KV_HEADS
HEAD_DIM = 64
WINDOW = 128
SWA_Q = SWA_HEADS * HEAD_DIM
SWA_KV = SWA_KV_HEADS * HEAD_DIM
SSM_HEADS = 8
SSM_HEAD_DIM = 64
SSM_GROUPS = 2
SSM_STATE = 128
SSM_INNER = SSM_HEADS * SSM_HEAD_DIM
SSM_CHUNK = 128
SSM_CONV_CH = SSM_INNER + 2 * SSM_GROUPS * SSM_STATE
CONV_WIDTH = 4
DN_HEADS = 8
DN_DK = 128
DN_DV = 128
DN_CHUNK = 64
DN_QK = DN_HEADS * DN_DK
DN_VW = DN_HEADS * DN_DV
DN_CONV_CH = 2 * DN_QK + DN_VW
N_GROUPS = 4
EXPERTS_PER_GROUP = 8
N_EXPERTS = N_GROUPS * EXPERTS_PER_GROUP
EXPERT_FF = 256

LANES = 128
SUBLANES = 8
VMEM_LIMIT = 56 * 1024 * 1024
DMA_PRIORITIES = 2

DN_SEQS_PER_STEP = 2
MOE_TILE = 512
DEC_TB = 8
ROW_DMA_UNROLL = 8

NT_DIMS = (((1,), (1,)), ((), ()))
TN_DIMS = (((0,), (0,)), ((), ()))


def _cparams(sem, **kw):
    return pltpu.CompilerParams(dimension_semantics=sem, vmem_limit_bytes=VMEM_LIMIT, **kw)


def _const_spec(shape):
    nd = len(shape)
    return pl.BlockSpec(shape, lambda *_: (0,) * nd)


def _sigmoid(x):
    return 1.0 / (1.0 + jnp.exp(-x))


def _silu(x):
    return x * _sigmoid(x)


def _softplus(x):
    return jnp.maximum(x, 0.0) + jnp.log(1.0 + jnp.exp(-jnp.abs(x)))


def _rms(x, gain):
    return x * lax.rsqrt(jnp.mean(x * x, axis=-1, keepdims=True) + EPS) * gain


def _prenorm_proj_kernel(x_ref, g_ref, w_ref, *out_refs, segs):
    h = _rms(x_ref[...], g_ref[...]).astype(BF16)
    for o_ref, (a, b) in zip(out_refs, segs):
        o_ref[...] = jnp.dot(h, w_ref[:, a:b], preferred_element_type=F32)


def prenorm_proj(x, gain, w_bf16, segs, tm):
    t = x.shape[0]
    n_pad = w_bf16.shape[1]
    return pl.pallas_call(
        functools.partial(_prenorm_proj_kernel, segs=segs),
        grid=(t // tm,),
        in_specs=[pl.BlockSpec((tm, D_MODEL), lambda i: (i, 0)),
                  _const_spec((1, D_MODEL)),
                  _const_spec((D_MODEL, n_pad))],
        out_specs=[pl.BlockSpec((tm, b - a), lambda i: (i, 0)) for a, b in segs],
        out_shape=[jax.ShapeDtypeStruct((t, b - a), F32) for a, b in segs],
        compiler_params=_cparams(("parallel",)),
        name="prenorm_proj",
    )(x, gain.reshape(1, D_MODEL), w_bf16)


def _swa_kernel(sink_ref, q_ref, kc_ref, kp_ref, vc_ref, vp_ref, qn_ref, kn_ref, o_ref, kout_ref):
    n = pl.program_id(1)
    blk = q_ref.shape[0]
    q = q_ref[...]
    kc, kp, vc, vp = kc_ref[...], kp_ref[...], vc_ref[...], vp_ref[...]
    qn, kn = qn_ref[...], kn_ref[...]
    row = lax.broadcasted_iota(jnp.int32, (blk, 2 * blk), 0)
    col = lax.broadcasted_iota(jnp.int32, (blk, 2 * blk), 1)
    rel = row + blk - col
    mask = (rel >= 0) & (rel <= WINDOW) & ((n > 0) | (col >= blk))
    outs, kouts = [], []
    for j in range(SWA_KV_HEADS):
        sl = slice(j * HEAD_DIM, (j + 1) * HEAD_DIM)
        kcj = _rms(kc[:, sl], kn)
        kpj = _rms(kp[:, sl], kn)
        kouts.append(kcj)
        kcat = jnp.concatenate([kpj, kcj], axis=0).astype(BF16)
        vcat = jnp.concatenate([vp[:, sl], vc[:, sl]], axis=0).astype(BF16)
        for g in range(SWA_GROUP):
            h = j * SWA_GROUP + g
            qh = _rms(q[:, h * HEAD_DIM:(h + 1) * HEAD_DIM], qn) * (HEAD_DIM ** -0.5)
            s = lax.dot_general(qh.astype(BF16), kcat, NT_DIMS, preferred_element_type=F32)
            s = jnp.where(mask, s, -jnp.inf)
            sink = sink_ref[h]
            m = jnp.maximum(jnp.max(s, axis=-1, keepdims=True), sink)
            p = jnp.exp(s - m)
            p = p / (jnp.sum(p, axis=-1, keepdims=True) + jnp.exp(sink - m))
            outs.append(jnp.dot(p.astype(BF16), vcat, preferred_element_type=F32))
    o_ref[...] = jnp.concatenate(outs, axis=1)
    kout_ref[...] = jnp.concatenate(kouts, axis=1)


def swa_attention(q, k, v, q_norm, k_norm, sinks):
    bsz, length = q.shape[0], q.shape[1]
    nb = length // WINDOW
    prev = lambda b, n: (b, jnp.maximum(n - 1, 0), 0)
    cur = lambda b, n: (b, n, 0)
    kv_blk = (None, WINDOW, SWA_KV)
    return pl.pallas_call(
        _swa_kernel,
        grid=(bsz, nb),
        in_specs=[pl.BlockSpec(memory_space=pltpu.SMEM),
                  pl.BlockSpec((None, WINDOW, SWA_Q), cur),
                  pl.BlockSpec(kv_blk, cur), pl.BlockSpec(kv_blk, prev),
                  pl.BlockSpec(kv_blk, cur), pl.BlockSpec(kv_blk, prev),
                  _const_spec((1, HEAD_DIM)), _const_spec((1, HEAD_DIM))],
        out_specs=[pl.BlockSpec((None, WINDOW, SWA_Q), cur), pl.BlockSpec(kv_blk, cur)],
        out_shape=[jax.ShapeDtypeStruct((bsz, length, SWA_Q), F32),
                   jax.ShapeDtypeStruct((bsz, length, SWA_KV), F32)],
        compiler_params=_cparams(("parallel", "arbitrary")),
        name="swa_attention",
    )(sinks, q, k, k, v, v, q_norm.reshape(1, HEAD_DIM), k_norm.reshape(1, HEAD_DIM))


def _chunk_conv(x_ref, xx_scr, w_ref, rows):
    xx_scr[SUBLANES:SUBLANES + rows, :] = x_ref[...]
    acc = None
    for tap in range(CONV_WIDTH):
        off = SUBLANES - (CONV_WIDTH - 1) + tap
        term = w_ref[tap:tap + 1, :] * xx_scr[off:off + rows, :]
        acc = term if acc is None else acc + term
    return acc


def _carry_conv_tail(xx_scr, rows):
    xx_scr[0:SUBLANES, :] = xx_scr[rows:rows + SUBLANES, :]


def _ssd_kernel(xbc_ref, z_ref, dt_ref, cw_ref, cb_ref, dtb_ref, alog_ref, dd_ref, nrm_ref,
                y_ref, hout_ref, xx_scr, h_scr):
    q_len = xbc_ref.shape[0]

    @pl.when(pl.program_id(1) == 0)
    def _():
        xx_scr[0:SUBLANES, :] = jnp.zeros((SUBLANES, xx_scr.shape[1]), F32)
        h_scr[...] = jnp.zeros_like(h_scr)

    act = _silu(_chunk_conv(xbc_ref, xx_scr, cw_ref, q_len) + cb_ref[...])
    _carry_conv_tail(xx_scr, q_len)
    xs = act[:, :SSM_INNER]
    bm = act[:, SSM_INNER:SSM_INNER + SSM_GROUPS * SSM_STATE]
    cm = act[:, SSM_INNER + SSM_GROUPS * SSM_STATE:]

    row = lax.broadcasted_iota(jnp.int32, (q_len, q_len), 0)
    col = lax.broadcasted_iota(jnp.int32, (q_len, q_len), 1)
    causal = row >= col
    dt = _softplus(dt_ref[...] + dtb_ref[...])
    da = dt * (-jnp.exp(alog_ref[...]))
    cum = jnp.dot(causal.astype(F32), da, preferred_element_type=F32, precision=lax.Precision.HIGHEST)
    cum_t = cum.T
    dt_t = dt.T
    e_cum = jnp.exp(cum)
    hpg = SSM_HEADS // SSM_GROUPS
    gw = hpg * SSM_HEAD_DIM
    ys = []
    for g in range(SSM_GROUPS):
        bm_g = bm[:, g * SSM_STATE:(g + 1) * SSM_STATE].astype(BF16)
        cm_g = cm[:, g * SSM_STATE:(g + 1) * SSM_STATE].astype(BF16)
        cb = lax.dot_general(cm_g, bm_g, NT_DIMS, preferred_element_type=F32)
        h_g = h_scr[g * gw:(g + 1) * gw, :]
        y_state = lax.dot_general(cm_g, h_g.astype(BF16), NT_DIMS, preferred_element_type=F32)
        xt_parts, dec_parts = [], []
        for hh in range(hpg):
            h = g * hpg + hh
            x_h = xs[:, h * SSM_HEAD_DIM:(h + 1) * SSM_HEAD_DIM]
            cum_c = cum[:, h:h + 1]
            seg = jnp.exp(jnp.where(causal, cum_c - cum_t[h:h + 1, :], -jnp.inf))
            wgt = cb * seg * dt_t[h:h + 1, :]
            y = jnp.dot(wgt.astype(BF16), x_h.astype(BF16), preferred_element_type=F32)
            y = y + y_state[:, hh * SSM_HEAD_DIM:(hh + 1) * SSM_HEAD_DIM] * e_cum[:, h:h + 1]
            ys.append(y + dd_ref[0, h] * x_h)
            c_last = cum[q_len - 1:q_len, h:h + 1]
            xt_parts.append(x_h * (jnp.exp(c_last - cum_c) * dt[:, h:h + 1]))
            dec_parts.append(jnp.broadcast_to(jnp.exp(c_last), (SSM_HEAD_DIM, SSM_STATE)))
        xt = jnp.concatenate(xt_parts, axis=1).astype(BF16)
        upd = lax.dot_general(xt, bm_g, TN_DIMS, preferred_element_type=F32)
        h_scr[g * gw:(g + 1) * gw, :] = h_g * jnp.concatenate(dec_parts, axis=0) + upd
    y_all = jnp.concatenate(ys, axis=1) * _silu(z_ref[...])
    nrm = nrm_ref[...]
    y_ref[...] = jnp.concatenate(
        [_rms(y_all[:, g * gw:(g + 1) * gw], nrm[:, g * gw:(g + 1) * gw]) for g in range(SSM_GROUPS)], axis=1)
    hout_ref[...] = h_scr[...]


def ssd_mixer(xbc, z, dt, conv_w, conv_b, dt_bias, a_log, d_skip, norm_g):
    bsz, length = xbc.shape[0], xbc.shape[1]
    nc = length // SSM_CHUNK
    cur = lambda b, c: (b, c, 0)
    per_b = lambda b, c: (b, 0, 0)
    pad8 = lambda v: jnp.pad(v.reshape(1, SSM_HEADS), ((0, 0), (0, LANES - SSM_HEADS)))
    return pl.pallas_call(
        _ssd_kernel,
        grid=(bsz, nc),
        in_specs=[pl.BlockSpec((None, SSM_CHUNK, SSM_CONV_CH), cur),
                  pl.BlockSpec((None, SSM_CHUNK, SSM_INNER), cur),
                  pl.BlockSpec((None, SSM_CHUNK, LANES), cur),
                  _const_spec((CONV_WIDTH, SSM_CONV_CH)), _const_spec((1, SSM_CONV_CH)),
                  _const_spec((1, LANES)), _const_spec((1, LANES)),
                  pl.BlockSpec(memory_space=pltpu.SMEM),
                  _const_spec((1, SSM_INNER))],
        out_specs=[pl.BlockSpec((None, SSM_CHUNK, SSM_INNER), cur),
                   pl.BlockSpec((None, SSM_INNER, SSM_STATE), per_b)],
        out_shape=[jax.ShapeDtypeStruct((bsz, length, SSM_INNER), F32),
                   jax.ShapeDtypeStruct((bsz, SSM_INNER, SSM_STATE), F32)],
        scratch_shapes=[pltpu.VMEM((SSM_CHUNK + SUBLANES, SSM_CONV_CH), F32),
                        pltpu.VMEM((SSM_INNER, SSM_STATE), F32)],
        compiler_params=_cparams(("parallel", "arbitrary")),
        name="ssd_mixer",
    )(xbc, z, dt, conv_w, conv_b.reshape(1, SSM_CONV_CH), pad8(dt_bias), pad8(a_log),
      d_skip.reshape(1, SSM_HEADS), norm_g.reshape(1, SSM_INNER))


def _unit_lower_inverses(lmats, row, col):
    c = lmats[0].shape[0]
    mm = lambda a, b: jnp.dot(a.astype(BF16), b.astype(BF16), preferred_element_type=F32)
    eye = (row == col).astype(F32)
    blk = SUBLANES
    same = (row // blk) == (col // blk)
    xs = [jnp.where(same, -l, 0.0) for l in lmats]
    invs = [eye + x for x in xs]
    p = blk
    while p > 2:
        xs = [mm(x, x) for x in xs]
        invs = [i + mm(i, x) for i, x in zip(invs, xs)]
        p //= 2
    while blk < c:
        outer = ((row // (2 * blk)) == (col // (2 * blk))) & ((row // blk) != (col // blk))
        ts = [mm(i, jnp.where(outer, l, 0.0)) for i, l in zip(invs, lmats)]
        invs = [i - mm(t, i) for i, t in zip(invs, ts)]
        blk *= 2
    return invs


def _dn_kernel(qkv_ref, z_ref, ba_ref, cw_ref, dtb_ref, alog_ref, nrm_ref, o_ref, sout_ref, xx_scr, s_scr):
    nseq, c_len = qkv_ref.shape[0], qkv_ref.shape[1]
    seqs = range(nseq)
    chains = [(s, h) for s in seqs for h in range(DN_HEADS)]
    ids = range(len(chains))

    @pl.when(pl.program_id(1) == 0)
    def _():
        xx_scr[:, 0:SUBLANES, :] = jnp.zeros((nseq, SUBLANES, xx_scr.shape[2]), F32)
        s_scr[...] = jnp.zeros_like(s_scr)

    row = lax.broadcasted_iota(jnp.int32, (c_len, c_len), 0)
    col = lax.broadcasted_iota(jnp.int32, (c_len, c_len), 1)
    incl = row >= col
    strict = row > col
    tri = incl.astype(F32)
    nrm = nrm_ref[...]
    bf = lambda t: t.astype(BF16)
    mm = lambda a, b: jnp.dot(bf(a), bf(b), preferred_element_type=F32)

    act, beta, cum, cum_t, e_cum, e_rest, e_last = [], [], [], [], [], [], []
    for s in seqs:
        act.append(_silu(_chunk_conv(qkv_ref.at[s], xx_scr.at[s], cw_ref, c_len)))
        _carry_conv_tail(xx_scr.at[s], c_len)
        ba = ba_ref[s]
        beta.append(_sigmoid(ba))
        gate = -jnp.exp(alog_ref[...]) * _softplus(ba + dtb_ref[...])
        cum_s = jnp.dot(tri, gate, preferred_element_type=F32, precision=lax.Precision.HIGHEST)
        c_last = cum_s[c_len - 1:c_len, :]
        cum.append(cum_s)
        cum_t.append(cum_s.T)
        e_cum.append(jnp.exp(cum_s))
        e_rest.append(jnp.exp(c_last - cum_s))
        e_last.append(jnp.exp(c_last))

    q, k, kb, rhs, decay = [], [], [], [], []
    for i in ids:
        s, h = chains[i]
        gl = DN_HEADS + h
        q_h = act[s][:, h * DN_DK:(h + 1) * DN_DK]
        k_h = act[s][:, DN_QK + h * DN_DK:DN_QK + (h + 1) * DN_DK]
        v_h = act[s][:, 2 * DN_QK + h * DN_DV:2 * DN_QK + (h + 1) * DN_DV]
        q.append(q_h * lax.rsqrt(jnp.sum(q_h * q_h, axis=-1, keepdims=True) + EPS) * (DN_DK ** -0.5))
        k.append(k_h * lax.rsqrt(jnp.sum(k_h * k_h, axis=-1, keepdims=True) + EPS))
        beta_c = beta[s][:, h:h + 1]
        kb.append(k[i] * beta_c)
        rhs.append(jnp.concatenate([v_h * beta_c, kb[i] * e_cum[s][:, gl:gl + 1]], axis=1))
        decay.append(jnp.exp(jnp.where(incl, cum[s][:, gl:gl + 1] - cum_t[s][gl:gl + 1, :], -jnp.inf)))
    kq = [lax.dot_general(bf(jnp.concatenate([kb[i], q[i]], axis=0)), bf(k[i]), NT_DIMS,
                          preferred_element_type=F32) for i in ids]
    lmat = [jnp.where(strict, kq[i][:c_len] * decay[i], 0.0) for i in ids]
    attn = [kq[i][c_len:] * decay[i] for i in ids]
    tinv = _unit_lower_inverses(lmat, row, col)
    uw = [mm(tinv[i], rhs[i]) for i in ids]
    s_old = [s_scr[s, h * DN_DK:(h + 1) * DN_DK, :] for s, h in chains]
    qe = [q[i] * e_cum[s][:, DN_HEADS + h:DN_HEADS + h + 1] for i, (s, h) in enumerate(chains)]
    wq = [mm(jnp.concatenate([uw[i][:, DN_DV:], qe[i]], axis=0), s_old[i]) for i in ids]
    v_new = [uw[i][:, :DN_DV] - wq[i][:c_len] for i in ids]
    o = [wq[i][c_len:] + mm(attn[i], v_new[i]) for i in ids]
    for i in ids:
        s, h = chains[i]
        gl = DN_HEADS + h
        s_scr[s, h * DN_DK:(h + 1) * DN_DK, :] = s_old[i] * e_last[s][:, gl:gl + 1] + lax.dot_general(
            bf(k[i] * e_rest[s][:, gl:gl + 1]), bf(v_new[i]), TN_DIMS, preferred_element_type=F32)
    for s in seqs:
        z = z_ref[s]
        o_ref[s] = jnp.concatenate(
            [_rms(o[s * DN_HEADS + h], nrm) * _silu(z[:, h * DN_DV:(h + 1) * DN_DV]) for h in range(DN_HEADS)],
            axis=1)
    sout_ref[...] = s_scr[...]


def dn_mixer(qkv, z, ba, conv_w, dt_bias, a_log, norm_g, nseq):
    bsz, length = qkv.shape[0], qkv.shape[1]
    nc = length // DN_CHUNK
    cur = lambda b, c: (b, c, 0)
    per_b = lambda b, c: (b, 0, 0)
    pad_a = lambda v: jnp.pad(v.reshape(1, DN_HEADS), ((0, 0), (DN_HEADS, LANES - 2 * DN_HEADS)))
    return pl.pallas_call(
        _dn_kernel,
        grid=(bsz // nseq, nc),
        in_specs=[pl.BlockSpec((nseq, DN_CHUNK, DN_CONV_CH), cur),
                  pl.BlockSpec((nseq, DN_CHUNK, DN_VW), cur),
                  pl.BlockSpec((nseq, DN_CHUNK, LANES), cur),
                  _const_spec((CONV_WIDTH, DN_CONV_CH)),
                  _const_spec((1, LANES)), _const_spec((1, LANES)), _const_spec((1, DN_DV))],
        out_specs=[pl.BlockSpec((nseq, DN_CHUNK, DN_VW), cur),
                   pl.BlockSpec((nseq, DN_QK, DN_DV), per_b)],
        out_shape=[jax.ShapeDtypeStruct((bsz, length, DN_VW), F32),
                   jax.ShapeDtypeStruct((bsz, DN_QK, DN_DV), F32)],
        scratch_shapes=[pltpu.VMEM((nseq, DN_CHUNK + SUBLANES, DN_CONV_CH), F32),
                        pltpu.VMEM((nseq, DN_QK, DN_DV), F32)],
        compiler_params=_cparams(("parallel", "arbitrary")),
        name="dn_mixer",
    )(qkv, z, ba, conv_w, pad_a(dt_bias), pad_a(a_log), norm_g.reshape(1, DN_DV))


X_SUB = D_MODEL // LANES


def _rows_to_slabs(x):
    parts = jnp.stack([x[:, s * LANES:(s + 1) * LANES] for s in range(X_SUB)], axis=0)
    return jnp.transpose(parts, (1, 0, 2))


def _slabs_to_rows(slab):
    parts = jnp.transpose(slab, (1, 0, 2))
    return jnp.concatenate([parts[s] for s in range(X_SUB)], axis=1)


def _resid_proj_kernel(x_ref, *refs, n_in):
    acc = x_ref[...]
    for a_ref, w_ref in zip(refs[:n_in], refs[n_in:2 * n_in]):
        acc = acc + jnp.dot(a_ref[...].astype(BF16), w_ref[...], preferred_element_type=F32)
    refs[2 * n_in][...] = acc


def resid_proj(x, acts, ws, tm):
    t = x.shape[0]
    n_in = len(acts)
    tok = lambda i: (i, 0)
    return pl.pallas_call(
        functools.partial(_resid_proj_kernel, n_in=n_in),
        grid=(t // tm,),
        in_specs=([pl.BlockSpec((tm, D_MODEL), tok)] + [pl.BlockSpec((tm, a.shape[1]), tok) for a in acts]
                  + [_const_spec(w.shape) for w in ws]),
        out_specs=pl.BlockSpec((tm, D_MODEL), tok),
        out_shape=jax.ShapeDtypeStruct((t, D_MODEL), F32),
        compiler_params=_cparams(("parallel",)),
        name="resid_proj",
    )(x, *acts, *ws)


def _split_bf16(v):
    hi = v.astype(BF16)
    return hi, (v - hi.astype(F32)).astype(BF16)


def _router_kernel(x_ref, g_ref, wr_hi_ref, wr_lo_ref, br_ref, xg_ref, gates_ref, grp_ref, rank_ref, cnt_ref,
                   carry_scr):
    tm = x_ref.shape[0]

    @pl.when(pl.program_id(0) == 0)
    def _():
        carry_scr[...] = jnp.zeros_like(carry_scr)

    x = x_ref[...]
    h_hi, h_lo = _split_bf16(_rms(x, g_ref[...]))
    w_hi, w_lo = wr_hi_ref[...], wr_lo_ref[...]
    logits = (jnp.dot(h_hi, w_hi, preferred_element_type=F32) + jnp.dot(h_lo, w_hi, preferred_element_type=F32)
              + jnp.dot(h_hi, w_lo, preferred_element_type=F32)) + br_ref[...]
    lt = logits.T
    sub = lax.broadcasted_iota(jnp.int32, (SUBLANES, tm), 0)
    neg = -jnp.inf
    glog = jnp.where(sub < N_GROUPS, lt[N_EXPERTS:N_EXPERTS + SUBLANES], neg)
    gmax = jnp.max(glog, axis=0, keepdims=True)
    g_p = 1.0 / jnp.sum(jnp.exp(glog - gmax), axis=0, keepdims=True)
    g_i = jnp.min(jnp.where(glog == gmax, sub, SUBLANES), axis=0, keepdims=True)
    sel = lt[0:EXPERTS_PER_GROUP]
    for g in range(1, N_GROUPS):
        sel = jnp.where(g_i == g, lt[g * EXPERTS_PER_GROUP:(g + 1) * EXPERTS_PER_GROUP], sel)
    m1 = jnp.max(sel, axis=0, keepdims=True)
    zsum = jnp.sum(jnp.exp(sel - m1), axis=0, keepdims=True)
    i1 = jnp.min(jnp.where(sel == m1, sub, SUBLANES), axis=0, keepdims=True)
    sel2 = jnp.where(sub == i1, neg, sel)
    m2 = jnp.max(sel2, axis=0, keepdims=True)
    i2 = jnp.min(jnp.where(sel2 == m2, sub, SUBLANES), axis=0, keepdims=True)
    p1 = 1.0 / zsum
    p2 = jnp.exp(m2 - m1) / zsum
    gate1 = g_p * p1 / (p1 + p2)
    gate2 = g_p * p2 / (p1 + p2)
    gates = jnp.where(sub == i1, gate1, 0.0) + jnp.where(sub == i2, gate2, 0.0)

    onehot = (sub == g_i).astype(F32)
    before = (lax.broadcasted_iota(jnp.int32, (tm, tm), 0) < lax.broadcasted_iota(jnp.int32, (tm, tm), 1))
    rank_in_tile = jnp.dot(onehot.astype(BF16), before.astype(BF16), preferred_element_type=F32)
    carry = carry_scr[...]
    rank = jnp.sum(onehot * (rank_in_tile + carry[:, 0:1]), axis=0, keepdims=True)
    grp_ref[...] = g_i
    rank_ref[...] = rank.astype(jnp.int32)
    new_carry = carry + jnp.sum(onehot, axis=1, keepdims=True)
    carry_scr[...] = new_carry
    cnt_ref[...] = new_carry

    gates_ref[...] = jnp.concatenate([gates, jnp.zeros((LANES - SUBLANES, tm), F32)], axis=0).T
    xg_ref[...] = _rows_to_slabs(x)


def moe_router(x, gain, w_router_pad, b_router_pad, tm):
    t = x.shape[0]
    w_hi = w_router_pad.astype(BF16)
    w_lo = (w_router_pad - w_hi.astype(F32)).astype(BF16)
    return pl.pallas_call(
        _router_kernel,
        grid=(t // tm,),
        in_specs=[pl.BlockSpec((tm, D_MODEL), lambda i: (i, 0)), _const_spec((1, D_MODEL)),
                  _const_spec((D_MODEL, LANES)), _const_spec((D_MODEL, LANES)), _const_spec((1, LANES))],
        out_specs=[pl.BlockSpec((tm, X_SUB, LANES), lambda i: (i, 0, 0)),
                   pl.BlockSpec((tm, LANES), lambda i: (i, 0)),
                   pl.BlockSpec((None, 1, tm), lambda i: (i, 0, 0)),
                   pl.BlockSpec((None, 1, tm), lambda i: (i, 0, 0)),
                   _const_spec((SUBLANES, LANES))],
        out_shape=[jax.ShapeDtypeStruct((t, X_SUB, LANES), F32),
                   jax.ShapeDtypeStruct((t, LANES), F32),
                   jax.ShapeDtypeStruct((t // tm, 1, tm), jnp.int32),
                   jax.ShapeDtypeStruct((t // tm, 1, tm), jnp.int32),
                   jax.ShapeDtypeStruct((SUBLANES, LANES), F32)],
        scratch_shapes=[pltpu.VMEM((SUBLANES, LANES), F32)],
        compiler_params=_cparams(("arbitrary",)),
        name="moe_router",
    )(x, gain.reshape(1, D_MODEL), w_hi, w_lo, b_router_pad)


def _issue_row_copies(n_rows, make_copy):
    def trip(i, carry):
        for u in range(ROW_DMA_UNROLL):
            make_copy(i * ROW_DMA_UNROLL + u).start(priority=u % DMA_PRIORITIES)
        return carry

    lax.fori_loop(0, n_rows // ROW_DMA_UNROLL, trip, 0)


def _dispatch_kernel(pos_ref, pad_lo_ref, pad_hi_ref, xg_ref, gates_ref, xs_hbm, gs_hbm, zero_scr,
                     sem, gsem, zsem, zgsem):
    tm = xg_ref.shape[0]
    gate_row = lambda ref, r: ref.at[pl.ds(r, 1)]
    _issue_row_copies(tm, lambda r: pltpu.make_async_copy(xg_ref.at[r], xs_hbm.at[pos_ref[0, r]], sem))
    _issue_row_copies(tm, lambda r: pltpu.make_async_copy(gate_row(gates_ref, r),
                                                          gate_row(gs_hbm, pos_ref[0, r]), gsem))
    pltpu.make_async_copy(xg_ref, xs_hbm.at[pl.ds(0, tm)], sem).wait()
    pltpu.make_async_copy(gates_ref, gs_hbm.at[pl.ds(0, tm)], gsem).wait()

    @pl.when(pl.program_id(0) == pl.num_programs(0) - 1)
    def _():
        zero_scr[...] = jnp.zeros_like(zero_scr)
        zero_gate = gate_row(zero_scr, 0)
        for k in range(pad_lo_ref.shape[0]):
            def fill(j, carry):
                pltpu.make_async_copy(zero_scr, xs_hbm.at[j], zsem).start()
                pltpu.make_async_copy(zero_gate, gate_row(gs_hbm, j), zgsem).start()
                return carry

            def drain(j, carry):
                pltpu.make_async_copy(zero_scr, xs_hbm.at[j], zsem).wait()
                pltpu.make_async_copy(zero_gate, gate_row(gs_hbm, j), zgsem).wait()
                return carry

            lax.fori_loop(pad_lo_ref[k], pad_hi_ref[k], fill, 0)
            lax.fori_loop(pad_lo_ref[k], pad_hi_ref[k], drain, 0)


def moe_dispatch(pos, pad_lo, pad_hi, xg, gates, tm, n_slots):
    t = xg.shape[0]
    dma = pltpu.SemaphoreType.DMA
    return pl.pallas_call(
        _dispatch_kernel,
        grid=(t // tm,),
        in_specs=[pl.BlockSpec((None, 1, tm), lambda i: (i, 0, 0), memory_space=pltpu.SMEM),
                  pl.BlockSpec(memory_space=pltpu.SMEM), pl.BlockSpec(memory_space=pltpu.SMEM),
                  pl.BlockSpec((tm, X_SUB, LANES), lambda i: (i, 0, 0)),
                  pl.BlockSpec((tm, LANES), lambda i: (i, 0))],
        out_specs=[pl.BlockSpec(memory_space=pl.ANY), pl.BlockSpec(memory_space=pl.ANY)],
        out_shape=[jax.ShapeDtypeStruct((n_slots, X_SUB, LANES), F32),
                   jax.ShapeDtypeStruct((n_slots, LANES), F32)],
        scratch_shapes=[pltpu.VMEM((X_SUB, LANES), F32), dma, dma, dma, dma],
        compiler_params=_cparams(("arbitrary",), has_side_effects=True),
        name="moe_dispatch",
    )(pos, pad_lo, pad_hi, xg, gates)


def _collect_kernel(pos_ref, os_hbm, out_ref, buf, sem):
    tm = out_ref.shape[0]
    _issue_row_copies(tm, lambda r: pltpu.make_async_copy(os_hbm.at[pos_ref[0, r]], buf.at[r], sem))
    pltpu.make_async_copy(os_hbm.at[pl.ds(0, tm)], buf, sem).wait()
    out_ref[...] = _slabs_to_rows(buf[...])


def moe_collect(pos, os_sorted, t, tm):
    return pl.pallas_call(
        _collect_kernel,
        grid=(t // tm,),
        in_specs=[pl.BlockSpec((None, 1, tm), lambda i: (i, 0, 0), memory_space=pltpu.SMEM),
                  pl.BlockSpec(memory_space=pl.ANY)],
        out_specs=pl.BlockSpec((tm, D_MODEL), lambda i: (i, 0)),
        out_shape=jax.ShapeDtypeStruct((t, D_MODEL), F32),
        scratch_shapes=[pltpu.VMEM((tm, X_SUB, LANES), F32), pltpu.SemaphoreType.DMA],
        compiler_params=_cparams(("arbitrary",), has_side_effects=True),
        name="moe_collect",
    )(pos, os_sorted)


def _group_experts_kernel(grp_ref, xs_ref, gates_ref, g_ref, wg_ref, wu_ref, wd_ref, o_ref):
    del grp_ref
    x = _slabs_to_rows(xs_ref[...])
    gates = gates_ref[...]
    h = _rms(x, g_ref[...]).astype(BF16)
    acc = x
    for e in range(EXPERTS_PER_GROUP):
        gate = jnp.dot(h, wg_ref[e], preferred_element_type=F32)
        up = jnp.dot(h, wu_ref[e], preferred_element_type=F32)
        act = (_silu(gate) * up).astype(BF16)
        acc = acc + gates[:, e:e + 1] * jnp.dot(act, wd_ref[e], preferred_element_type=F32)
    o_ref[...] = _rows_to_slabs(acc)


def moe_group_experts(tile_grp, xs, gs, gain, wg, wu, wd, tile):
    n_steps = xs.shape[0] // tile
    rows = lambda i, grp: (i, 0, 0)
    wts = lambda i, grp: (grp[i], 0, 0, 0)
    return pl.pallas_call(
        _group_experts_kernel,
        grid_spec=pltpu.PrefetchScalarGridSpec(
            num_scalar_prefetch=1,
            grid=(n_steps,),
            in_specs=[pl.BlockSpec((tile, X_SUB, LANES), rows),
                      pl.BlockSpec((tile, LANES), lambda i, grp: (i, 0)),
                      pl.BlockSpec((1, D_MODEL), lambda i, grp: (0, 0)),
                      pl.BlockSpec((None, EXPERTS_PER_GROUP, D_MODEL, EXPERT_FF), wts),
                      pl.BlockSpec((None, EXPERTS_PER_GROUP, D_MODEL, EXPERT_FF), wts),
                      pl.BlockSpec((None, EXPERTS_PER_GROUP, EXPERT_FF, D_MODEL), wts)],
            out_specs=pl.BlockSpec((tile, X_SUB, LANES), rows)),
        out_shape=jax.ShapeDtypeStruct((xs.shape[0], X_SUB, LANES), F32),
        compiler_params=_cparams(("arbitrary",)),
        name="moe_group_experts",
    )(tile_grp, xs, gs, gain.reshape(1, D_MODEL), wg, wu, wd)


def mix_out_and_moe(x, acts, ws, gain, w_route, b_route, wg, wu, wd):
    t = x.shape[0]
    tile = MOE_TILE if t % MOE_TILE == 0 else t
    n_tiles = t // tile + N_GROUPS
    x = resid_proj(x, acts, ws, tile)
    xg, gates, grp, rank, cnt = moe_router(x, gain, w_route, b_route, tile)
    counts = cnt[:N_GROUPS, 0].astype(jnp.int32)
    tile_ends = jnp.cumsum((counts + tile - 1) // tile)
    offs = (tile_ends - (counts + tile - 1) // tile) * tile
    pos = rank + sum(jnp.where(grp == g, offs[g], 0) for g in range(N_GROUPS))
    pad_lo = jnp.concatenate([offs + counts, tile_ends[-1:] * tile])
    pad_hi = jnp.concatenate([tile_ends * tile, jnp.full((1,), n_tiles * tile, jnp.int32)])
    tile_grp = jnp.minimum(jnp.sum((jnp.arange(n_tiles, dtype=jnp.int32)[:, None] >= tile_ends[None, :])
                                   .astype(jnp.int32), axis=1), N_GROUPS - 1)
    xs, gs = moe_dispatch(pos, pad_lo, pad_hi, xg, gates, tile, n_tiles * tile)
    os_sorted = moe_group_experts(tile_grp, xs, gs, gain, wg, wu, wd, tile)
    return moe_collect(pos, os_sorted, t, tile)


BATCH_NT = (((2,), (2,)), ((0,), (0,)))
BATCH_NN = (((2,), (1,)), ((0,), (0,)))


def _token_major(parts):
    return jnp.transpose(jnp.stack(parts, axis=0), (1, 0, 2))


def _pad_rows(parts, tb, width):
    return parts + [jnp.zeros((tb, width), F32)] * (SUBLANES - len(parts))


def _spread(v, first, n, width):
    tb = v.shape[0]
    return jnp.concatenate([jnp.broadcast_to(v[:, first + i:first + i + 1], (tb, width)) for i in range(n)], axis=1)


def _columns(rows):
    tb, m = rows.shape
    return jnp.concatenate([rows, jnp.zeros((LANES - tb, m), F32)], axis=0).T


def _one_step_conv(x_ref, b0_ref, b1_ref, b2_ref, w_ref):
    return (w_ref[0:1, :] * b0_ref[...] + w_ref[1:2, :] * b1_ref[...] + w_ref[2:3, :] * b2_ref[...]
            + w_ref[3:4, :] * x_ref[...])


def _swa_decode_kernel(sink_ref, q_ref, k_ref, v_ref, kc_ref, vc_ref, qn_ref, kn_ref, o_ref, kout_ref):
    tb = q_ref.shape[0]
    lo = lax.broadcasted_iota(jnp.int32, (tb, LANES), 1) < HEAD_DIM

    def pair_rms(t, gain):
        sq = t * t
        s_lo = jnp.sum(jnp.where(lo, sq, 0.0), axis=-1, keepdims=True)
        s_hi = jnp.sum(jnp.where(lo, 0.0, sq), axis=-1, keepdims=True)
        return t * lax.rsqrt(jnp.where(lo, s_lo, s_hi) * (1.0 / HEAD_DIM) + EPS) * gain

    qn2 = jnp.concatenate([qn_ref[...], qn_ref[...]], axis=1)
    kn2 = jnp.concatenate([kn_ref[...], kn_ref[...]], axis=1)
    k_new = pair_rms(k_ref[...], kn2)
    kout_ref[...] = k_new
    v_new = v_ref[...]
    rows = []
    for h in range(SWA_HEADS):
        t = pair_rms(q_ref[:, (h // 2) * LANES:(h // 2 + 1) * LANES], qn2) * (HEAD_DIM ** -0.5)
        j = h // SWA_GROUP
        if h % 2 != j:
            t = pltpu.roll(t, HEAD_DIM, axis=1)
        rows.append(jnp.where(lo if j == 0 else jnp.logical_not(lo), t, 0.0))
    q8 = _token_major(rows)
    s = lax.dot_general(q8.astype(BF16), kc_ref[...].astype(BF16), BATCH_NT, preferred_element_type=F32)
    s_new = jnp.sum(q8 * k_new[:, None, :], axis=-1, keepdims=True)
    sub = lax.broadcasted_iota(jnp.int32, (1, SWA_HEADS, 1), 1)
    sink = jnp.zeros((1, SWA_HEADS, 1), F32)
    for h in range(SWA_HEADS):
        sink = jnp.where(sub == h, sink_ref[h], sink)
    m = jnp.maximum(jnp.maximum(jnp.max(s, axis=-1, keepdims=True), s_new), sink)
    p = jnp.exp(s - m)
    p_new = jnp.exp(s_new - m)
    den = jnp.sum(p, axis=-1, keepdims=True) + p_new + jnp.exp(sink - m)
    o8 = lax.dot_general((p / den).astype(BF16), vc_ref[...].astype(BF16), BATCH_NN,
                         preferred_element_type=F32) + (p_new / den) * v_new[:, None, :]
    o_h = jnp.transpose(o8, (1, 0, 2))
    tiles = []
    for t in range(SWA_HEADS // 2):
        halves = []
        for h in (2 * t, 2 * t + 1):
            piece = o_h[h]
            if h % 2 != h // SWA_GROUP:
                piece = pltpu.roll(piece, HEAD_DIM, axis=1)
            halves.append(piece)
        tiles.append(jnp.where(lo, halves[0], halves[1]))
    o_ref[...] = jnp.concatenate(tiles, axis=1)


def swa_decode(q, k, v, k_cache, v_cache, q_norm, k_norm, sinks):
    bsz = q.shape[0]
    tok = lambda i: (i, 0)
    tok3 = lambda i: (i, 0, 0)
    return pl.pallas_call(
        _swa_decode_kernel,
        grid=(bsz // DEC_TB,),
        in_specs=[pl.BlockSpec(memory_space=pltpu.SMEM),
                  pl.BlockSpec((DEC_TB, SWA_Q), tok), pl.BlockSpec((DEC_TB, SWA_KV), tok),
                  pl.BlockSpec((DEC_TB, SWA_KV), tok),
                  pl.BlockSpec((DEC_TB, WINDOW, SWA_KV), tok3), pl.BlockSpec((DEC_TB, WINDOW, SWA_KV), tok3),
                  _const_spec((1, HEAD_DIM)), _const_spec((1, HEAD_DIM))],
        out_specs=[pl.BlockSpec((DEC_TB, SWA_Q), tok), pl.BlockSpec((DEC_TB, SWA_KV), tok)],
        out_shape=[jax.ShapeDtypeStruct((bsz, SWA_Q), F32), jax.ShapeDtypeStruct((bsz, SWA_KV), F32)],
        compiler_params=_cparams(("parallel",)),
        name="swa_decode",
    )(sinks, q, k, v, k_cache, v_cache, q_norm.reshape(1, HEAD_DIM), k_norm.reshape(1, HEAD_DIM))


def _ssd_decode_kernel(x_ref, b0_ref, b1_ref, b2_ref, z_ref, dt_ref, cw_ref, cb_ref, dtb_ref, alog_ref, dd_ref,
                       nrm_ref, h0_ref, y_ref, hout_ref):
    tb = x_ref.shape[0]
    act = _silu(_one_step_conv(x_ref, b0_ref, b1_ref, b2_ref, cw_ref) + cb_ref[...])
    xs = act[:, :SSM_INNER]
    bm = act[:, SSM_INNER:SSM_INNER + SSM_GROUPS * SSM_STATE]
    cm = act[:, SSM_INNER + SSM_GROUPS * SSM_STATE:]
    dt = _softplus(dt_ref[...] + dtb_ref[...])
    dec = jnp.exp(dt * (-jnp.exp(alog_ref[...])))
    dt_x = _spread(dt, 0, SSM_HEADS, SSM_HEAD_DIM)
    dec_x = _spread(dec, 0, SSM_HEADS, SSM_HEAD_DIM)
    gw = SSM_INNER // SSM_GROUPS
    grp = lambda t, g: t[:, g * SSM_STATE:(g + 1) * SSM_STATE]
    cb_x = jnp.concatenate([jnp.broadcast_to(jnp.sum(grp(cm, g) * grp(bm, g), axis=-1, keepdims=True), (tb, gw))
                            for g in range(SSM_GROUPS)], axis=1)
    h0 = h0_ref[...]
    c8 = _token_major(_pad_rows([grp(cm, g) for g in range(SSM_GROUPS)], tb, SSM_STATE))
    ys = jnp.transpose(lax.dot_general(c8.astype(BF16), h0.astype(BF16), BATCH_NT, preferred_element_type=F32),
                       (1, 0, 2))
    y_state = jnp.concatenate([ys[g][:, g * gw:(g + 1) * gw] for g in range(SSM_GROUPS)], axis=1)
    y = cb_x * dt_x * xs + y_state * dec_x + dd_ref[...] * xs
    y = y * _silu(z_ref[...])
    nrm = nrm_ref[...]
    y_ref[...] = jnp.concatenate(
        [_rms(y[:, g * gw:(g + 1) * gw], nrm[:, g * gw:(g + 1) * gw]) for g in range(SSM_GROUPS)], axis=1)
    u_cols = _columns(dt_x * xs)
    hpg = SSM_HEADS // SSM_GROUPS
    for b in range(tb):
        for g in range(SSM_GROUPS):
            rows = slice(g * gw, (g + 1) * gw)
            decay = jnp.concatenate([jnp.broadcast_to(dec[b:b + 1, h:h + 1], (SSM_HEAD_DIM, SSM_STATE))
                                     for h in range(g * hpg, (g + 1) * hpg)], axis=0)
            hout_ref[b, rows, :] = h0[b, rows, :] * decay + u_cols[rows, b:b + 1] * grp(bm, g)[b:b + 1, :]


def ssd_decode(xbc, bufs, z, dt, conv_w, conv_b, dt_bias, a_log, d_skip, norm_g, h0):
    bsz = xbc.shape[0]
    tok = lambda i: (i, 0)
    tok3 = lambda i: (i, 0, 0)
    pad8 = lambda v: jnp.pad(v.reshape(1, SSM_HEADS), ((0, 0), (0, LANES - SSM_HEADS)))
    row = lambda w: pl.BlockSpec((DEC_TB, w), tok)
    return pl.pallas_call(
        _ssd_decode_kernel,
        grid=(bsz // DEC_TB,),
        in_specs=[row(SSM_CONV_CH)] * 4 + [row(SSM_INNER), row(LANES),
                  _const_spec((CONV_WIDTH, SSM_CONV_CH)), _const_spec((1, SSM_CONV_CH)),
                  _const_spec((1, LANES)), _const_spec((1, LANES)), _const_spec((1, SSM_INNER)),
                  _const_spec((1, SSM_INNER)),
                  pl.BlockSpec((DEC_TB, SSM_INNER, SSM_STATE), tok3)],
        out_specs=[row(SSM_INNER), pl.BlockSpec((DEC_TB, SSM_INNER, SSM_STATE), tok3)],
        out_shape=[jax.ShapeDtypeStruct((bsz, SSM_INNER), F32),
                   jax.ShapeDtypeStruct((bsz, SSM_INNER, SSM_STATE), F32)],
        compiler_params=_cparams(("parallel",)),
        name="ssd_decode",
    )(xbc, *bufs, z, dt, conv_w, conv_b.reshape(1, SSM_CONV_CH), pad8(dt_bias), pad8(a_log),
      jnp.repeat(d_skip, SSM_HEAD_DIM).reshape(1, SSM_INNER), norm_g.reshape(1, SSM_INNER), h0)


def _dn_decode_kernel(x_ref, b0_ref, b1_ref, b2_ref, z_ref, ba_ref, cw_ref, dtb_ref, alog_ref, nrm_ref, s0_ref,
                      o_ref, sout_ref):
    tb = x_ref.shape[0]
    act = _silu(_one_step_conv(x_ref, b0_ref, b1_ref, b2_ref, cw_ref))
    ba = ba_ref[...]
    beta = _sigmoid(ba)
    eg = jnp.exp(-jnp.exp(alog_ref[...]) * _softplus(ba + dtb_ref[...]))
    z = z_ref[...]
    nrm = nrm_ref[...]
    outs = []
    for h in range(DN_HEADS):
        q_h = act[:, h * DN_DK:(h + 1) * DN_DK]
        k_h = act[:, DN_QK + h * DN_DK:DN_QK + (h + 1) * DN_DK]
        v_h = act[:, 2 * DN_QK + h * DN_DV:2 * DN_QK + (h + 1) * DN_DV]
        q_h = q_h * lax.rsqrt(jnp.sum(q_h * q_h, axis=-1, keepdims=True) + EPS) * (DN_DK ** -0.5)
        k_h = k_h * lax.rsqrt(jnp.sum(k_h * k_h, axis=-1, keepdims=True) + EPS)
        beta_c = beta[:, h:h + 1]
        eg_c = eg[:, DN_HEADS + h:DN_HEADS + h + 1]
        s_h = s0_ref[:, h * DN_DK:(h + 1) * DN_DK, :]
        kq8 = _token_major(_pad_rows([k_h, q_h], tb, DN_DK))
        r = jnp.transpose(lax.dot_general(kq8.astype(BF16), s_h.astype(BF16), BATCH_NN,
                                          preferred_element_type=F32), (1, 0, 2))
        v_new = beta_c * v_h - (beta_c * eg_c) * r[0]
        o_h = eg_c * r[1] + jnp.sum(q_h * k_h, axis=-1, keepdims=True) * v_new
        outs.append(_rms(o_h, nrm) * _silu(z[:, h * DN_DV:(h + 1) * DN_DV]))
        k_cols = _columns(k_h)
        for b in range(tb):
            sout_ref[b, h * DN_DK:(h + 1) * DN_DK, :] = (
                s_h[b] * jnp.broadcast_to(eg_c[b:b + 1, :], (DN_DK, DN_DV)) + k_cols[:, b:b + 1] * v_new[b:b + 1, :])
    o_ref[...] = jnp.concatenate(outs, axis=1)


def dn_decode(qkv, bufs, z, ba, conv_w, dt_bias, a_log, norm_g, s0):
    bsz = qkv.shape[0]
    tok = lambda i: (i, 0)
    tok3 = lambda i: (i, 0, 0)
    pad_a = lambda v: jnp.pad(v.reshape(1, DN_HEADS), ((0, 0), (DN_HEADS, LANES - 2 * DN_HEADS)))
    row = lambda w: pl.BlockSpec((DEC_TB, w), tok)
    return pl.pallas_call(
        _dn_decode_kernel,
        grid=(bsz // DEC_TB,),
        in_specs=[row(DN_CONV_CH)] * 4 + [row(DN_VW), row(LANES),
                  _const_spec((CONV_WIDTH, DN_CONV_CH)),
                  _const_spec((1, LANES)), _const_spec((1, LANES)), _const_spec((1, DN_DV)),
                  pl.BlockSpec((DEC_TB, DN_QK, DN_DV), tok3)],
        out_specs=[row(DN_VW), pl.BlockSpec((DEC_TB, DN_QK, DN_DV), tok3)],
        out_shape=[jax.ShapeDtypeStruct((bsz, DN_VW), F32), jax.ShapeDtypeStruct((bsz, DN_QK, DN_DV), F32)],
        compiler_params=_cparams(("parallel",)),
        name="dn_decode",
    )(qkv, *bufs, z, ba, conv_w, pad_a(dt_bias), pad_a(a_log), norm_g.reshape(1, DN_DV), s0)


def _pad_cols(w, n_pad):
    return jnp.pad(w, ((0, 0), (0, n_pad - w.shape[1])))


EVEN_SEGS = ((0, 512), (512, 640), (640, 768), (768, 1280), (1280, 2304), (2304, 2432))
ODD_SEGS = ((0, 3072), (3072, 4096), (4096, 4224))


def _trunk(x_seq, p, states):
    prompt = states is None
    bsz, length = x_seq.shape[0], x_seq.shape[1]
    t = bsz * length
    tm = 512 if t % 512 == 0 else t
    x = x_seq.reshape(t, D_MODEL)
    seq = lambda u: u.reshape(bsz, length, u.shape[-1])

    q, k, v, z, xbc, dt = prenorm_proj(x, p['ln_mix'][0], p['w_in_even'], EVEN_SEGS, tm)
    if prompt:
        v3, xbc3 = seq(v), seq(xbc)
        att, k_normed = swa_attention(seq(q), seq(k), v3, p['q_norm'], p['k_norm'], p['attn_sinks'])
        new_k = k_normed[:, -WINDOW:].reshape(bsz, WINDOW, SWA_KV_HEADS, HEAD_DIM)
        new_v = v3[:, -WINDOW:].reshape(bsz, WINDOW, SWA_KV_HEADS, HEAD_DIM)
        new_ssm_conv = xbc3[:, -(CONV_WIDTH - 1):]
        y_ssm, new_h = ssd_mixer(xbc3, seq(z), seq(dt), p['ssm_conv_w'], p['ssm_conv_b'], p['ssm_dt_bias'],
                                 p['ssm_A_log'], p['ssm_D'], p['ssm_norm'])
    else:
        k_win, v_win, ssm_h, ssm_conv, dn_s, dn_conv = states
        kb = k_win.reshape(bsz, WINDOW, SWA_KV)
        vb = v_win.reshape(bsz, WINDOW, SWA_KV)
        att, k_normed = swa_decode(q, k, v, kb, vb, p['q_norm'], p['k_norm'], p['attn_sinks'])
        new_k = jnp.concatenate([kb[:, 1:], k_normed[:, None]], axis=1).reshape(bsz, WINDOW, SWA_KV_HEADS, HEAD_DIM)
        new_v = jnp.concatenate([vb[:, 1:], v[:, None]], axis=1).reshape(bsz, WINDOW, SWA_KV_HEADS, HEAD_DIM)
        new_ssm_conv = jnp.concatenate([ssm_conv[:, 1:], xbc[:, None, :]], axis=1)
        y_ssm, new_h = ssd_decode(xbc, [ssm_conv[:, i] for i in range(CONV_WIDTH - 1)], z, dt, p['ssm_conv_w'],
                                  p['ssm_conv_b'], p['ssm_dt_bias'], p['ssm_A_log'], p['ssm_D'], p['ssm_norm'],
                                  ssm_h.reshape(bsz, SSM_INNER, SSM_STATE))
    x = mix_out_and_moe(x, [att.reshape(t, SWA_Q), y_ssm.reshape(t, SSM_INNER)],
                        [p['w_out_even'][:SWA_Q], p['w_out_even'][SWA_Q:]],
                        p['ln_ffn'][0], p['w_route'][0], p['b_route'][0],
                        p['moe_w_gate'][0], p['moe_w_up'][0], p['moe_w_down'][0])

    qkv, zz, ba = prenorm_proj(x, p['ln_mix'][1], p['w_in_odd'], ODD_SEGS, tm)
    if prompt:
        qkv3 = seq(qkv)
        new_dn_conv = qkv3[:, -(CONV_WIDTH - 1):]
        o_dn, new_s = dn_mixer(qkv3, seq(zz), seq(ba), p['dn_conv_w'], p['dn_dt_bias'], p['dn_A_log'],
                               p['dn_norm'], DN_SEQS_PER_STEP)
    else:
        new_dn_conv = jnp.concatenate([dn_conv[:, 1:], qkv[:, None, :]], axis=1)
        o_dn, new_s = dn_decode(qkv, [dn_conv[:, i] for i in range(CONV_WIDTH - 1)], zz, ba, p['dn_conv_w'],
                                p['dn_dt_bias'], p['dn_A_log'], p['dn_norm'], dn_s.reshape(bsz, DN_QK, DN_DV))
    x = mix_out_and_moe(x, [o_dn.reshape(t, DN_VW)], [p['w_out_odd']],
                        p['ln_ffn'][1], p['w_route'][1], p['b_route'][1],
                        p['moe_w_gate'][1], p['moe_w_up'][1], p['moe_w_down'][1])

    return (x.reshape(bsz, length, D_MODEL), new_k[None], new_v[None],
            new_h.reshape(1, bsz, SSM_HEADS, SSM_HEAD_DIM, SSM_STATE), new_ssm_conv[None],
            new_s.reshape(1, bsz, DN_HEADS, DN_DK, DN_DV), new_dn_conv[None])


def kernel(x_prompt, x_sample, cache_k_win, cache_v_win, state_ssm, state_ssm_conv, state_dn, state_dn_conv,
           ln_mix, ln_ffn, w_in_even, q_norm, k_norm, attn_sinks, ssm_conv_w, ssm_conv_b, ssm_dt_bias,
           ssm_A_log, ssm_D, ssm_norm, w_out_even, w_in_odd, dn_conv_w, dn_dt_bias, dn_A_log, dn_norm,
           w_out_odd, moe_w_group, moe_b_group, moe_w_router, moe_b_router, moe_w_gate, moe_w_up, moe_w_down):
    w_route = _pad_cols(jnp.concatenate([moe_w_router, moe_w_group], axis=-1).reshape(-1, N_EXPERTS + N_GROUPS),
                        LANES).reshape(2, D_MODEL, LANES)
    b_route = _pad_cols(jnp.concatenate([moe_b_router, moe_b_group], axis=-1), LANES).reshape(2, 1, LANES)
    p = {
        'ln_mix': ln_mix, 'ln_ffn': ln_ffn,
        'w_in_even': _pad_cols(w_in_even[0], EVEN_SEGS[-1][1]).astype(BF16),
        'q_norm': q_norm[0], 'k_norm': k_norm[0], 'attn_sinks': attn_sinks[0],
        'ssm_conv_w': ssm_conv_w[0], 'ssm_conv_b': ssm_conv_b[0], 'ssm_dt_bias': ssm_dt_bias[0],
        'ssm_A_log': ssm_A_log[0], 'ssm_D': ssm_D[0], 'ssm_norm': ssm_norm[0],
        'w_out_even': w_out_even[0].astype(BF16),
        'w_in_odd': _pad_cols(w_in_odd[0], ODD_SEGS[-1][1]).astype(BF16),
        'dn_conv_w': dn_conv_w[0], 'dn_dt_bias': dn_dt_bias[0], 'dn_A_log': dn_A_log[0], 'dn_norm': dn_norm[0],
        'w_out_odd': w_out_odd[0].astype(BF16),
        'w_route': w_route, 'b_route': b_route,
        'moe_w_gate': moe_w_gate.astype(BF16).reshape(2, N_GROUPS, EXPERTS_PER_GROUP, D_MODEL, EXPERT_FF),
        'moe_w_up': moe_w_up.astype(BF16).reshape(2, N_GROUPS, EXPERTS_PER_GROUP, D_MODEL, EXPERT_FF),
        'moe_w_down': moe_w_down.astype(BF16).reshape(2, N_GROUPS, EXPERTS_PER_GROUP, EXPERT_FF, D_MODEL),
    }
    y_p, kp, vp, sp, scp, dnp, dncp = _trunk(x_prompt, p, None)
    sample_states = (cache_k_win[0], cache_v_win[0], state_ssm[0], state_ssm_conv[0], state_dn[0],
                     state_dn_conv[0])
    y_s, ks, vs, ss, scs, dns, dncs = _trunk(x_sample, p, sample_states)
    return (y_p, y_s, kp, ks, vp, vs, sp, ss, scp, scs, dnp, dns, dncp, dncs)
```

```python
import functools

import jax
import jax.numpy as jnp
from jax import lax
from jax.experimental import pallas as pl
from jax.experimental.pallas import tpu as pltpu

F32 = jnp.float32
BF16 = jnp.bfloat16
EPS = 1e-6

D_MODEL = 1024
SWA_HEADS = 8
SWA_KV_HEADS = 2
SWA_GROUP = SWA_HEADS // SWA_KV_HEADS
HEAD_DIM = 64
WINDOW = 128
SWA_Q = SWA_HEADS * HEAD_DIM
SWA_KV = SWA_KV_HEADS * HEAD_DIM
SSM_HEADS = 8
SSM_HEAD_DIM = 64
SSM_GROUPS = 2
SSM_STATE = 128
SSM_INNER = SSM_HEADS * SSM_HEAD_DIM
SSM_CHUNK = 128
SSM_CONV_CH = SSM_INNER + 2 * SSM_GROUPS * SSM_STATE
CONV_WIDTH = 4
DN_HEADS = 8
DN_DK = 128
DN_DV = 128
DN_CHUNK = 64
DN_QK = DN_HEADS * DN_DK
DN_VW = DN_HEADS * DN_DV
DN_CONV_CH = 2 * DN_QK + DN_VW
N_GROUPS = 4
EXPERTS_PER_GROUP = 8
N_EXPERTS = N_GROUPS * EXPERTS_PER_GROUP
EXPERT_FF = 256

LANES = 128
SUBLANES = 8
VMEM_LIMIT = 56 * 1024 * 1024
DMA_PRIORITIES = 2

DN_SEQS_PER_STEP = 2
MOE_TILE = 512
MOE_TOKEN_TILE = 1024
DEC_TB = 8
ROW_DMA_UNROLL = 8

NT_DIMS = (((1,), (1,)), ((), ()))
TN_DIMS = (((0,), (0,)), ((), ()))


def _cparams(sem, **kw):
    return pltpu.CompilerParams(dimension_semantics=sem, vmem_limit_bytes=VMEM_LIMIT, **kw)


def _const_spec(shape):
    nd = len(shape)
    return pl.BlockSpec(shape, lambda *_: (0,) * nd)


def _sigmoid(x):
    return 1.0 / (1.0 + jnp.exp(-x))


def _silu(x):
    return x * _sigmoid(x)


def _softplus(x):
    return jnp.maximum(x, 0.0) + jnp.log(1.0 + jnp.exp(-jnp.abs(x)))


def _rms(x, gain):
    return x * lax.rsqrt(jnp.mean(x * x, axis=-1, keepdims=True) + EPS) * gain


def _prenorm_proj_kernel(x_ref, g_ref, w_ref, *out_refs, segs):
    h = _rms(x_ref[...], g_ref[...]).astype(BF16)
    for o_ref, (a, b) in zip(out_refs, segs):
        o_ref[...] = jnp.dot(h, w_ref[:, a:b], preferred_element_type=F32)


def prenorm_proj(x, gain, w_bf16, segs, tm):
    t = x.shape[0]
    n_pad = w_bf16.shape[1]
    return pl.pallas_call(
        functools.partial(_prenorm_proj_kernel, segs=segs),
        grid=(t // tm,),
        in_specs=[pl.BlockSpec((tm, D_MODEL), lambda i: (i, 0)),
                  _const_spec((1, D_MODEL)),
                  _const_spec((D_MODEL, n_pad))],
        out_specs=[pl.BlockSpec((tm, b - a), lambda i: (i, 0)) for a, b in segs],
        out_shape=[jax.ShapeDtypeStruct((t, b - a), F32) for a, b in segs],
        compiler_params=_cparams(("parallel",)),
        name="prenorm_proj",
    )(x, gain.reshape(1, D_MODEL), w_bf16)


def _swa_kernel(sink_ref, q_ref, kc_ref, kp_ref, vc_ref, vp_ref, qn_ref, kn_ref, o_ref, kout_ref):
    n = pl.program_id(1)
    blk = q_ref.shape[0]
    q = q_ref[...]
    kc, kp, vc, vp = kc_ref[...], kp_ref[...], vc_ref[...], vp_ref[...]
    qn, kn = qn_ref[...], kn_ref[...]
    row = lax.broadcasted_iota(jnp.int32, (blk, 2 * blk), 0)
    col = lax.broadcasted_iota(jnp.int32, (blk, 2 * blk), 1)
    rel = row + blk - col
    mask = (rel >= 0) & (rel <= WINDOW) & ((n > 0) | (col >= blk))
    outs, kouts = [], []
    for j in range(SWA_KV_HEADS):
        sl = slice(j * HEAD_DIM, (j + 1) * HEAD_DIM)
        kcj = _rms(kc[:, sl], kn)
        kpj = _rms(kp[:, sl], kn)
        kouts.append(kcj)
        kcat = jnp.concatenate([kpj, kcj], axis=0).astype(BF16)
        vcat = jnp.concatenate([vp[:, sl], vc[:, sl]], axis=0).astype(BF16)
        for g in range(SWA_GROUP):
            h = j * SWA_GROUP + g
            qh = _rms(q[:, h * HEAD_DIM:(h + 1) * HEAD_DIM], qn) * (HEAD_DIM ** -0.5)
            s = lax.dot_general(qh.astype(BF16), kcat, NT_DIMS, preferred_element_type=F32)
            s = jnp.where(mask, s, -jnp.inf)
            sink = sink_ref[h]
            m = jnp.maximum(jnp.max(s, axis=-1, keepdims=True), sink)
            p = jnp.exp(s - m)
            p = p / (jnp.sum(p, axis=-1, keepdims=True) + jnp.exp(sink - m))
            outs.append(jnp.dot(p.astype(BF16), vcat, preferred_element_type=F32))
    o_ref[...] = jnp.concatenate(outs, axis=1)
    kout_ref[...] = jnp.concatenate(kouts, axis=1)


def swa_attention(q, k, v, q_norm, k_norm, sinks):
    bsz, length = q.shape[0], q.shape[1]
    nb = length // WINDOW
    prev = lambda b, n: (b, jnp.maximum(n - 1, 0), 0)
    cur = lambda b, n: (b, n, 0)
    kv_blk = (None, WINDOW, SWA_KV)
    return pl.pallas_call(
        _swa_kernel,
        grid=(bsz, nb),
        in_specs=[pl.BlockSpec(memory_space=pltpu.SMEM),
                  pl.BlockSpec((None, WINDOW, SWA_Q), cur),
                  pl.BlockSpec(kv_blk, cur), pl.BlockSpec(kv_blk, prev),
                  pl.BlockSpec(kv_blk, cur), pl.BlockSpec(kv_blk, prev),
                  _const_spec((1, HEAD_DIM)), _const_spec((1, HEAD_DIM))],
        out_specs=[pl.BlockSpec((None, WINDOW, SWA_Q), cur), pl.BlockSpec(kv_blk, cur)],
        out_shape=[jax.ShapeDtypeStruct((bsz, length, SWA_Q), F32),
                   jax.ShapeDtypeStruct((bsz, length, SWA_KV), F32)],
        compiler_params=_cparams(("parallel", "arbitrary")),
        name="swa_attention",
    )(sinks, q, k, k, v, v, q_norm.reshape(1, HEAD_DIM), k_norm.reshape(1, HEAD_DIM))


def _chunk_conv(x_ref, xx_scr, w_ref, rows):
    xx_scr[SUBLANES:SUBLANES + rows, :] = x_ref[...]
    acc = None
    for tap in range(CONV_WIDTH):
        off = SUBLANES - (CONV_WIDTH - 1) + tap
        term = w_ref[tap:tap + 1, :] * xx_scr[off:off + rows, :]
        acc = term if acc is None else acc + term
    return acc


def _carry_conv_tail(xx_scr, rows):
    xx_scr[0:SUBLANES, :] = xx_scr[rows:rows + SUBLANES, :]


def _ssd_kernel(xbc_ref, z_ref, dt_ref, cw_ref, cb_ref, dtb_ref, alog_ref, dd_ref, nrm_ref,
                y_ref, hout_ref, xx_scr, h_scr):
    q_len = xbc_ref.shape[0]

    @pl.when(pl.program_id(1) == 0)
    def _():
        xx_scr[0:SUBLANES, :] = jnp.zeros((SUBLANES, xx_scr.shape[1]), F32)
        h_scr[...] = jnp.zeros_like(h_scr)

    act = _silu(_chunk_conv(xbc_ref, xx_scr, cw_ref, q_len) + cb_ref[...])
    _carry_conv_tail(xx_scr, q_len)
    xs = act[:, :SSM_INNER]
    bm = act[:, SSM_INNER:SSM_INNER + SSM_GROUPS * SSM_STATE]
    cm = act[:, SSM_INNER + SSM_GROUPS * SSM_STATE:]

    row = lax.broadcasted_iota(jnp.int32, (q_len, q_len), 0)
    col = lax.broadcasted_iota(jnp.int32, (q_len, q_len), 1)
    causal = row >= col
    dt = _softplus(dt_ref[...] + dtb_ref[...])
    da = dt * (-jnp.exp(alog_ref[...]))
    cum = jnp.dot(causal.astype(F32), da, preferred_element_type=F32, precision=lax.Precision.HIGHEST)
    cum_t = cum.T
    dt_t = dt.T
    e_cum = jnp.exp(cum)
    hpg = SSM_HEADS // SSM_GROUPS
    gw = hpg * SSM_HEAD_DIM
    ys = []
    for g in range(SSM_GROUPS):
        bm_g = bm[:, g * SSM_STATE:(g + 1) * SSM_STATE].astype(BF16)
        cm_g = cm[:, g * SSM_STATE:(g + 1) * SSM_STATE].astype(BF16)
        cb = lax.dot_general(cm_g, bm_g, NT_DIMS, preferred_element_type=F32)
        h_g = h_scr[g * gw:(g + 1) * gw, :]
        y_state = lax.dot_general(cm_g, h_g.astype(BF16), NT_DIMS, preferred_element_type=F32)
        xt_parts, dec_parts = [], []
        for hh in range(hpg):
            h = g * hpg + hh
            x_h = xs[:, h * SSM_HEAD_DIM:(h + 1) * SSM_HEAD_DIM]
            cum_c = cum[:, h:h + 1]
            seg = jnp.exp(jnp.where(causal, cum_c - cum_t[h:h + 1, :], -jnp.inf))
            wgt = cb * seg * dt_t[h:h + 1, :]
            y = jnp.dot(wgt.astype(BF16), x_h.astype(BF16), preferred_element_type=F32)
            y = y + y_state[:, hh * SSM_HEAD_DIM:(hh + 1) * SSM_HEAD_DIM] * e_cum[:, h:h + 1]
            ys.append(y + dd_ref[0, h] * x_h)
            c_last = cum[q_len - 1:q_len, h:h + 1]
            xt_parts.append(x_h * (jnp.exp(c_last - cum_c) * dt[:, h:h + 1]))
            dec_parts.append(jnp.broadcast_to(jnp.exp(c_last), (SSM_HEAD_DIM, SSM_STATE)))
        xt = jnp.concatenate(xt_parts, axis=1).astype(BF16)
        upd = lax.dot_general(xt, bm_g, TN_DIMS, preferred_element_type=F32)
        h_scr[g * gw:(g + 1) * gw, :] = h_g * jnp.concatenate(dec_parts, axis=0) + upd
    y_all = jnp.concatenate(ys, axis=1) * _silu(z_ref[...])
    nrm = nrm_ref[...]
    y_ref[...] = jnp.concatenate(
        [_rms(y_all[:, g * gw:(g + 1) * gw], nrm[:, g * gw:(g + 1) * gw]) for g in range(SSM_GROUPS)], axis=1)
    hout_ref[...] = h_scr[...]


def ssd_mixer(xbc, z, dt, conv_w, conv_b, dt_bias, a_log, d_skip, norm_g):
    bsz, length = xbc.shape[0], xbc.shape[1]
    nc = length // SSM_CHUNK
    cur = lambda b, c: (b, c, 0)
    per_b = lambda b, c: (b, 0, 0)
    pad8 = lambda v: jnp.pad(v.reshape(1, SSM_HEADS), ((0, 0), (0, LANES - SSM_HEADS)))
    return pl.pallas_call(
        _ssd_kernel,
        grid=(bsz, nc),
        in_specs=[pl.BlockSpec((None, SSM_CHUNK, SSM_CONV_CH), cur),
                  pl.BlockSpec((None, SSM_CHUNK, SSM_INNER), cur),
                  pl.BlockSpec((None, SSM_CHUNK, LANES), cur),
                  _const_spec((CONV_WIDTH, SSM_CONV_CH)), _const_spec((1, SSM_CONV_CH)),
                  _const_spec((1, LANES)), _const_spec((1, LANES)),
                  pl.BlockSpec(memory_space=pltpu.SMEM),
                  _const_spec((1, SSM_INNER))],
        out_specs=[pl.BlockSpec((None, SSM_CHUNK, SSM_INNER), cur),
                   pl.BlockSpec((None, SSM_INNER, SSM_STATE), per_b)],
        out_shape=[jax.ShapeDtypeStruct((bsz, length, SSM_INNER), F32),
                   jax.ShapeDtypeStruct((bsz, SSM_INNER, SSM_STATE), F32)],
        scratch_shapes=[pltpu.VMEM((SSM_CHUNK + SUBLANES, SSM_CONV_CH), F32),
                        pltpu.VMEM((SSM_INNER, SSM_STATE), F32)],
        compiler_params=_cparams(("parallel", "arbitrary")),
        name="ssd_mixer",
    )(xbc, z, dt, conv_w, conv_b.reshape(1, SSM_CONV_CH), pad8(dt_bias), pad8(a_log),
      d_skip.reshape(1, SSM_HEADS), norm_g.reshape(1, SSM_INNER))


def _unit_lower_inverses(lmats, row, col):
    c = lmats[0].shape[0]
    mm = lambda a, b: jnp.dot(a.astype(BF16), b.astype(BF16), preferred_element_type=F32)
    eye = (row == col).astype(F32)
    blk = SUBLANES
    same = (row // blk) == (col // blk)
    xs = [jnp.where(same, -l, 0.0) for l in lmats]
    invs = [eye + x for x in xs]
    p = blk
    while p > 2:
        xs = [mm(x, x) for x in xs]
        invs = [i + mm(i, x) for i, x in zip(invs, xs)]
        p //= 2
    while blk < c:
        outer = ((row // (2 * blk)) == (col // (2 * blk))) & ((row // blk) != (col // blk))
        ts = [mm(i, jnp.where(outer, l, 0.0)) for i, l in zip(invs, lmats)]
        invs = [i - mm(t, i) for i, t in zip(invs, ts)]
        blk *= 2
    return invs


def _dn_kernel(qkv_ref, z_ref, ba_ref, cw_ref, dtb_ref, alog_ref, nrm_ref, o_ref, sout_ref, xx_scr, s_scr):
    nseq, c_len = qkv_ref.shape[0], qkv_ref.shape[1]
    seqs = range(nseq)
    chains = [(s, h) for s in seqs for h in range(DN_HEADS)]
    ids = range(len(chains))

    @pl.when(pl.program_id(1) == 0)
    def _():
        xx_scr[:, 0:SUBLANES, :] = jnp.zeros((nseq, SUBLANES, xx_scr.shape[2]), F32)
        s_scr[...] = jnp.zeros_like(s_scr)

    row = lax.broadcasted_iota(jnp.int32, (c_len, c_len), 0)
    col = lax.broadcasted_iota(jnp.int32, (c_len, c_len), 1)
    incl = row >= col
    strict = row > col
    tri = incl.astype(F32)
    nrm = nrm_ref[...]
    bf = lambda t: t.astype(BF16)
    mm = lambda a, b: jnp.dot(bf(a), bf(b), preferred_element_type=F32)

    act, beta, cum, cum_t, e_cum, e_rest, e_last = [], [], [], [], [], [], []
    for s in seqs:
        act.append(_silu(_chunk_conv(qkv_ref.at[s], xx_scr.at[s], cw_ref, c_len)))
        _carry_conv_tail(xx_scr.at[s], c_len)
        ba = ba_ref[s]
        beta.append(_sigmoid(ba))
        gate = -jnp.exp(alog_ref[...]) * _softplus(ba + dtb_ref[...])
        cum_s = jnp.dot(tri, gate, preferred_element_type=F32, precision=lax.Precision.HIGHEST)
        c_last = cum_s[c_len - 1:c_len, :]
        cum.append(cum_s)
        cum_t.append(cum_s.T)
        e_cum.append(jnp.exp(cum_s))
        e_rest.append(jnp.exp(c_last - cum_s))
        e_last.append(jnp.exp(c_last))

    q, k, kb, rhs, decay = [], [], [], [], []
    for i in ids:
        s, h = chains[i]
        gl = DN_HEADS + h
        q_h = act[s][:, h * DN_DK:(h + 1) * DN_DK]
        k_h = act[s][:, DN_QK + h * DN_DK:DN_QK + (h + 1) * DN_DK]
        v_h = act[s][:, 2 * DN_QK + h * DN_DV:2 * DN_QK + (h + 1) * DN_DV]
        q.append(q_h * lax.rsqrt(jnp.sum(q_h * q_h, axis=-1, keepdims=True) + EPS) * (DN_DK ** -0.5))
        k.append(k_h * lax.rsqrt(jnp.sum(k_h * k_h, axis=-1, keepdims=True) + EPS))
        beta_c = beta[s][:, h:h + 1]
        kb.append(k[i] * beta_c)
        rhs.append(jnp.concatenate([v_h * beta_c, kb[i] * e_cum[s][:, gl:gl + 1]], axis=1))
        decay.append(jnp.exp(jnp.where(incl, cum[s][:, gl:gl + 1] - cum_t[s][gl:gl + 1, :], -jnp.inf)))
    kq = [lax.dot_general(bf(jnp.concatenate([kb[i], q[i]], axis=0)), bf(k[i]), NT_DIMS,
                          preferred_element_type=F32) for i in ids]
    lmat = [jnp.where(strict, kq[i][:c_len] * decay[i], 0.0) for i in ids]
    attn = [kq[i][c_len:] * decay[i] for i in ids]
    tinv = _unit_lower_inverses(lmat, row, col)
    uw = [mm(tinv[i], rhs[i]) for i in ids]
    s_old = [s_scr[s, h * DN_DK:(h + 1) * DN_DK, :] for s, h in chains]
    qe = [q[i] * e_cum[s][:, DN_HEADS + h:DN_HEADS + h + 1] for i, (s, h) in enumerate(chains)]
    wq = [mm(jnp.concatenate([uw[i][:, DN_DV:], qe[i]], axis=0), s_old[i]) for i in ids]
    v_new = [uw[i][:, :DN_DV] - wq[i][:c_len] for i in ids]
    o = [wq[i][c_len:] + mm(attn[i], v_new[i]) for i in ids]
    for i in ids:
        s, h = chains[i]
        gl = DN_HEADS + h
        s_scr[s, h * DN_DK:(h + 1) * DN_DK, :] = s_old[i] * e_last[s][:, gl:gl + 1] + lax.dot_general(
            bf(k[i] * e_rest[s][:, gl:gl + 1]), bf(v_new[i]), TN_DIMS, preferred_element_type=F32)
    for s in seqs:
        z = z_ref[s]
        o_ref[s] = jnp.concatenate(
            [_rms(o[s * DN_HEADS + h], nrm) * _silu(z[:, h * DN_DV:(h + 1) * DN_DV]) for h in range(DN_HEADS)],
            axis=1)
    sout_ref[...] = s_scr[...]


def dn_mixer(qkv, z, ba, conv_w, dt_bias, a_log, norm_g, nseq):
    bsz, length = qkv.shape[0], qkv.shape[1]
    nc = length // DN_CHUNK
    cur = lambda b, c: (b, c, 0)
    per_b = lambda b, c: (b, 0, 0)
    pad_a = lambda v: jnp.pad(v.reshape(1, DN_HEADS), ((0, 0), (DN_HEADS, LANES - 2 * DN_HEADS)))
    return pl.pallas_call(
        _dn_kernel,
        grid=(bsz // nseq, nc),
        in_specs=[pl.BlockSpec((nseq, DN_CHUNK, DN_CONV_CH), cur),
                  pl.BlockSpec((nseq, DN_CHUNK, DN_VW), cur),
                  pl.BlockSpec((nseq, DN_CHUNK, LANES), cur),
                  _const_spec((CONV_WIDTH, DN_CONV_CH)),
                  _const_spec((1, LANES)), _const_spec((1, LANES)), _const_spec((1, DN_DV))],
        out_specs=[pl.BlockSpec((nseq, DN_CHUNK, DN_VW), cur),
                   pl.BlockSpec((nseq, DN_QK, DN_DV), per_b)],
        out_shape=[jax.ShapeDtypeStruct((bsz, length, DN_VW), F32),
                   jax.ShapeDtypeStruct((bsz, DN_QK, DN_DV), F32)],
        scratch_shapes=[pltpu.VMEM((nseq, DN_CHUNK + SUBLANES, DN_CONV_CH), F32),
                        pltpu.VMEM((nseq, DN_QK, DN_DV), F32)],
        compiler_params=_cparams(("parallel", "arbitrary")),
        name="dn_mixer",
    )(qkv, z, ba, conv_w, pad_a(dt_bias), pad_a(a_log), norm_g.reshape(1, DN_DV))


X_SUB = D_MODEL // LANES


def _rows_to_slabs(x):
    parts = jnp.stack([x[:, s * LANES:(s + 1) * LANES] for s in range(X_SUB)], axis=0)
    return jnp.transpose(parts, (1, 0, 2))


def _slabs_to_rows(slab):
    parts = jnp.transpose(slab, (1, 0, 2))
    return jnp.concatenate([parts[s] for s in range(X_SUB)], axis=1)


def _resid_proj_kernel(x_ref, *refs, n_in):
    acc = x_ref[...]
    for a_ref, w_ref in zip(refs[:n_in], refs[n_in:2 * n_in]):
        acc = acc + jnp.dot(a_ref[...].astype(BF16), w_ref[...], preferred_element_type=F32)
    refs[2 * n_in][...] = acc


def resid_proj(x, acts, ws, tm):
    t = x.shape[0]
    n_in = len(acts)
    tok = lambda i: (i, 0)
    return pl.pallas_call(
        functools.partial(_resid_proj_kernel, n_in=n_in),
        grid=(t // tm,),
        in_specs=([pl.BlockSpec((tm, D_MODEL), tok)] + [pl.BlockSpec((tm, a.shape[1]), tok) for a in acts]
                  + [_const_spec(w.shape) for w in ws]),
        out_specs=pl.BlockSpec((tm, D_MODEL), tok),
        out_shape=jax.ShapeDtypeStruct((t, D_MODEL), F32),
        compiler_params=_cparams(("parallel",)),
        name="resid_proj",
    )(x, *acts, *ws)


def _split_bf16(v):
    hi = v.astype(BF16)
    return hi, (v - hi.astype(F32)).astype(BF16)


def _router_kernel(x_ref, g_ref, wr_hi_ref, wr_lo_ref, br_ref, xg_ref, gates_ref, grp_ref, rank_ref, cnt_ref,
                   carry_scr):
    tm = x_ref.shape[0]

    @pl.when(pl.program_id(0) == 0)
    def _():
        carry_scr[...] = jnp.zeros_like(carry_scr)

    x = x_ref[...]
    h_hi, h_lo = _split_bf16(_rms(x, g_ref[...]))
    w_hi, w_lo = wr_hi_ref[...], wr_lo_ref[...]
    logits = (jnp.dot(h_hi, w_hi, preferred_element_type=F32) + jnp.dot(h_lo, w_hi, preferred_element_type=F32)
              + jnp.dot(h_hi, w_lo, preferred_element_type=F32)) + br_ref[...]
    lt = logits.T
    sub = lax.broadcasted_iota(jnp.int32, (SUBLANES, tm), 0)
    neg = -jnp.inf
    glog = jnp.where(sub < N_GROUPS, lt[N_EXPERTS:N_EXPERTS + SUBLANES], neg)
    gmax = jnp.max(glog, axis=0, keepdims=True)
    g_p = 1.0 / jnp.sum(jnp.exp(glog - gmax), axis=0, keepdims=True)
    g_i = jnp.min(jnp.where(glog == gmax, sub, SUBLANES), axis=0, keepdims=True)
    sel = lt[0:EXPERTS_PER_GROUP]
    for g in range(1, N_GROUPS):
        sel = jnp.where(g_i == g, lt[g * EXPERTS_PER_GROUP:(g + 1) * EXPERTS_PER_GROUP], sel)
    m1 = jnp.max(sel, axis=0, keepdims=True)
    zsum = jnp.sum(jnp.exp(sel - m1), axis=0, keepdims=True)
    i1 = jnp.min(jnp.where(sel == m1, sub, SUBLANES), axis=0, keepdims=True)
    sel2 = jnp.where(sub == i1, neg, sel)
    m2 = jnp.max(sel2, axis=0, keepdims=True)
    i2 = jnp.min(jnp.where(sel2 == m2, sub, SUBLANES), axis=0, keepdims=True)
    p1 = 1.0 / zsum
    p2 = jnp.exp(m2 - m1) / zsum
    gate1 = g_p * p1 / (p1 + p2)
    gate2 = g_p * p2 / (p1 + p2)
    gates = jnp.where(sub == i1, gate1, 0.0) + jnp.where(sub == i2, gate2, 0.0)

    onehot = (sub == g_i).astype(F32)
    before = (lax.broadcasted_iota(jnp.int32, (tm, tm), 0) < lax.broadcasted_iota(jnp.int32, (tm, tm), 1))
    rank_in_tile = jnp.dot(onehot.astype(BF16), before.astype(BF16), preferred_element_type=F32)
    carry = carry_scr[...]
    rank = jnp.sum(onehot * (rank_in_tile + carry[:, 0:1]), axis=0, keepdims=True)
    grp_ref[...] = g_i
    rank_ref[...] = rank.astype(jnp.int32)
    new_carry = carry + jnp.sum(onehot, axis=1, keepdims=True)
    carry_scr[...] = new_carry
    cnt_ref[...] = new_carry

    gates_ref[...] = jnp.concatenate([gates, jnp.zeros((LANES - SUBLANES, tm), F32)], axis=0).T
    xg_ref[...] = _rows_to_slabs(x)


def moe_router(x, gain, w_router_pad, b_router_pad, tm):
    t = x.shape[0]
    w_hi = w_router_pad.astype(BF16)
    w_lo = (w_router_pad - w_hi.astype(F32)).astype(BF16)
    return pl.pallas_call(
        _router_kernel,
        grid=(t // tm,),
        in_specs=[pl.BlockSpec((tm, D_MODEL), lambda i: (i, 0)), _const_spec((1, D_MODEL)),
                  _const_spec((D_MODEL, LANES)), _const_spec((D_MODEL, LANES)), _const_spec((1, LANES))],
        out_specs=[pl.BlockSpec((tm, X_SUB, LANES), lambda i: (i, 0, 0)),
                   pl.BlockSpec((tm, LANES), lambda i: (i, 0)),
                   pl.BlockSpec((None, 1, tm), lambda i: (i, 0, 0)),
                   pl.BlockSpec((None, 1, tm), lambda i: (i, 0, 0)),
                   _const_spec((SUBLANES, LANES))],
        out_shape=[jax.ShapeDtypeStruct((t, X_SUB, LANES), F32),
                   jax.ShapeDtypeStruct((t, LANES), F32),
                   jax.ShapeDtypeStruct((t // tm, 1, tm), jnp.int32),
                   jax.ShapeDtypeStruct((t // tm, 1, tm), jnp.int32),
                   jax.ShapeDtypeStruct((SUBLANES, LANES), F32)],
        scratch_shapes=[pltpu.VMEM((SUBLANES, LANES), F32)],
        compiler_params=_cparams(("arbitrary",)),
        name="moe_router",
    )(x, gain.reshape(1, D_MODEL), w_hi, w_lo, b_router_pad)


def _issue_row_copies(n_rows, make_copy):
    def trip(i, carry):
        for u in range(ROW_DMA_UNROLL):
            make_copy(i * ROW_DMA_UNROLL + u).start(priority=u % DMA_PRIORITIES)
        return carry

    lax.fori_loop(0, n_rows // ROW_DMA_UNROLL, trip, 0)


def _dispatch_kernel(pos_ref, pad_lo_ref, pad_hi_ref, xg_ref, gates_ref, xs_hbm, gs_hbm, zero_scr,
                     sem, gsem, zsem, zgsem):
    tm = xg_ref.shape[0]
    gate_row = lambda ref, r: ref.at[pl.ds(r, 1)]
    _issue_row_copies(tm, lambda r: pltpu.make_async_copy(xg_ref.at[r], xs_hbm.at[pos_ref[0, r]], sem))
    _issue_row_copies(tm, lambda r: pltpu.make_async_copy(gate_row(gates_ref, r),
                                                          gate_row(gs_hbm, pos_ref[0, r]), gsem))
    pltpu.make_async_copy(xg_ref, xs_hbm.at[pl.ds(0, tm)], sem).wait()
    pltpu.make_async_copy(gates_ref, gs_hbm.at[pl.ds(0, tm)], gsem).wait()

    @pl.when(pl.program_id(0) == pl.num_programs(0) - 1)
    def _():
        zero_scr[...] = jnp.zeros_like(zero_scr)
        zero_gate = gate_row(zero_scr, 0)
        for k in range(pad_lo_ref.shape[0]):
            def fill(j, carry):
                pltpu.make_async_copy(zero_scr, xs_hbm.at[j], zsem).start()
                pltpu.make_async_copy(zero_gate, gate_row(gs_hbm, j), zgsem).start()
                return carry

            def drain(j, carry):
                pltpu.make_async_copy(zero_scr, xs_hbm.at[j], zsem).wait()
                pltpu.make_async_copy(zero_gate, gate_row(gs_hbm, j), zgsem).wait()
                return carry

            lax.fori_loop(pad_lo_ref[k], pad_hi_ref[k], fill, 0)
            lax.fori_loop(pad_lo_ref[k], pad_hi_ref[k], drain, 0)


def moe_dispatch(pos, pad_lo, pad_hi, xg, gates, tm, n_slots):
    t = xg.shape[0]
    dma = pltpu.SemaphoreType.DMA
    return pl.pallas_call(
        _dispatch_kernel,
        grid=(t // tm,),
        in_specs=[pl.BlockSpec((None, 1, tm), lambda i: (i, 0, 0), memory_space=pltpu.SMEM),
                  pl.BlockSpec(memory_space=pltpu.SMEM), pl.BlockSpec(memory_space=pltpu.SMEM),
                  pl.BlockSpec((tm, X_SUB, LANES), lambda i: (i, 0, 0)),
                  pl.BlockSpec((tm, LANES), lambda i: (i, 0))],
        out_specs=[pl.BlockSpec(memory_space=pl.ANY), pl.BlockSpec(memory_space=pl.ANY)],
        out_shape=[jax.ShapeDtypeStruct((n_slots, X_SUB, LANES), F32),
                   jax.ShapeDtypeStruct((n_slots, LANES), F32)],
        scratch_shapes=[pltpu.VMEM((X_SUB, LANES), F32), dma, dma, dma, dma],
        compiler_params=_cparams(("arbitrary",), has_side_effects=True),
        name="moe_dispatch",
    )(pos, pad_lo, pad_hi, xg, gates)


def _collect_kernel(pos_ref, os_hbm, out_ref, buf, sem):
    tm = out_ref.shape[0]
    _issue_row_copies(tm, lambda r: pltpu.make_async_copy(os_hbm.at[pos_ref[0, r]], buf.at[r], sem))
    pltpu.make_async_copy(os_hbm.at[pl.ds(0, tm)], buf, sem).wait()
    out_ref[...] = _slabs_to_rows(buf[...])


def moe_collect(pos, os_sorted, t, tm):
    return pl.pallas_call(
        _collect_kernel,
        grid=(t // tm,),
        in_specs=[pl.BlockSpec((None, 1, tm), lambda i: (i, 0, 0), memory_space=pltpu.SMEM),
                  pl.BlockSpec(memory_space=pl.ANY)],
        out_specs=pl.BlockSpec((tm, D_MODEL), lambda i: (i, 0)),
        out_shape=jax.ShapeDtypeStruct((t, D_MODEL), F32),
        scratch_shapes=[pltpu.VMEM((tm, X_SUB, LANES), F32), pltpu.SemaphoreType.DMA],
        compiler_params=_cparams(("arbitrary",), has_side_effects=True),
        name="moe_collect",
    )(pos, os_sorted)


def _group_experts_kernel(grp_ref, xs_ref, gates_ref, g_ref, wg_ref, wu_ref, wd_ref, o_ref):
    del grp_ref
    x = _slabs_to_rows(xs_ref[...])
    gates = gates_ref[...]
    h = _rms(x, g_ref[...]).astype(BF16)
    acc = x
    for e in range(EXPERTS_PER_GROUP):
        gate = jnp.dot(h, wg_ref[e], preferred_element_type=F32)
        up = jnp.dot(h, wu_ref[e], preferred_element_type=F32)
        act = (_silu(gate) * up).astype(BF16)
        acc = acc + gates[:, e:e + 1] * jnp.dot(act, wd_ref[e], preferred_element_type=F32)
    o_ref[...] = _rows_to_slabs(acc)


def moe_group_experts(tile_grp, xs, gs, gain, wg, wu, wd, tile):
    n_steps = xs.shape[0] // tile
    rows = lambda i, grp: (i, 0, 0)
    wts = lambda i, grp: (grp[i], 0, 0, 0)
    return pl.pallas_call(
        _group_experts_kernel,
        grid_spec=pltpu.PrefetchScalarGridSpec(
            num_scalar_prefetch=1,
            grid=(n_steps,),
            in_specs=[pl.BlockSpec((tile, X_SUB, LANES), rows),
                      pl.BlockSpec((tile, LANES), lambda i, grp: (i, 0)),
                      pl.BlockSpec((1, D_MODEL), lambda i, grp: (0, 0)),
                      pl.BlockSpec((None, EXPERTS_PER_GROUP, D_MODEL, EXPERT_FF), wts),
                      pl.BlockSpec((None, EXPERTS_PER_GROUP, D_MODEL, EXPERT_FF), wts),
                      pl.BlockSpec((None, EXPERTS_PER_GROUP, EXPERT_FF, D_MODEL), wts)],
            out_specs=pl.BlockSpec((tile, X_SUB, LANES), rows)),
        out_shape=jax.ShapeDtypeStruct((xs.shape[0], X_SUB, LANES), F32),
        compiler_params=_cparams(("arbitrary",)),
        name="moe_group_experts",
    )(tile_grp, xs, gs, gain.reshape(1, D_MODEL), wg, wu, wd)


def mix_out_and_moe(x, acts, ws, gain, w_route, b_route, wg, wu, wd):
    t = x.shape[0]
    tile = MOE_TILE if t % MOE_TILE == 0 else t
    tm = MOE_TOKEN_TILE if t % MOE_TOKEN_TILE == 0 else tile
    n_tiles = t // tile + N_GROUPS
    x = resid_proj(x, acts, ws, tile)
    xg, gates, grp, rank, cnt = moe_router(x, gain, w_route, b_route, tm)
    counts = cnt[:N_GROUPS, 0].astype(jnp.int32)
    tile_ends = jnp.cumsum((counts + tile - 1) // tile)
    offs = (tile_ends - (counts + tile - 1) // tile) * tile
    pos = rank + sum(jnp.where(grp == g, offs[g], 0) for g in range(N_GROUPS))
    pad_lo = jnp.concatenate([offs + counts, tile_ends[-1:] * tile])
    pad_hi = jnp.concatenate([tile_ends * tile, jnp.full((1,), n_tiles * tile, jnp.int32)])
    tile_grp = jnp.minimum(jnp.sum((jnp.arange(n_tiles, dtype=jnp.int32)[:, None] >= tile_ends[None, :])
                                   .astype(jnp.int32), axis=1), N_GROUPS - 1)
    xs, gs = moe_dispatch(pos, pad_lo, pad_hi, xg, gates, tm, n_tiles * tile)
    os_sorted = moe_group_experts(tile_grp, xs, gs, gain, wg, wu, wd, tile)
    return moe_collect(pos, os_sorted, t, tm)


BATCH_NT = (((2,), (2,)), ((0,), (0,)))
BATCH_NN = (((2,), (1,)), ((0,), (0,)))


def _token_major(parts):
    return jnp.transpose(jnp.stack(parts, axis=0), (1, 0, 2))


def _pad_rows(parts, tb, width):
    return parts + [jnp.zeros((tb, width), F32)] * (SUBLANES - len(parts))


def _spread(v, first, n, width):
    tb = v.shape[0]
    return jnp.concatenate([jnp.broadcast_to(v[:, first + i:first + i + 1], (tb, width)) for i in range(n)], axis=1)


def _columns(rows):
    tb, m = rows.shape
    return jnp.concatenate([rows, jnp.zeros((LANES - tb, m), F32)], axis=0).T


def _one_step_conv(x_ref, buf_ref, w_ref):
    return (w_ref[0:1, :] * buf_ref[:, 0, :] + w_ref[1:2, :] * buf_ref[:, 1, :] + w_ref[2:3, :] * buf_ref[:, 2, :]
            + w_ref[3:4, :] * x_ref[...])


def _swa_decode_kernel(sink_ref, q_ref, k_ref, v_ref, kc_ref, vc_ref, qn_ref, kn_ref, o_ref, kout_ref):
    tb = q_ref.shape[0]
    lo = lax.broadcasted_iota(jnp.int32, (tb, LANES), 1) < HEAD_DIM

    def pair_rms(t, gain):
        sq = t * t
        s_lo = jnp.sum(jnp.where(lo, sq, 0.0), axis=-1, keepdims=True)
        s_hi = jnp.sum(jnp.where(lo, 0.0, sq), axis=-1, keepdims=True)
        return t * lax.rsqrt(jnp.where(lo, s_lo, s_hi) * (1.0 / HEAD_DIM) + EPS) * gain

    qn2 = jnp.concatenate([qn_ref[...], qn_ref[...]], axis=1)
    kn2 = jnp.concatenate([kn_ref[...], kn_ref[...]], axis=1)
    k_new = pair_rms(k_ref[...], kn2)
    kout_ref[...] = k_new
    v_new = v_ref[...]
    rows = []
    for h in range(SWA_HEADS):
        t = pair_rms(q_ref[:, (h // 2) * LANES:(h // 2 + 1) * LANES], qn2) * (HEAD_DIM ** -0.5)
        j = h // SWA_GROUP
        if h % 2 != j:
            t = pltpu.roll(t, HEAD_DIM, axis=1)
        rows.append(jnp.where(lo if j == 0 else jnp.logical_not(lo), t, 0.0))
    q8 = _token_major(rows)
    s = lax.dot_general(q8.astype(BF16), kc_ref[...].astype(BF16), BATCH_NT, preferred_element_type=F32)
    s_new = jnp.sum(q8 * k_new[:, None, :], axis=-1, keepdims=True)
    sub = lax.broadcasted_iota(jnp.int32, (1, SWA_HEADS, 1), 1)
    sink = jnp.zeros((1, SWA_HEADS, 1), F32)
    for h in range(SWA_HEADS):
        sink = jnp.where(sub == h, sink_ref[h], sink)
    m = jnp.maximum(jnp.maximum(jnp.max(s, axis=-1, keepdims=True), s_new), sink)
    p = jnp.exp(s - m)
    p_new = jnp.exp(s_new - m)
    den = jnp.sum(p, axis=-1, keepdims=True) + p_new + jnp.exp(sink - m)
    o8 = lax.dot_general((p / den).astype(BF16), vc_ref[...].astype(BF16), BATCH_NN,
                         preferred_element_type=F32) + (p_new / den) * v_new[:, None, :]
    o_h = jnp.transpose(o8, (1, 0, 2))
    tiles = []
    for t in range(SWA_HEADS // 2):
        halves = []
        for h in (2 * t, 2 * t + 1):
            piece = o_h[h]
            if h % 2 != h // SWA_GROUP:
                piece = pltpu.roll(piece, HEAD_DIM, axis=1)
            halves.append(piece)
        tiles.append(jnp.where(lo, halves[0], halves[1]))
    o_ref[...] = jnp.concatenate(tiles, axis=1)


def swa_decode(q, k, v, k_cache, v_cache, q_norm, k_norm, sinks):
    bsz = q.shape[0]
    tok = lambda i: (i, 0)
    tok3 = lambda i: (i, 0, 0)
    return pl.pallas_call(
        _swa_decode_kernel,
        grid=(bsz // DEC_TB,),
        in_specs=[pl.BlockSpec(memory_space=pltpu.SMEM),
                  pl.BlockSpec((DEC_TB, SWA_Q), tok), pl.BlockSpec((DEC_TB, SWA_KV), tok),
                  pl.BlockSpec((DEC_TB, SWA_KV), tok),
                  pl.BlockSpec((DEC_TB, WINDOW, SWA_KV), tok3), pl.BlockSpec((DEC_TB, WINDOW, SWA_KV), tok3),
                  _const_spec((1, HEAD_DIM)), _const_spec((1, HEAD_DIM))],
        out_specs=[pl.BlockSpec((DEC_TB, SWA_Q), tok), pl.BlockSpec((DEC_TB, SWA_KV), tok)],
        out_shape=[jax.ShapeDtypeStruct((bsz, SWA_Q), F32), jax.ShapeDtypeStruct((bsz, SWA_KV), F32)],
        compiler_params=_cparams(("parallel",)),
        name="swa_decode",
    )(sinks, q, k, v, k_cache, v_cache, q_norm.reshape(1, HEAD_DIM), k_norm.reshape(1, HEAD_DIM))


def _ssd_decode_kernel(x_ref, buf_ref, z_ref, dt_ref, cw_ref, cb_ref, dtb_ref, alog_ref, dd_ref,
                       nrm_ref, h0_ref, y_ref, hout_ref):
    tb = x_ref.shape[0]
    act = _silu(_one_step_conv(x_ref, buf_ref, cw_ref) + cb_ref[...])
    xs = act[:, :SSM_INNER]
    bm = act[:, SSM_INNER:SSM_INNER + SSM_GROUPS * SSM_STATE]
    cm = act[:, SSM_INNER + SSM_GROUPS * SSM_STATE:]
    dt = _softplus(dt_ref[...] + dtb_ref[...])
    dec = jnp.exp(dt * (-jnp.exp(alog_ref[...])))
    dt_x = _spread(dt, 0, SSM_HEADS, SSM_HEAD_DIM)
    dec_x = _spread(dec, 0, SSM_HEADS, SSM_HEAD_DIM)
    gw = SSM_INNER // SSM_GROUPS
    grp = lambda t, g: t[:, g * SSM_STATE:(g + 1) * SSM_STATE]
    cb_x = jnp.concatenate([jnp.broadcast_to(jnp.sum(grp(cm, g) * grp(bm, g), axis=-1, keepdims=True), (tb, gw))
                            for g in range(SSM_GROUPS)], axis=1)
    h0 = h0_ref[...]
    c8 = _token_major(_pad_rows([grp(cm, g) for g in range(SSM_GROUPS)], tb, SSM_STATE))
    ys = jnp.transpose(lax.dot_general(c8.astype(BF16), h0.astype(BF16), BATCH_NT, preferred_element_type=F32),
                       (1, 0, 2))
    y_state = jnp.concatenate([ys[g][:, g * gw:(g + 1) * gw] for g in range(SSM_GROUPS)], axis=1)
    y = cb_x * dt_x * xs + y_state * dec_x + dd_ref[...] * xs
    y = y * _silu(z_ref[...])
    nrm = nrm_ref[...]
    y_ref[...] = jnp.concatenate(
        [_rms(y[:, g * gw:(g + 1) * gw], nrm[:, g * gw:(g + 1) * gw]) for g in range(SSM_GROUPS)], axis=1)
    u_cols = _columns(dt_x * xs)
    hpg = SSM_HEADS // SSM_GROUPS
    for b in range(tb):
        for g in range(SSM_GROUPS):
            rows = slice(g * gw, (g + 1) * gw)
            decay = jnp.concatenate([jnp.broadcast_to(dec[b:b + 1, h:h + 1], (SSM_HEAD_DIM, SSM_STATE))
                                     for h in range(g * hpg, (g + 1) * hpg)], axis=0)
            hout_ref[b, rows, :] = h0[b, rows, :] * decay + u_cols[rows, b:b + 1] * grp(bm, g)[b:b + 1, :]


def ssd_decode(xbc, bufs, z, dt, conv_w, conv_b, dt_bias, a_log, d_skip, norm_g, h0):
    bsz = xbc.shape[0]
    tok = lambda i: (i, 0)
    tok3 = lambda i: (i, 0, 0)
    pad8 = lambda v: jnp.pad(v.reshape(1, SSM_HEADS), ((0, 0), (0, LANES - SSM_HEADS)))
    row = lambda w: pl.BlockSpec((DEC_TB, w), tok)
    return pl.pallas_call(
        _ssd_decode_kernel,
        grid=(bsz // DEC_TB,),
        in_specs=[row(SSM_CONV_CH), pl.BlockSpec((DEC_TB, CONV_WIDTH - 1, SSM_CONV_CH), tok3),
                  row(SSM_INNER), row(LANES),
                  _const_spec((CONV_WIDTH, SSM_CONV_CH)), _const_spec((1, SSM_CONV_CH)),
                  _const_spec((1, LANES)), _const_spec((1, LANES)), _const_spec((1, SSM_INNER)),
                  _const_spec((1, SSM_INNER)),
                  pl.BlockSpec((DEC_TB, SSM_INNER, SSM_STATE), tok3)],
        out_specs=[row(SSM_INNER), pl.BlockSpec((DEC_TB, SSM_INNER, SSM_STATE), tok3)],
        out_shape=[jax.ShapeDtypeStruct((bsz, SSM_INNER), F32),
                   jax.ShapeDtypeStruct((bsz, SSM_INNER, SSM_STATE), F32)],
        compiler_params=_cparams(("parallel",)),
        name="ssd_decode",
    )(xbc, bufs, z, dt, conv_w, conv_b.reshape(1, SSM_CONV_CH), pad8(dt_bias), pad8(a_log),
      jnp.repeat(d_skip, SSM_HEAD_DIM).reshape(1, SSM_INNER), norm_g.reshape(1, SSM_INNER), h0)


def _dn_decode_kernel(x_ref, buf_ref, z_ref, ba_ref, cw_ref, dtb_ref, alog_ref, nrm_ref, s0_ref,
                      o_ref, sout_ref):
    tb = x_ref.shape[0]
    act = _silu(_one_step_conv(x_ref, buf_ref, cw_ref))
    ba = ba_ref[...]
    beta = _sigmoid(ba)
    eg = jnp.exp(-jnp.exp(alog_ref[...]) * _softplus(ba + dtb_ref[...]))
    z = z_ref[...]
    nrm = nrm_ref[...]
    outs = []
    for h in range(DN_HEADS):
        q_h = act[:, h * DN_DK:(h + 1) * DN_DK]
        k_h = act[:, DN_QK + h * DN_DK:DN_QK + (h + 1) * DN_DK]
        v_h = act[:, 2 * DN_QK + h * DN_DV:2 * DN_QK + (h + 1) * DN_DV]
        q_h = q_h * lax.rsqrt(jnp.sum(q_h * q_h, axis=-1, keepdims=True) + EPS) * (DN_DK ** -0.5)
        k_h = k_h * lax.rsqrt(jnp.sum(k_h * k_h, axis=-1, keepdims=True) + EPS)
        beta_c = beta[:, h:h + 1]
        eg_c = eg[:, DN_HEADS + h:DN_HEADS + h + 1]
        s_h = s0_ref[:, h * DN_DK:(h + 1) * DN_DK, :]
        kq8 = _token_major(_pad_rows([k_h, q_h], tb, DN_DK))
        r = jnp.transpose(lax.dot_general(kq8.astype(BF16), s_h.astype(BF16), BATCH_NN,
                                          preferred_element_type=F32), (1, 0, 2))
        v_new = beta_c * v_h - (beta_c * eg_c) * r[0]
        o_h = eg_c * r[1] + jnp.sum(q_h * k_h, axis=-1, keepdims=True) * v_new
        outs.append(_rms(o_h, nrm) * _silu(z[:, h * DN_DV:(h + 1) * DN_DV]))
        k_cols = _columns(k_h)
        for b in range(tb):
            sout_ref[b, h * DN_DK:(h + 1) * DN_DK, :] = (
                s_h[b] * jnp.broadcast_to(eg_c[b:b + 1, :], (DN_DK, DN_DV)) + k_cols[:, b:b + 1] * v_new[b:b + 1, :])
    o_ref[...] = jnp.concatenate(outs, axis=1)


def dn_decode(qkv, bufs, z, ba, conv_w, dt_bias, a_log, norm_g, s0):
    bsz = qkv.shape[0]
    tok = lambda i: (i, 0)
    tok3 = lambda i: (i, 0, 0)
    pad_a = lambda v: jnp.pad(v.reshape(1, DN_HEADS), ((0, 0), (DN_HEADS, LANES - 2 * DN_HEADS)))
    row = lambda w: pl.BlockSpec((DEC_TB, w), tok)
    return pl.pallas_call(
        _dn_decode_kernel,
        grid=(bsz // DEC_TB,),
        in_specs=[row(DN_CONV_CH), pl.BlockSpec((DEC_TB, CONV_WIDTH - 1, DN_CONV_CH), tok3),
                  row(DN_VW), row(LANES),
                  _const_spec((CONV_WIDTH, DN_CONV_CH)),
                  _const_spec((1, LANES)), _const_spec((1, LANES)), _const_spec((1, DN_DV)),
                  pl.BlockSpec((DEC_TB, DN_QK, DN_DV), tok3)],
        out_specs=[row(DN_VW), pl.BlockSpec((DEC_TB, DN_QK, DN_DV), tok3)],
        out_shape=[jax.ShapeDtypeStruct((bsz, DN_VW), F32), jax.ShapeDtypeStruct((bsz, DN_QK, DN_DV), F32)],
        compiler_params=_cparams(("parallel",)),
        name="dn_decode",
    )(qkv, bufs, z, ba, conv_w, pad_a(dt_bias), pad_a(a_log), norm_g.reshape(1, DN_DV), s0)


def _pad_cols(w, n_pad):
    return jnp.pad(w, ((0, 0), (0, n_pad - w.shape[1])))


EVEN_SEGS = ((0, 512), (512, 640), (640, 768), (768, 1280), (1280, 2304), (2304, 2432))
ODD_SEGS = ((0, 3072), (3072, 4096), (4096, 4224))


def _trunk(x_seq, p, states):
    prompt = states is None
    bsz, length = x_seq.shape[0], x_seq.shape[1]
    t = bsz * length
    tm = 512 if t % 512 == 0 else t
    x = x_seq.reshape(t, D_MODEL)
    seq = lambda u: u.reshape(bsz, length, u.shape[-1])

    q, k, v, z, xbc, dt = prenorm_proj(x, p['ln_mix'][0], p['w_in_even'], EVEN_SEGS, tm)
    if prompt:
        v3, xbc3 = seq(v), seq(xbc)
        att, k_normed = swa_attention(seq(q), seq(k), v3, p['q_norm'], p['k_norm'], p['attn_sinks'])
        new_k = k_normed[:, -WINDOW:].reshape(bsz, WINDOW, SWA_KV_HEADS, HEAD_DIM)
        new_v = v3[:, -WINDOW:].reshape(bsz, WINDOW, SWA_KV_HEADS, HEAD_DIM)
        new_ssm_conv = xbc3[:, -(CONV_WIDTH - 1):]
        y_ssm, new_h = ssd_mixer(xbc3, seq(z), seq(dt), p['ssm_conv_w'], p['ssm_conv_b'], p['ssm_dt_bias'],
                                 p['ssm_A_log'], p['ssm_D'], p['ssm_norm'])
    else:
        k_win, v_win, ssm_h, ssm_conv, dn_s, dn_conv = states
        kb = k_win.reshape(bsz, WINDOW, SWA_KV)
        vb = v_win.reshape(bsz, WINDOW, SWA_KV)
        att, k_normed = swa_decode(q, k, v, kb, vb, p['q_norm'], p['k_norm'], p['attn_sinks'])
        new_k = jnp.concatenate([kb[:, 1:], k_normed[:, None]], axis=1).reshape(bsz, WINDOW, SWA_KV_HEADS, HEAD_DIM)
        new_v = jnp.concatenate([vb[:, 1:], v[:, None]], axis=1).reshape(bsz, WINDOW, SWA_KV_HEADS, HEAD_DIM)
        new_ssm_conv = jnp.concatenate([ssm_conv[:, 1:], xbc[:, None, :]], axis=1)
        y_ssm, new_h = ssd_decode(xbc, ssm_conv, z, dt, p['ssm_conv_w'],
                                  p['ssm_conv_b'], p['ssm_dt_bias'], p['ssm_A_log'], p['ssm_D'], p['ssm_norm'],
                                  ssm_h.reshape(bsz, SSM_INNER, SSM_STATE))
    x = mix_out_and_moe(x, [att.reshape(t, SWA_Q), y_ssm.reshape(t, SSM_INNER)],
                        [p['w_out_even'][:SWA_Q], p['w_out_even'][SWA_Q:]],
                        p['ln_ffn'][0], p['w_route'][0], p['b_route'][0],
                        p['moe_w_gate'][0], p['moe_w_up'][0], p['moe_w_down'][0])

    qkv, zz, ba = prenorm_proj(x, p['ln_mix'][1], p['w_in_odd'], ODD_SEGS, tm)
    if prompt:
        qkv3 = seq(qkv)
        new_dn_conv = qkv3[:, -(CONV_WIDTH - 1):]
        o_dn, new_s = dn_mixer(qkv3, seq(zz), seq(ba), p['dn_conv_w'], p['dn_dt_bias'], p['dn_A_log'],
                               p['dn_norm'], DN_SEQS_PER_STEP)
    else:
        new_dn_conv = jnp.concatenate([dn_conv[:, 1:], qkv[:, None, :]], axis=1)
        o_dn, new_s = dn_decode(qkv, dn_conv, zz, ba, p['dn_conv_w'],
                                p['dn_dt_bias'], p['dn_A_log'], p['dn_norm'], dn_s.reshape(bsz, DN_QK, DN_DV))
    x = mix_out_and_moe(x, [o_dn.reshape(t, DN_VW)], [p['w_out_odd']],
                        p['ln_ffn'][1], p['w_route'][1], p['b_route'][1],
                        p['moe_w_gate'][1], p['moe_w_up'][1], p['moe_w_down'][1])

    return (x.reshape(bsz, length, D_MODEL), new_k[None], new_v[None],
            new_h.reshape(1, bsz, SSM_HEADS, SSM_HEAD_DIM, SSM_STATE), new_ssm_conv[None],
            new_s.reshape(1, bsz, DN_HEADS, DN_DK, DN_DV), new_dn_conv[None])


def kernel(x_prompt, x_sample, cache_k_win, cache_v_win, state_ssm, state_ssm_conv, state_dn, state_dn_conv,
           ln_mix, ln_ffn, w_in_even, q_norm, k_norm, attn_sinks, ssm_conv_w, ssm_conv_b, ssm_dt_bias,
           ssm_A_log, ssm_D, ssm_norm, w_out_even, w_in_odd, dn_conv_w, dn_dt_bias, dn_A_log, dn_norm,
           w_out_odd, moe_w_group, moe_b_group, moe_w_router, moe_b_router, moe_w_gate, moe_w_up, moe_w_down):
    w_route = _pad_cols(jnp.concatenate([moe_w_router, moe_w_group], axis=-1).reshape(-1, N_EXPERTS + N_GROUPS),
                        LANES).reshape(2, D_MODEL, LANES)
    b_route = _pad_cols(jnp.concatenate([moe_b_router, moe_b_group], axis=-1), LANES).reshape(2, 1, LANES)
    p = {
        'ln_mix': ln_mix, 'ln_ffn': ln_ffn,
        'w_in_even': _pad_cols(w_in_even[0], EVEN_SEGS[-1][1]).astype(BF16),
        'q_norm': q_norm[0], 'k_norm': k_norm[0], 'attn_sinks': attn_sinks[0],
        'ssm_conv_w': ssm_conv_w[0], 'ssm_conv_b': ssm_conv_b[0], 'ssm_dt_bias': ssm_dt_bias[0],
        'ssm_A_log': ssm_A_log[0], 'ssm_D': ssm_D[0], 'ssm_norm': ssm_norm[0],
        'w_out_even': w_out_even[0].astype(BF16),
        'w_in_odd': _pad_cols(w_in_odd[0], ODD_SEGS[-1][1]).astype(BF16),
        'dn_conv_w': dn_conv_w[0], 'dn_dt_bias': dn_dt_bias[0], 'dn_A_log': dn_A_log[0], 'dn_norm': dn_norm[0],
        'w_out_odd': w_out_odd[0].astype(BF16),
        'w_route': w_route, 'b_route': b_route,
        'moe_w_gate': moe_w_gate.astype(BF16).reshape(2, N_GROUPS, EXPERTS_PER_GROUP, D_MODEL, EXPERT_FF),
        'moe_w_up': moe_w_up.astype(BF16).reshape(2, N_GROUPS, EXPERTS_PER_GROUP, D_MODEL, EXPERT_FF),
        'moe_w_down': moe_w_down.astype(BF16).reshape(2, N_GROUPS, EXPERTS_PER_GROUP, EXPERT_FF, D_MODEL),
    }
    y_p, kp, vp, sp, scp, dnp, dncp = _trunk(x_prompt, p, None)
    sample_states = (cache_k_win[0], cache_v_win[0], state_ssm[0], state_ssm_conv[0], state_dn[0],
                     state_dn_conv[0])
    y_s, ks, vs, ss, scs, dns, dncs = _trunk(x_sample, p, sample_states)
    return (y_p, y_s, kp, ks, vp, vs, sp, ss, scp, scs, dnp, dns, dncp, dncs)
```

```python
import functools

import jax
import jax.numpy as jnp
from jax import lax
from jax.experimental import pallas as pl
from jax.experimental.pallas import tpu as pltpu

F32 = jnp.float32
BF16 = jnp.bfloat16
EPS = 1e-6

D_MODEL = 1024
SWA_HEADS = 8
SWA_KV_HEADS = 2
SWA_GROUP = SWA_HEADS // SWA_KV_HEADS
HEAD_DIM = 64
WINDOW = 128
SWA_Q = SWA_HEADS * HEAD_DIM
SWA_KV = SWA_KV_HEADS * HEAD_DIM
SSM_HEADS = 8
SSM_HEAD_DIM = 64
SSM_GROUPS = 2
SSM_STATE = 128
SSM_INNER = SSM_HEADS * SSM_HEAD_DIM
SSM_CHUNK = 128
SSM_CONV_CH = SSM_INNER + 2 * SSM_GROUPS * SSM_STATE
CONV_WIDTH = 4
DN_HEADS = 8
DN_DK = 128
DN_DV = 128
DN_CHUNK = 64
DN_QK = DN_HEADS * DN_DK
DN_VW = DN_HEADS * DN_DV
DN_CONV_CH = 2 * DN_QK + DN_VW
N_GROUPS = 4
EXPERTS_PER_GROUP = 8
N_EXPERTS = N_GROUPS * EXPERTS_PER_GROUP
EXPERT_FF = 256

LANES = 128
SUBLANES = 8
VMEM_LIMIT = 56 * 1024 * 1024
DMA_PRIORITIES = 2

DN_SEQS_PER_STEP = 2
MOE_TILE = 512
MOE_TOKEN_TILE = 1024
MOE_MOVE_TILE = 2048
DEC_TB = 8
ROW_DMA_UNROLL = 8

NT_DIMS = (((1,), (1,)), ((), ()))
TN_DIMS = (((0,), (0,)), ((), ()))


def _cparams(sem, **kw):
    return pltpu.CompilerParams(dimension_semantics=sem, vmem_limit_bytes=VMEM_LIMIT, **kw)


def _const_spec(shape):
    nd = len(shape)
    return pl.BlockSpec(shape, lambda *_: (0,) * nd)


def _sigmoid(x):
    return 1.0 / (1.0 + jnp.exp(-x))


def _silu(x):
    return x * _sigmoid(x)


def _softplus(x):
    return jnp.maximum(x, 0.0) + jnp.log(1.0 + jnp.exp(-jnp.abs(x)))


def _rms(x, gain):
    return x * lax.rsqrt(jnp.mean(x * x, axis=-1, keepdims=True) + EPS) * gain


def _prenorm_proj_kernel(x_ref, g_ref, w_ref, *out_refs, segs):
    h = _rms(x_ref[...], g_ref[...]).astype(BF16)
    for o_ref, (a, b) in zip(out_refs, segs):
        o_ref[...] = jnp.dot(h, w_ref[:, a:b], preferred_element_type=F32)


def prenorm_proj(x, gain, w_bf16, segs, tm):
    t = x.shape[0]
    n_pad = w_bf16.shape[1]
    return pl.pallas_call(
        functools.partial(_prenorm_proj_kernel, segs=segs),
        grid=(t // tm,),
        in_specs=[pl.BlockSpec((tm, D_MODEL), lambda i: (i, 0)),
                  _const_spec((1, D_MODEL)),
                  _const_spec((D_MODEL, n_pad))],
        out_specs=[pl.BlockSpec((tm, b - a), lambda i: (i, 0)) for a, b in segs],
        out_shape=[jax.ShapeDtypeStruct((t, b - a), F32) for a, b in segs],
        compiler_params=_cparams(("parallel",)),
        name="prenorm_proj",
    )(x, gain.reshape(1, D_MODEL), w_bf16)


def _swa_kernel(sink_ref, q_ref, kc_ref, kp_ref, vc_ref, vp_ref, qn_ref, kn_ref, o_ref, kout_ref):
    n = pl.program_id(1)
    blk = q_ref.shape[0]
    q = q_ref[...]
    kc, kp, vc, vp = kc_ref[...], kp_ref[...], vc_ref[...], vp_ref[...]
    qn, kn = qn_ref[...], kn_ref[...]
    row = lax.broadcasted_iota(jnp.int32, (blk, 2 * blk), 0)
    col = lax.broadcasted_iota(jnp.int32, (blk, 2 * blk), 1)
    rel = row + blk - col
    mask = (rel >= 0) & (rel <= WINDOW) & ((n > 0) | (col >= blk))
    outs, kouts = [], []
    for j in range(SWA_KV_HEADS):
        sl = slice(j * HEAD_DIM, (j + 1) * HEAD_DIM)
        kcj = _rms(kc[:, sl], kn)
        kpj = _rms(kp[:, sl], kn)
        kouts.append(kcj)
        kcat = jnp.concatenate([kpj, kcj], axis=0).astype(BF16)
        vcat = jnp.concatenate([vp[:, sl], vc[:, sl]], axis=0).astype(BF16)
        for g in range(SWA_GROUP):
            h = j * SWA_GROUP + g
            qh = _rms(q[:, h * HEAD_DIM:(h + 1) * HEAD_DIM], qn) * (HEAD_DIM ** -0.5)
            s = lax.dot_general(qh.astype(BF16), kcat, NT_DIMS, preferred_element_type=F32)
            s = jnp.where(mask, s, -jnp.inf)
            sink = sink_ref[h]
            m = jnp.maximum(jnp.max(s, axis=-1, keepdims=True), sink)
            p = jnp.exp(s - m)
            p = p / (jnp.sum(p, axis=-1, keepdims=True) + jnp.exp(sink - m))
            outs.append(jnp.dot(p.astype(BF16), vcat, preferred_element_type=F32))
    o_ref[...] = jnp.concatenate(outs, axis=1)
    kout_ref[...] = jnp.concatenate(kouts, axis=1)


def swa_attention(q, k, v, q_norm, k_norm, sinks):
    bsz, length = q.shape[0], q.shape[1]
    nb = length // WINDOW
    prev = lambda b, n: (b, jnp.maximum(n - 1, 0), 0)
    cur = lambda b, n: (b, n, 0)
    kv_blk = (None, WINDOW, SWA_KV)
    return pl.pallas_call(
        _swa_kernel,
        grid=(bsz, nb),
        in_specs=[pl.BlockSpec(memory_space=pltpu.SMEM),
                  pl.BlockSpec((None, WINDOW, SWA_Q), cur),
                  pl.BlockSpec(kv_blk, cur), pl.BlockSpec(kv_blk, prev),
                  pl.BlockSpec(kv_blk, cur), pl.BlockSpec(kv_blk, prev),
                  _const_spec((1, HEAD_DIM)), _const_spec((1, HEAD_DIM))],
        out_specs=[pl.BlockSpec((None, WINDOW, SWA_Q), cur), pl.BlockSpec(kv_blk, cur)],
        out_shape=[jax.ShapeDtypeStruct((bsz, length, SWA_Q), F32),
                   jax.ShapeDtypeStruct((bsz, length, SWA_KV), F32)],
        compiler_params=_cparams(("parallel", "arbitrary")),
        name="swa_attention",
    )(sinks, q, k, k, v, v, q_norm.reshape(1, HEAD_DIM), k_norm.reshape(1, HEAD_DIM))


def _chunk_conv(x_ref, xx_scr, w_ref, rows):
    xx_scr[SUBLANES:SUBLANES + rows, :] = x_ref[...]
    acc = None
    for tap in range(CONV_WIDTH):
        off = SUBLANES - (CONV_WIDTH - 1) + tap
        term = w_ref[tap:tap + 1, :] * xx_scr[off:off + rows, :]
        acc = term if acc is None else acc + term
    return acc


def _carry_conv_tail(xx_scr, rows):
    xx_scr[0:SUBLANES, :] = xx_scr[rows:rows + SUBLANES, :]


def _ssd_kernel(xbc_ref, z_ref, dt_ref, cw_ref, cb_ref, dtb_ref, alog_ref, dd_ref, nrm_ref,
                y_ref, hout_ref, xx_scr, h_scr):
    q_len = xbc_ref.shape[0]

    @pl.when(pl.program_id(1) == 0)
    def _():
        xx_scr[0:SUBLANES, :] = jnp.zeros((SUBLANES, xx_scr.shape[1]), F32)
        h_scr[...] = jnp.zeros_like(h_scr)

    act = _silu(_chunk_conv(xbc_ref, xx_scr, cw_ref, q_len) + cb_ref[...])
    _carry_conv_tail(xx_scr, q_len)
    xs = act[:, :SSM_INNER]
    bm = act[:, SSM_INNER:SSM_INNER + SSM_GROUPS * SSM_STATE]
    cm = act[:, SSM_INNER + SSM_GROUPS * SSM_STATE:]

    row = lax.broadcasted_iota(jnp.int32, (q_len, q_len), 0)
    col = lax.broadcasted_iota(jnp.int32, (q_len, q_len), 1)
    causal = row >= col
    dt = _softplus(dt_ref[...] + dtb_ref[...])
    da = dt * (-jnp.exp(alog_ref[...]))
    cum = jnp.dot(causal.astype(F32), da, preferred_element_type=F32, precision=lax.Precision.HIGHEST)
    cum_t = cum.T
    dt_t = dt.T
    e_cum = jnp.exp(cum)
    hpg = SSM_HEADS // SSM_GROUPS
    gw = hpg * SSM_HEAD_DIM
    ys = []
    for g in range(SSM_GROUPS):
        bm_g = bm[:, g * SSM_STATE:(g + 1) * SSM_STATE].astype(BF16)
        cm_g = cm[:, g * SSM_STATE:(g + 1) * SSM_STATE].astype(BF16)
        cb = lax.dot_general(cm_g, bm_g, NT_DIMS, preferred_element_type=F32)
        h_g = h_scr[g * gw:(g + 1) * gw, :]
        y_state = lax.dot_general(cm_g, h_g.astype(BF16), NT_DIMS, preferred_element_type=F32)
        xt_parts, dec_parts = [], []
        for hh in range(hpg):
            h = g * hpg + hh
            x_h = xs[:, h * SSM_HEAD_DIM:(h + 1) * SSM_HEAD_DIM]
            cum_c = cum[:, h:h + 1]
            seg = jnp.exp(jnp.where(causal, cum_c - cum_t[h:h + 1, :], -jnp.inf))
            wgt = cb * seg * dt_t[h:h + 1, :]
            y = jnp.dot(wgt.astype(BF16), x_h.astype(BF16), preferred_element_type=F32)
            y = y + y_state[:, hh * SSM_HEAD_DIM:(hh + 1) * SSM_HEAD_DIM] * e_cum[:, h:h + 1]
            ys.append(y + dd_ref[0, h] * x_h)
            c_last = cum[q_len - 1:q_len, h:h + 1]
            xt_parts.append(x_h * (jnp.exp(c_last - cum_c) * dt[:, h:h + 1]))
            dec_parts.append(jnp.broadcast_to(jnp.exp(c_last), (SSM_HEAD_DIM, SSM_STATE)))
        xt = jnp.concatenate(xt_parts, axis=1).astype(BF16)
        upd = lax.dot_general(xt, bm_g, TN_DIMS, preferred_element_type=F32)
        h_scr[g * gw:(g + 1) * gw, :] = h_g * jnp.concatenate(dec_parts, axis=0) + upd
    y_all = jnp.concatenate(ys, axis=1) * _silu(z_ref[...])
    nrm = nrm_ref[...]
    y_ref[...] = jnp.concatenate(
        [_rms(y_all[:, g * gw:(g + 1) * gw], nrm[:, g * gw:(g + 1) * gw]) for g in range(SSM_GROUPS)], axis=1)
    hout_ref[...] = h_scr[...]


def ssd_mixer(xbc, z, dt, conv_w, conv_b, dt_bias, a_log, d_skip, norm_g):
    bsz, length = xbc.shape[0], xbc.shape[1]
    nc = length // SSM_CHUNK
    cur = lambda b, c: (b, c, 0)
    per_b = lambda b, c: (b, 0, 0)
    pad8 = lambda v: jnp.pad(v.reshape(1, SSM_HEADS), ((0, 0), (0, LANES - SSM_HEADS)))
    return pl.pallas_call(
        _ssd_kernel,
        grid=(bsz, nc),
        in_specs=[pl.BlockSpec((None, SSM_CHUNK, SSM_CONV_CH), cur),
                  pl.BlockSpec((None, SSM_CHUNK, SSM_INNER), cur),
                  pl.BlockSpec((None, SSM_CHUNK, LANES), cur),
                  _const_spec((CONV_WIDTH, SSM_CONV_CH)), _const_spec((1, SSM_CONV_CH)),
                  _const_spec((1, LANES)), _const_spec((1, LANES)),
                  pl.BlockSpec(memory_space=pltpu.SMEM),
                  _const_spec((1, SSM_INNER))],
        out_specs=[pl.BlockSpec((None, SSM_CHUNK, SSM_INNER), cur),
                   pl.BlockSpec((None, SSM_INNER, SSM_STATE), per_b)],
        out_shape=[jax.ShapeDtypeStruct((bsz, length, SSM_INNER), F32),
                   jax.ShapeDtypeStruct((bsz, SSM_INNER, SSM_STATE), F32)],
        scratch_shapes=[pltpu.VMEM((SSM_CHUNK + SUBLANES, SSM_CONV_CH), F32),
                        pltpu.VMEM((SSM_INNER, SSM_STATE), F32)],
        compiler_params=_cparams(("parallel", "arbitrary")),
        name="ssd_mixer",
    )(xbc, z, dt, conv_w, conv_b.reshape(1, SSM_CONV_CH), pad8(dt_bias), pad8(a_log),
      d_skip.reshape(1, SSM_HEADS), norm_g.reshape(1, SSM_INNER))


def _unit_lower_inverses(lmats, row, col):
    c = lmats[0].shape[0]
    mm = lambda a, b: jnp.dot(a.astype(BF16), b.astype(BF16), preferred_element_type=F32)
    eye = (row == col).astype(F32)
    blk = SUBLANES
    same = (row // blk) == (col // blk)
    xs = [jnp.where(same, -l, 0.0) for l in lmats]
    invs = [eye + x for x in xs]
    p = blk
    while p > 2:
        xs = [mm(x, x) for x in xs]
        invs = [i + mm(i, x) for i, x in zip(invs, xs)]
        p //= 2
    while blk < c:
        outer = ((row // (2 * blk)) == (col // (2 * blk))) & ((row // blk) != (col // blk))
        ts = [mm(i, jnp.where(outer, l, 0.0)) for i, l in zip(invs, lmats)]
        invs = [i - mm(t, i) for i, t in zip(invs, ts)]
        blk *= 2
    return invs


def _dn_kernel(qkv_ref, z_ref, ba_ref, cw_ref, dtb_ref, alog_ref, nrm_ref, o_ref, sout_ref, xx_scr, s_scr):
    nseq, c_len = qkv_ref.shape[0], qkv_ref.shape[1]
    seqs = range(nseq)
    chains = [(s, h) for s in seqs for h in range(DN_HEADS)]
    ids = range(len(chains))

    @pl.when(pl.program_id(1) == 0)
    def _():
        xx_scr[:, 0:SUBLANES, :] = jnp.zeros((nseq, SUBLANES, xx_scr.shape[2]), F32)
        s_scr[...] = jnp.zeros_like(s_scr)

    row = lax.broadcasted_iota(jnp.int32, (c_len, c_len), 0)
    col = lax.broadcasted_iota(jnp.int32, (c_len, c_len), 1)
    incl = row >= col
    strict = row > col
    tri = incl.astype(F32)
    nrm = nrm_ref[...]
    bf = lambda t: t.astype(BF16)
    mm = lambda a, b: jnp.dot(bf(a), bf(b), preferred_element_type=F32)

    act, beta, cum, cum_t, e_cum, e_rest, e_last = [], [], [], [], [], [], []
    for s in seqs:
        act.append(_silu(_chunk_conv(qkv_ref.at[s], xx_scr.at[s], cw_ref, c_len)))
        _carry_conv_tail(xx_scr.at[s], c_len)
        ba = ba_ref[s]
        beta.append(_sigmoid(ba))
        gate = -jnp.exp(alog_ref[...]) * _softplus(ba + dtb_ref[...])
        cum_s = jnp.dot(tri, gate, preferred_element_type=F32, precision=lax.Precision.HIGHEST)
        c_last = cum_s[c_len - 1:c_len, :]
        cum.append(cum_s)
        cum_t.append(cum_s.T)
        e_cum.append(jnp.exp(cum_s))
        e_rest.append(jnp.exp(c_last - cum_s))
        e_last.append(jnp.exp(c_last))

    q, k, kb, rhs, decay = [], [], [], [], []
    for i in ids:
        s, h = chains[i]
        gl = DN_HEADS + h
        q_h = act[s][:, h * DN_DK:(h + 1) * DN_DK]
        k_h = act[s][:, DN_QK + h * DN_DK:DN_QK + (h + 1) * DN_DK]
        v_h = act[s][:, 2 * DN_QK + h * DN_DV:2 * DN_QK + (h + 1) * DN_DV]
        q.append(q_h * lax.rsqrt(jnp.sum(q_h * q_h, axis=-1, keepdims=True) + EPS) * (DN_DK ** -0.5))
        k.append(k_h * lax.rsqrt(jnp.sum(k_h * k_h, axis=-1, keepdims=True) + EPS))
        beta_c = beta[s][:, h:h + 1]
        kb.append(k[i] * beta_c)
        rhs.append(jnp.concatenate([v_h * beta_c, kb[i] * e_cum[s][:, gl:gl + 1]], axis=1))
        decay.append(jnp.exp(jnp.where(incl, cum[s][:, gl:gl + 1] - cum_t[s][gl:gl + 1, :], -jnp.inf)))
    kq = [lax.dot_general(bf(jnp.concatenate([kb[i], q[i]], axis=0)), bf(k[i]), NT_DIMS,
                          preferred_element_type=F32) for i in ids]
    lmat = [jnp.where(strict, kq[i][:c_len] * decay[i], 0.0) for i in ids]
    attn = [kq[i][c_len:] * decay[i] for i in ids]
    tinv = _unit_lower_inverses(lmat, row, col)
    uw = [mm(tinv[i], rhs[i]) for i in ids]
    s_old = [s_scr[s, h * DN_DK:(h + 1) * DN_DK, :] for s, h in chains]
    qe = [q[i] * e_cum[s][:, DN_HEADS + h:DN_HEADS + h + 1] for i, (s, h) in enumerate(chains)]
    wq = [mm(jnp.concatenate([uw[i][:, DN_DV:], qe[i]], axis=0), s_old[i]) for i in ids]
    v_new = [uw[i][:, :DN_DV] - wq[i][:c_len] for i in ids]
    o = [wq[i][c_len:] + mm(attn[i], v_new[i]) for i in ids]
    for i in ids:
        s, h = chains[i]
        gl = DN_HEADS + h
        s_scr[s, h * DN_DK:(h + 1) * DN_DK, :] = s_old[i] * e_last[s][:, gl:gl + 1] + lax.dot_general(
            bf(k[i] * e_rest[s][:, gl:gl + 1]), bf(v_new[i]), TN_DIMS, preferred_element_type=F32)
    for s in seqs:
        z = z_ref[s]
        o_ref[s] = jnp.concatenate(
            [_rms(o[s * DN_HEADS + h], nrm) * _silu(z[:, h * DN_DV:(h + 1) * DN_DV]) for h in range(DN_HEADS)],
            axis=1)
    sout_ref[...] = s_scr[...]


def dn_mixer(qkv, z, ba, conv_w, dt_bias, a_log, norm_g, nseq):
    bsz, length = qkv.shape[0], qkv.shape[1]
    nc = length // DN_CHUNK
    cur = lambda b, c: (b, c, 0)
    per_b = lambda b, c: (b, 0, 0)
    pad_a = lambda v: jnp.pad(v.reshape(1, DN_HEADS), ((0, 0), (DN_HEADS, LANES - 2 * DN_HEADS)))
    return pl.pallas_call(
        _dn_kernel,
        grid=(bsz // nseq, nc),
        in_specs=[pl.BlockSpec((nseq, DN_CHUNK, DN_CONV_CH), cur),
                  pl.BlockSpec((nseq, DN_CHUNK, DN_VW), cur),
                  pl.BlockSpec((nseq, DN_CHUNK, LANES), cur),
                  _const_spec((CONV_WIDTH, DN_CONV_CH)),
                  _const_spec((1, LANES)), _const_spec((1, LANES)), _const_spec((1, DN_DV))],
        out_specs=[pl.BlockSpec((nseq, DN_CHUNK, DN_VW), cur),
                   pl.BlockSpec((nseq, DN_QK, DN_DV), per_b)],
        out_shape=[jax.ShapeDtypeStruct((bsz, length, DN_VW), F32),
                   jax.ShapeDtypeStruct((bsz, DN_QK, DN_DV), F32)],
        scratch_shapes=[pltpu.VMEM((nseq, DN_CHUNK + SUBLANES, DN_CONV_CH), F32),
                        pltpu.VMEM((nseq, DN_QK, DN_DV), F32)],
        compiler_params=_cparams(("parallel", "arbitrary")),
        name="dn_mixer",
    )(qkv, z, ba, conv_w, pad_a(dt_bias), pad_a(a_log), norm_g.reshape(1, DN_DV))


X_SUB = D_MODEL // LANES


def _rows_to_slabs(x):
    parts = jnp.stack([x[:, s * LANES:(s + 1) * LANES] for s in range(X_SUB)], axis=0)
    return jnp.transpose(parts, (1, 0, 2))


def _slabs_to_rows(slab):
    parts = jnp.transpose(slab, (1, 0, 2))
    return jnp.concatenate([parts[s] for s in range(X_SUB)], axis=1)


def _resid_proj_kernel(x_ref, *refs, n_in):
    acc = x_ref[...]
    for a_ref, w_ref in zip(refs[:n_in], refs[n_in:2 * n_in]):
        acc = acc + jnp.dot(a_ref[...].astype(BF16), w_ref[...], preferred_element_type=F32)
    refs[2 * n_in][...] = acc


def resid_proj(x, acts, ws, tm):
    t = x.shape[0]
    n_in = len(acts)
    tok = lambda i: (i, 0)
    return pl.pallas_call(
        functools.partial(_resid_proj_kernel, n_in=n_in),
        grid=(t // tm,),
        in_specs=([pl.BlockSpec((tm, D_MODEL), tok)] + [pl.BlockSpec((tm, a.shape[1]), tok) for a in acts]
                  + [_const_spec(w.shape) for w in ws]),
        out_specs=pl.BlockSpec((tm, D_MODEL), tok),
        out_shape=jax.ShapeDtypeStruct((t, D_MODEL), F32),
        compiler_params=_cparams(("parallel",)),
        name="resid_proj",
    )(x, *acts, *ws)


def _split_bf16(v):
    hi = v.astype(BF16)
    return hi, (v - hi.astype(F32)).astype(BF16)


def _router_kernel(x_ref, g_ref, wr_hi_ref, wr_lo_ref, br_ref, xg_ref, gates_ref, grp_ref, rank_ref, cnt_ref,
                   carry_scr):
    tm = x_ref.shape[0]

    @pl.when(pl.program_id(0) == 0)
    def _():
        carry_scr[...] = jnp.zeros_like(carry_scr)

    x = x_ref[...]
    h_hi, h_lo = _split_bf16(_rms(x, g_ref[...]))
    w_hi, w_lo = wr_hi_ref[...], wr_lo_ref[...]
    logits = (jnp.dot(h_hi, w_hi, preferred_element_type=F32) + jnp.dot(h_lo, w_hi, preferred_element_type=F32)
              + jnp.dot(h_hi, w_lo, preferred_element_type=F32)) + br_ref[...]
    lt = logits.T
    sub = lax.broadcasted_iota(jnp.int32, (SUBLANES, tm), 0)
    neg = -jnp.inf
    glog = jnp.where(sub < N_GROUPS, lt[N_EXPERTS:N_EXPERTS + SUBLANES], neg)
    gmax = jnp.max(glog, axis=0, keepdims=True)
    g_p = 1.0 / jnp.sum(jnp.exp(glog - gmax), axis=0, keepdims=True)
    g_i = jnp.min(jnp.where(glog == gmax, sub, SUBLANES), axis=0, keepdims=True)
    sel = lt[0:EXPERTS_PER_GROUP]
    for g in range(1, N_GROUPS):
        sel = jnp.where(g_i == g, lt[g * EXPERTS_PER_GROUP:(g + 1) * EXPERTS_PER_GROUP], sel)
    m1 = jnp.max(sel, axis=0, keepdims=True)
    zsum = jnp.sum(jnp.exp(sel - m1), axis=0, keepdims=True)
    i1 = jnp.min(jnp.where(sel == m1, sub, SUBLANES), axis=0, keepdims=True)
    sel2 = jnp.where(sub == i1, neg, sel)
    m2 = jnp.max(sel2, axis=0, keepdims=True)
    i2 = jnp.min(jnp.where(sel2 == m2, sub, SUBLANES), axis=0, keepdims=True)
    p1 = 1.0 / zsum
    p2 = jnp.exp(m2 - m1) / zsum
    gate1 = g_p * p1 / (p1 + p2)
    gate2 = g_p * p2 / (p1 + p2)
    gates = jnp.where(sub == i1, gate1, 0.0) + jnp.where(sub == i2, gate2, 0.0)

    onehot = (sub == g_i).astype(F32)
    before = (lax.broadcasted_iota(jnp.int32, (tm, tm), 0) < lax.broadcasted_iota(jnp.int32, (tm, tm), 1))
    rank_in_tile = jnp.dot(onehot.astype(BF16), before.astype(BF16), preferred_element_type=F32)
    carry = carry_scr[...]
    rank = jnp.sum(onehot * (rank_in_tile + carry[:, 0:1]), axis=0, keepdims=True)
    grp_ref[...] = g_i
    rank_ref[...] = rank.astype(jnp.int32)
    new_carry = carry + jnp.sum(onehot, axis=1, keepdims=True)
    carry_scr[...] = new_carry
    cnt_ref[...] = new_carry

    gates_ref[...] = jnp.concatenate([gates, jnp.zeros((LANES - SUBLANES, tm), F32)], axis=0).T
    xg_ref[...] = _rows_to_slabs(x)


def moe_router(x, gain, w_router_pad, b_router_pad, tm):
    t = x.shape[0]
    w_hi = w_router_pad.astype(BF16)
    w_lo = (w_router_pad - w_hi.astype(F32)).astype(BF16)
    return pl.pallas_call(
        _router_kernel,
        grid=(t // tm,),
        in_specs=[pl.BlockSpec((tm, D_MODEL), lambda i: (i, 0)), _const_spec((1, D_MODEL)),
                  _const_spec((D_MODEL, LANES)), _const_spec((D_MODEL, LANES)), _const_spec((1, LANES))],
        out_specs=[pl.BlockSpec((tm, X_SUB, LANES), lambda i: (i, 0, 0)),
                   pl.BlockSpec((tm, LANES), lambda i: (i, 0)),
                   pl.BlockSpec((None, 1, tm), lambda i: (i, 0, 0)),
                   pl.BlockSpec((None, 1, tm), lambda i: (i, 0, 0)),
                   _const_spec((SUBLANES, LANES))],
        out_shape=[jax.ShapeDtypeStruct((t, X_SUB, LANES), F32),
                   jax.ShapeDtypeStruct((t, LANES), F32),
                   jax.ShapeDtypeStruct((t // tm, 1, tm), jnp.int32),
                   jax.ShapeDtypeStruct((t // tm, 1, tm), jnp.int32),
                   jax.ShapeDtypeStruct((SUBLANES, LANES), F32)],
        scratch_shapes=[pltpu.VMEM((SUBLANES, LANES), F32)],
        compiler_params=_cparams(("arbitrary",)),
        name="moe_router",
    )(x, gain.reshape(1, D_MODEL), w_hi, w_lo, b_router_pad)


def _issue_row_copies(n_rows, make_copy):
    def trip(i, carry):
        for u in range(ROW_DMA_UNROLL):
            make_copy(i * ROW_DMA_UNROLL + u).start(priority=u % DMA_PRIORITIES)
        return carry

    lax.fori_loop(0, n_rows // ROW_DMA_UNROLL, trip, 0)


def _dispatch_kernel(pos_ref, pad_lo_ref, pad_hi_ref, xg_ref, gates_ref, xs_hbm, gs_hbm, zero_scr,
                     sem, gsem, zsem, zgsem):
    tm = xg_ref.shape[0]
    gate_row = lambda ref, r: ref.at[pl.ds(r, 1)]
    _issue_row_copies(tm, lambda r: pltpu.make_async_copy(xg_ref.at[r], xs_hbm.at[pos_ref[0, r]], sem))
    _issue_row_copies(tm, lambda r: pltpu.make_async_copy(gate_row(gates_ref, r),
                                                          gate_row(gs_hbm, pos_ref[0, r]), gsem))
    pltpu.make_async_copy(xg_ref, xs_hbm.at[pl.ds(0, tm)], sem).wait()
    pltpu.make_async_copy(gates_ref, gs_hbm.at[pl.ds(0, tm)], gsem).wait()

    @pl.when(pl.program_id(0) == pl.num_programs(0) - 1)
    def _():
        zero_scr[...] = jnp.zeros_like(zero_scr)
        zero_gate = gate_row(zero_scr, 0)
        for k in range(pad_lo_ref.shape[0]):
            def fill(j, carry):
                pltpu.make_async_copy(zero_scr, xs_hbm.at[j], zsem).start()
                pltpu.make_async_copy(zero_gate, gate_row(gs_hbm, j), zgsem).start()
                return carry

            def drain(j, carry):
                pltpu.make_async_copy(zero_scr, xs_hbm.at[j], zsem).wait()
                pltpu.make_async_copy(zero_gate, gate_row(gs_hbm, j), zgsem).wait()
                return carry

            lax.fori_loop(pad_lo_ref[k], pad_hi_ref[k], fill, 0)
            lax.fori_loop(pad_lo_ref[k], pad_hi_ref[k], drain, 0)


def moe_dispatch(pos, pad_lo, pad_hi, xg, gates, tm, n_slots):
    t = xg.shape[0]
    dma = pltpu.SemaphoreType.DMA
    return pl.pallas_call(
        _dispatch_kernel,
        grid=(t // tm,),
        in_specs=[pl.BlockSpec((None, 1, tm), lambda i: (i, 0, 0), memory_space=pltpu.SMEM),
                  pl.BlockSpec(memory_space=pltpu.SMEM), pl.BlockSpec(memory_space=pltpu.SMEM),
                  pl.BlockSpec((tm, X_SUB, LANES), lambda i: (i, 0, 0)),
                  pl.BlockSpec((tm, LANES), lambda i: (i, 0))],
        out_specs=[pl.BlockSpec(memory_space=pl.ANY), pl.BlockSpec(memory_space=pl.ANY)],
        out_shape=[jax.ShapeDtypeStruct((n_slots, X_SUB, LANES), F32),
                   jax.ShapeDtypeStruct((n_slots, LANES), F32)],
        scratch_shapes=[pltpu.VMEM((X_SUB, LANES), F32), dma, dma, dma, dma],
        compiler_params=_cparams(("arbitrary",), has_side_effects=True),
        name="moe_dispatch",
    )(pos, pad_lo, pad_hi, xg, gates)


def _collect_kernel(pos_ref, os_hbm, out_ref, buf, sem):
    tm = out_ref.shape[0]
    _issue_row_copies(tm, lambda r: pltpu.make_async_copy(os_hbm.at[pos_ref[0, r]], buf.at[r], sem))
    pltpu.make_async_copy(os_hbm.at[pl.ds(0, tm)], buf, sem).wait()
    out_ref[...] = _slabs_to_rows(buf[...])


def moe_collect(pos, os_sorted, t, tm):
    return pl.pallas_call(
        _collect_kernel,
        grid=(t // tm,),
        in_specs=[pl.BlockSpec((None, 1, tm), lambda i: (i, 0, 0), memory_space=pltpu.SMEM),
                  pl.BlockSpec(memory_space=pl.ANY)],
        out_specs=pl.BlockSpec((tm, D_MODEL), lambda i: (i, 0)),
        out_shape=jax.ShapeDtypeStruct((t, D_MODEL), F32),
        scratch_shapes=[pltpu.VMEM((tm, X_SUB, LANES), F32), pltpu.SemaphoreType.DMA],
        compiler_params=_cparams(("arbitrary",), has_side_effects=True),
        name="moe_collect",
    )(pos, os_sorted)


def _group_experts_kernel(grp_ref, xs_ref, gates_ref, g_ref, wg_ref, wu_ref, wd_ref, o_ref):
    del grp_ref
    x = _slabs_to_rows(xs_ref[...])
    gates = gates_ref[...]
    h = _rms(x, g_ref[...]).astype(BF16)
    acc = x
    for e in range(EXPERTS_PER_GROUP):
        gate = jnp.dot(h, wg_ref[e], preferred_element_type=F32)
        up = jnp.dot(h, wu_ref[e], preferred_element_type=F32)
        act = (_silu(gate) * up).astype(BF16)
        acc = acc + gates[:, e:e + 1] * jnp.dot(act, wd_ref[e], preferred_element_type=F32)
    o_ref[...] = _rows_to_slabs(acc)


def moe_group_experts(tile_grp, xs, gs, gain, wg, wu, wd, tile):
    n_steps = xs.shape[0] // tile
    rows = lambda i, grp: (i, 0, 0)
    wts = lambda i, grp: (grp[i], 0, 0, 0)
    return pl.pallas_call(
        _group_experts_kernel,
        grid_spec=pltpu.PrefetchScalarGridSpec(
            num_scalar_prefetch=1,
            grid=(n_steps,),
            in_specs=[pl.BlockSpec((tile, X_SUB, LANES), rows),
                      pl.BlockSpec((tile, LANES), lambda i, grp: (i, 0)),
                      pl.BlockSpec((1, D_MODEL), lambda i, grp: (0, 0)),
                      pl.BlockSpec((None, EXPERTS_PER_GROUP, D_MODEL, EXPERT_FF), wts),
                      pl.BlockSpec((None, EXPERTS_PER_GROUP, D_MODEL, EXPERT_FF), wts),
                      pl.BlockSpec((None, EXPERTS_PER_GROUP, EXPERT_FF, D_MODEL), wts)],
            out_specs=pl.BlockSpec((tile, X_SUB, LANES), rows)),
        out_shape=jax.ShapeDtypeStruct((xs.shape[0], X_SUB, LANES), F32),
        compiler_params=_cparams(("arbitrary",)),
        name="moe_group_experts",
    )(tile_grp, xs, gs, gain.reshape(1, D_MODEL), wg, wu, wd)


def mix_out_and_moe(x, acts, ws, gain, w_route, b_route, wg, wu, wd):
    t = x.shape[0]
    tile = MOE_TILE if t % MOE_TILE == 0 else t
    tm = MOE_TOKEN_TILE if t % MOE_TOKEN_TILE == 0 else tile
    tmove = MOE_MOVE_TILE if t % MOE_MOVE_TILE == 0 else tm
    n_tiles = t // tile + N_GROUPS
    x = resid_proj(x, acts, ws, tm)
    xg, gates, grp, rank, cnt = moe_router(x, gain, w_route, b_route, tm)
    counts = cnt[:N_GROUPS, 0].astype(jnp.int32)
    tile_ends = jnp.cumsum((counts + tile - 1) // tile)
    offs = (tile_ends - (counts + tile - 1) // tile) * tile
    pos = rank + sum(jnp.where(grp == g, offs[g], 0) for g in range(N_GROUPS))
    pad_lo = jnp.concatenate([offs + counts, tile_ends[-1:] * tile])
    pad_hi = jnp.concatenate([tile_ends * tile, jnp.full((1,), n_tiles * tile, jnp.int32)])
    tile_grp = jnp.minimum(jnp.sum((jnp.arange(n_tiles, dtype=jnp.int32)[:, None] >= tile_ends[None, :])
                                   .astype(jnp.int32), axis=1), N_GROUPS - 1)
    pos = pos.reshape(t // tmove, 1, tmove)
    xs, gs = moe_dispatch(pos, pad_lo, pad_hi, xg, gates, tmove, n_tiles * tile)
    os_sorted = moe_group_experts(tile_grp, xs, gs, gain, wg, wu, wd, tile)
    return moe_collect(pos, os_sorted, t, tmove)


BATCH_NT = (((2,), (2,)), ((0,), (0,)))
BATCH_NN = (((2,), (1,)), ((0,), (0,)))


def _token_major(parts):
    return jnp.transpose(jnp.stack(parts, axis=0), (1, 0, 2))


def _pad_rows(parts, tb, width):
    return parts + [jnp.zeros((tb, width), F32)] * (SUBLANES - len(parts))


def _spread(v, first, n, width):
    tb = v.shape[0]
    return jnp.concatenate([jnp.broadcast_to(v[:, first + i:first + i + 1], (tb, width)) for i in range(n)], axis=1)


def _columns(rows):
    tb, m = rows.shape
    return jnp.concatenate([rows, jnp.zeros((LANES - tb, m), F32)], axis=0).T


def _one_step_conv(x_ref, buf_ref, w_ref):
    return (w_ref[0:1, :] * buf_ref[:, 0, :] + w_ref[1:2, :] * buf_ref[:, 1, :] + w_ref[2:3, :] * buf_ref[:, 2, :]
            + w_ref[3:4, :] * x_ref[...])


def _swa_decode_kernel(sink_ref, q_ref, k_ref, v_ref, kc_ref, vc_ref, qn_ref, kn_ref, o_ref, kout_ref):
    tb = q_ref.shape[0]
    lo = lax.broadcasted_iota(jnp.int32, (tb, LANES), 1) < HEAD_DIM

    def pair_rms(t, gain):
        sq = t * t
        s_lo = jnp.sum(jnp.where(lo, sq, 0.0), axis=-1, keepdims=True)
        s_hi = jnp.sum(jnp.where(lo, 0.0, sq), axis=-1, keepdims=True)
        return t * lax.rsqrt(jnp.where(lo, s_lo, s_hi) * (1.0 / HEAD_DIM) + EPS) * gain

    qn2 = jnp.concatenate([qn_ref[...], qn_ref[...]], axis=1)
    kn2 = jnp.concatenate([kn_ref[...], kn_ref[...]], axis=1)
    k_new = pair_rms(k_ref[...], kn2)
    kout_ref[...] = k_new
    v_new = v_ref[...]
    rows = []
    for h in range(SWA_HEADS):
        t = pair_rms(q_ref[:, (h // 2) * LANES:(h // 2 + 1) * LANES], qn2) * (HEAD_DIM ** -0.5)
        j = h // SWA_GROUP
        if h % 2 != j:
            t = pltpu.roll(t, HEAD_DIM, axis=1)
        rows.append(jnp.where(lo if j == 0 else jnp.logical_not(lo), t, 0.0))
    q8 = _token_major(rows)
    s = lax.dot_general(q8.astype(BF16), kc_ref[...].astype(BF16), BATCH_NT, preferred_element_type=F32)
    s_new = jnp.sum(q8 * k_new[:, None, :], axis=-1, keepdims=True)
    sub = lax.broadcasted_iota(jnp.int32, (1, SWA_HEADS, 1), 1)
    sink = jnp.zeros((1, SWA_HEADS, 1), F32)
    for h in range(SWA_HEADS):
        sink = jnp.where(sub == h, sink_ref[h], sink)
    m = jnp.maximum(jnp.maximum(jnp.max(s, axis=-1, keepdims=True), s_new), sink)
    p = jnp.exp(s - m)
    p_new = jnp.exp(s_new - m)
    den = jnp.sum(p, axis=-1, keepdims=True) + p_new + jnp.exp(sink - m)
    o8 = lax.dot_general((p / den).astype(BF16), vc_ref[...].astype(BF16), BATCH_NN,
                         preferred_element_type=F32) + (p_new / den) * v_new[:, None, :]
    o_h = jnp.transpose(o8, (1, 0, 2))
    tiles = []
    for t in range(SWA_HEADS // 2):
        halves = []
        for h in (2 * t, 2 * t + 1):
            piece = o_h[h]
            if h % 2 != h // SWA_GROUP:
                piece = pltpu.roll(piece, HEAD_DIM, axis=1)
            halves.append(piece)
        tiles.append(jnp.where(lo, halves[0], halves[1]))
    o_ref[...] = jnp.concatenate(tiles, axis=1)


def swa_decode(q, k, v, k_cache, v_cache, q_norm, k_norm, sinks):
    bsz = q.shape[0]
    tok = lambda i: (i, 0)
    tok3 = lambda i: (i, 0, 0)
    return pl.pallas_call(
        _swa_decode_kernel,
        grid=(bsz // DEC_TB,),
        in_specs=[pl.BlockSpec(memory_space=pltpu.SMEM),
                  pl.BlockSpec((DEC_TB, SWA_Q), tok), pl.BlockSpec((DEC_TB, SWA_KV), tok),
                  pl.BlockSpec((DEC_TB, SWA_KV), tok),
                  pl.BlockSpec((DEC_TB, WINDOW, SWA_KV), tok3), pl.BlockSpec((DEC_TB, WINDOW, SWA_KV), tok3),
                  _const_spec((1, HEAD_DIM)), _const_spec((1, HEAD_DIM))],
        out_specs=[pl.BlockSpec((DEC_TB, SWA_Q), tok), pl.BlockSpec((DEC_TB, SWA_KV), tok)],
        out_shape=[jax.ShapeDtypeStruct((bsz, SWA_Q), F32), jax.ShapeDtypeStruct((bsz, SWA_KV), F32)],
        compiler_params=_cparams(("parallel",)),
        name="swa_decode",
    )(sinks, q, k, v, k_cache, v_cache, q_norm.reshape(1, HEAD_DIM), k_norm.reshape(1, HEAD_DIM))


def _ssd_decode_kernel(x_ref, buf_ref, z_ref, dt_ref, cw_ref, cb_ref, dtb_ref, alog_ref, dd_ref,
                       nrm_ref, h0_ref, y_ref, hout_ref):
    tb = x_ref.shape[0]
    act = _silu(_one_step_conv(x_ref, buf_ref, cw_ref) + cb_ref[...])
    xs = act[:, :SSM_INNER]
    bm = act[:, SSM_INNER:SSM_INNER + SSM_GROUPS * SSM_STATE]
    cm = act[:, SSM_INNER + SSM_GROUPS * SSM_STATE:]
    dt = _softplus(dt_ref[...] + dtb_ref[...])
    dec = jnp.exp(dt * (-jnp.exp(alog_ref[...])))
    dt_x = _spread(dt, 0, SSM_HEADS, SSM_HEAD_DIM)
    dec_x = _spread(dec, 0, SSM_HEADS, SSM_HEAD_DIM)
    gw = SSM_INNER // SSM_GROUPS
    grp = lambda t, g: t[:, g * SSM_STATE:(g + 1) * SSM_STATE]
    cb_x = jnp.concatenate([jnp.broadcast_to(jnp.sum(grp(cm, g) * grp(bm, g), axis=-1, keepdims=True), (tb, gw))
                            for g in range(SSM_GROUPS)], axis=1)
    h0 = h0_ref[...]
    c8 = _token_major(_pad_rows([grp(cm, g) for g in range(SSM_GROUPS)], tb, SSM_STATE))
    ys = jnp.transpose(lax.dot_general(c8.astype(BF16), h0.astype(BF16), BATCH_NT, preferred_element_type=F32),
                       (1, 0, 2))
    y_state = jnp.concatenate([ys[g][:, g * gw:(g + 1) * gw] for g in range(SSM_GROUPS)], axis=1)
    y = cb_x * dt_x * xs + y_state * dec_x + dd_ref[...] * xs
    y = y * _silu(z_ref[...])
    nrm = nrm_ref[...]
    y_ref[...] = jnp.concatenate(
        [_rms(y[:, g * gw:(g + 1) * gw], nrm[:, g * gw:(g + 1) * gw]) for g in range(SSM_GROUPS)], axis=1)
    u_cols = _columns(dt_x * xs)
    hpg = SSM_HEADS // SSM_GROUPS
    for b in range(tb):
        for g in range(SSM_GROUPS):
            rows = slice(g * gw, (g + 1) * gw)
            decay = jnp.concatenate([jnp.broadcast_to(dec[b:b + 1, h:h + 1], (SSM_HEAD_DIM, SSM_STATE))
                                     for h in range(g * hpg, (g + 1) * hpg)], axis=0)
            hout_ref[b, rows, :] = h0[b, rows, :] * decay + u_cols[rows, b:b + 1] * grp(bm, g)[b:b + 1, :]


def ssd_decode(xbc, bufs, z, dt, conv_w, conv_b, dt_bias, a_log, d_skip, norm_g, h0):
    bsz = xbc.shape[0]
    tok = lambda i: (i, 0)
    tok3 = lambda i: (i, 0, 0)
    pad8 = lambda v: jnp.pad(v.reshape(1, SSM_HEADS), ((0, 0), (0, LANES - SSM_HEADS)))
    row = lambda w: pl.BlockSpec((DEC_TB, w), tok)
    return pl.pallas_call(
        _ssd_decode_kernel,
        grid=(bsz // DEC_TB,),
        in_specs=[row(SSM_CONV_CH), pl.BlockSpec((DEC_TB, CONV_WIDTH - 1, SSM_CONV_CH), tok3),
                  row(SSM_INNER), row(LANES),
                  _const_spec((CONV_WIDTH, SSM_CONV_CH)), _const_spec((1, SSM_CONV_CH)),
                  _const_spec((1, LANES)), _const_spec((1, LANES)), _const_spec((1, SSM_INNER)),
                  _const_spec((1, SSM_INNER)),
                  pl.BlockSpec((DEC_TB, SSM_INNER, SSM_STATE), tok3)],
        out_specs=[row(SSM_INNER), pl.BlockSpec((DEC_TB, SSM_INNER, SSM_STATE), tok3)],
        out_shape=[jax.ShapeDtypeStruct((bsz, SSM_INNER), F32),
                   jax.ShapeDtypeStruct((bsz, SSM_INNER, SSM_STATE), F32)],
        compiler_params=_cparams(("parallel",)),
        name="ssd_decode",
    )(xbc, bufs, z, dt, conv_w, conv_b.reshape(1, SSM_CONV_CH), pad8(dt_bias), pad8(a_log),
      jnp.repeat(d_skip, SSM_HEAD_DIM).reshape(1, SSM_INNER), norm_g.reshape(1, SSM_INNER), h0)


def _dn_decode_kernel(x_ref, buf_ref, z_ref, ba_ref, cw_ref, dtb_ref, alog_ref, nrm_ref, s0_ref,
                      o_ref, sout_ref):
    tb = x_ref.shape[0]
    act = _silu(_one_step_conv(x_ref, buf_ref, cw_ref))
    ba = ba_ref[...]
    beta = _sigmoid(ba)
    eg = jnp.exp(-jnp.exp(alog_ref[...]) * _softplus(ba + dtb_ref[...]))
    z = z_ref[...]
    nrm = nrm_ref[...]
    outs = []
    for h in range(DN_HEADS):
        q_h = act[:, h * DN_DK:(h + 1) * DN_DK]
        k_h = act[:, DN_QK + h * DN_DK:DN_QK + (h + 1) * DN_DK]
        v_h = act[:, 2 * DN_QK + h * DN_DV:2 * DN_QK + (h + 1) * DN_DV]
        q_h = q_h * lax.rsqrt(jnp.sum(q_h * q_h, axis=-1, keepdims=True) + EPS) * (DN_DK ** -0.5)
        k_h = k_h * lax.rsqrt(jnp.sum(k_h * k_h, axis=-1, keepdims=True) + EPS)
        beta_c = beta[:, h:h + 1]
        eg_c = eg[:, DN_HEADS + h:DN_HEADS + h + 1]
        s_h = s0_ref[:, h * DN_DK:(h + 1) * DN_DK, :]
        kq8 = _token_major(_pad_rows([k_h, q_h], tb, DN_DK))
        r = jnp.transpose(lax.dot_general(kq8.astype(BF16), s_h.astype(BF16), BATCH_NN,
                                          preferred_element_type=F32), (1, 0, 2))
        v_new = beta_c * v_h - (beta_c * eg_c) * r[0]
        o_h = eg_c * r[1] + jnp.sum(q_h * k_h, axis=-1, keepdims=True) * v_new
        outs.append(_rms(o_h, nrm) * _silu(z[:, h * DN_DV:(h + 1) * DN_DV]))
        k_cols = _columns(k_h)
        for b in range(tb):
            sout_ref[b, h * DN_DK:(h + 1) * DN_DK, :] = (
                s_h[b] * jnp.broadcast_to(eg_c[b:b + 1, :], (DN_DK, DN_DV)) + k_cols[:, b:b + 1] * v_new[b:b + 1, :])
    o_ref[...] = jnp.concatenate(outs, axis=1)


def dn_decode(qkv, bufs, z, ba, conv_w, dt_bias, a_log, norm_g, s0):
    bsz = qkv.shape[0]
    tok = lambda i: (i, 0)
    tok3 = lambda i: (i, 0, 0)
    pad_a = lambda v: jnp.pad(v.reshape(1, DN_HEADS), ((0, 0), (DN_HEADS, LANES - 2 * DN_HEADS)))
    row = lambda w: pl.BlockSpec((DEC_TB, w), tok)
    return pl.pallas_call(
        _dn_decode_kernel,
        grid=(bsz // DEC_TB,),
        in_specs=[row(DN_CONV_CH), pl.BlockSpec((DEC_TB, CONV_WIDTH - 1, DN_CONV_CH), tok3),
                  row(DN_VW), row(LANES),
                  _const_spec((CONV_WIDTH, DN_CONV_CH)),
                  _const_spec((1, LANES)), _const_spec((1, LANES)), _const_spec((1, DN_DV)),
                  pl.BlockSpec((DEC_TB, DN_QK, DN_DV), tok3)],
        out_specs=[row(DN_VW), pl.BlockSpec((DEC_TB, DN_QK, DN_DV), tok3)],
        out_shape=[jax.ShapeDtypeStruct((bsz, DN_VW), F32), jax.ShapeDtypeStruct((bsz, DN_QK, DN_DV), F32)],
        compiler_params=_cparams(("parallel",)),
        name="dn_decode",
    )(qkv, bufs, z, ba, conv_w, pad_a(dt_bias), pad_a(a_log), norm_g.reshape(1, DN_DV), s0)


def _pad_cols(w, n_pad):
    return jnp.pad(w, ((0, 0), (0, n_pad - w.shape[1])))


EVEN_SEGS = ((0, 512), (512, 640), (640, 768), (768, 1280), (1280, 2304), (2304, 2432))
ODD_SEGS = ((0, 3072), (3072, 4096), (4096, 4224))


def _trunk(x_seq, p, states):
    prompt = states is None
    bsz, length = x_seq.shape[0], x_seq.shape[1]
    t = bsz * length
    tm = 512 if t % 512 == 0 else t
    x = x_seq.reshape(t, D_MODEL)
    seq = lambda u: u.reshape(bsz, length, u.shape[-1])

    q, k, v, z, xbc, dt = prenorm_proj(x, p['ln_mix'][0], p['w_in_even'], EVEN_SEGS, tm)
    if prompt:
        v3, xbc3 = seq(v), seq(xbc)
        att, k_normed = swa_attention(seq(q), seq(k), v3, p['q_norm'], p['k_norm'], p['attn_sinks'])
        new_k = k_normed[:, -WINDOW:].reshape(bsz, WINDOW, SWA_KV_HEADS, HEAD_DIM)
        new_v = v3[:, -WINDOW:].reshape(bsz, WINDOW, SWA_KV_HEADS, HEAD_DIM)
        new_ssm_conv = xbc3[:, -(CONV_WIDTH - 1):]
        y_ssm, new_h = ssd_mixer(xbc3, seq(z), seq(dt), p['ssm_conv_w'], p['ssm_conv_b'], p['ssm_dt_bias'],
                                 p['ssm_A_log'], p['ssm_D'], p['ssm_norm'])
    else:
        k_win, v_win, ssm_h, ssm_conv, dn_s, dn_conv = states
        kb = k_win.reshape(bsz, WINDOW, SWA_KV)
        vb = v_win.reshape(bsz, WINDOW, SWA_KV)
        att, k_normed = swa_decode(q, k, v, kb, vb, p['q_norm'], p['k_norm'], p['attn_sinks'])
        new_k = jnp.concatenate([kb[:, 1:], k_normed[:, None]], axis=1).reshape(bsz, WINDOW, SWA_KV_HEADS, HEAD_DIM)
        new_v = jnp.concatenate([vb[:, 1:], v[:, None]], axis=1).reshape(bsz, WINDOW, SWA_KV_HEADS, HEAD_DIM)
        new_ssm_conv = jnp.concatenate([ssm_conv[:, 1:], xbc[:, None, :]], axis=1)
        y_ssm, new_h = ssd_decode(xbc, ssm_conv, z, dt, p['ssm_conv_w'],
                                  p['ssm_conv_b'], p['ssm_dt_bias'], p['ssm_A_log'], p['ssm_D'], p['ssm_norm'],
                                  ssm_h.reshape(bsz, SSM_INNER, SSM_STATE))
    x = mix_out_and_moe(x, [att.reshape(t, SWA_Q), y_ssm.reshape(t, SSM_INNER)],
                        [p['w_out_even'][:SWA_Q], p['w_out_even'][SWA_Q:]],
                        p['ln_ffn'][0], p['w_route'][0], p['b_route'][0],
                        p['moe_w_gate'][0], p['moe_w_up'][0], p['moe_w_down'][0])

    qkv, zz, ba = prenorm_proj(x, p['ln_mix'][1], p['w_in_odd'], ODD_SEGS, tm)
    if prompt:
        qkv3 = seq(qkv)
        new_dn_conv = qkv3[:, -(CONV_WIDTH - 1):]
        o_dn, new_s = dn_mixer(qkv3, seq(zz), seq(ba), p['dn_conv_w'], p['dn_dt_bias'], p['dn_A_log'],
                               p['dn_norm'], DN_SEQS_PER_STEP)
    else:
        new_dn_conv = jnp.concatenate([dn_conv[:, 1:], qkv[:, None, :]], axis=1)
        o_dn, new_s = dn_decode(qkv, dn_conv, zz, ba, p['dn_conv_w'],
                                p['dn_dt_bias'], p['dn_A_log'], p['dn_norm'], dn_s.reshape(bsz, DN_QK, DN_DV))
    x = mix_out_and_moe(x, [o_dn.reshape(t, DN_VW)], [p['w_out_odd']],
                        p['ln_ffn'][1], p['w_route'][1], p['b_route'][1],
                        p['moe_w_gate'][1], p['moe_w_up'][1], p['moe_w_down'][1])

    return (x.reshape(bsz, length, D_MODEL), new_k[None], new_v[None],
            new_h.reshape(1, bsz, SSM_HEADS, SSM_HEAD_DIM, SSM_STATE), new_ssm_conv[None],
            new_s.reshape(1, bsz, DN_HEADS, DN_DK, DN_DV), new_dn_conv[None])


def kernel(x_prompt, x_sample, cache_k_win, cache_v_win, state_ssm, state_ssm_conv, state_dn, state_dn_conv,
           ln_mix, ln_ffn, w_in_even, q_norm, k_norm, attn_sinks, ssm_conv_w, ssm_conv_b, ssm_dt_bias,
           ssm_A_log, ssm_D, ssm_norm, w_out_even, w_in_odd, dn_conv_w, dn_dt_bias, dn_A_log, dn_norm,
           w_out_odd, moe_w_group, moe_b_group, moe_w_router, moe_b_router, moe_w_gate, moe_w_up, moe_w_down):
    w_route = _pad_cols(jnp.concatenate([moe_w_router, moe_w_group], axis=-1).reshape(-1, N_EXPERTS + N_GROUPS),
                        LANES).reshape(2, D_MODEL, LANES)
    b_route = _pad_cols(jnp.concatenate([moe_b_router, moe_b_group], axis=-1), LANES).reshape(2, 1, LANES)
    p = {
        'ln_mix': ln_mix, 'ln_ffn': ln_ffn,
        'w_in_even': _pad_cols(w_in_even[0], EVEN_SEGS[-1][1]).astype(BF16),
        'q_norm': q_norm[0], 'k_norm': k_norm[0], 'attn_sinks': attn_sinks[0],
        'ssm_conv_w': ssm_conv_w[0], 'ssm_conv_b': ssm_conv_b[0], 'ssm_dt_bias': ssm_dt_bias[0],
        'ssm_A_log': ssm_A_log[0], 'ssm_D': ssm_D[0], 'ssm_norm': ssm_norm[0],
        'w_out_even': w_out_even[0].astype(BF16),
        'w_in_odd': _pad_cols(w_in_odd[0], ODD_SEGS[-1][1]).astype(BF16),
        'dn_conv_w': dn_conv_w[0], 'dn_dt_bias': dn_dt_bias[0], 'dn_A_log': dn_A_log[0], 'dn_norm': dn_norm[0],
        'w_out_odd': w_out_odd[0].astype(BF16),
        'w_route': w_route, 'b_route': b_route,
        'moe_w_gate': moe_w_gate.astype(BF16).reshape(2, N_GROUPS, EXPERTS_PER_GROUP, D_MODEL, EXPERT_FF),
        'moe_w_up': moe_w_up.astype(BF16).reshape(2, N_GROUPS, EXPERTS_PER_GROUP, D_MODEL, EXPERT_FF),
        'moe_w_down': moe_w_down.astype(BF16).reshape(2, N_GROUPS, EXPERTS_PER_GROUP, EXPERT_FF, D_MODEL),
    }
    y_p, kp, vp, sp, scp, dnp, dncp = _trunk(x_prompt, p, None)
    sample_states = (cache_k_win[0], cache_v_win[0], state_ssm[0], state_ssm_conv[0], state_dn[0],
                     state_dn_conv[0])
    y_s, ks, vs, ss, scs, dns, dncs = _trunk(x_sample, p, sample_states)
    return (y_p, y_s, kp, ks, vp, vs, sp, ss, scp, scs, dnp, dns, dncp, dncs)
```

```python
import functools

import jax
import jax.numpy as jnp
from jax import lax
from jax.experimental import pallas as pl
from jax.experimental.pallas import tpu as pltpu

F32 = jnp.float32
BF16 = jnp.bfloat16
EPS = 1e-6

D_MODEL = 1024
SWA_HEADS = 8
SWA_KV_HEADS = 2
SWA_GROUP = SWA_HEADS // SWA_KV_HEADS
HEAD_DIM = 64
WINDOW = 128
SWA_Q = SWA_HEADS * HEAD_DIM
SWA_KV = SWA_KV_HEADS * HEAD_DIM
SSM_HEADS = 8
SSM_HEAD_DIM = 64
SSM_GROUPS = 2
SSM_STATE = 128
SSM_INNER = SSM_HEADS * SSM_HEAD_DIM
SSM_CHUNK = 128
SSM_CONV_CH = SSM_INNER + 2 * SSM_GROUPS * SSM_STATE
CONV_WIDTH = 4
DN_HEADS = 8
DN_DK = 128
DN_DV = 128
DN_CHUNK = 64
DN_QK = DN_HEADS * DN_DK
DN_VW = DN_HEADS * DN_DV
DN_CONV_CH = 2 * DN_QK + DN_VW
N_GROUPS = 4
EXPERTS_PER_GROUP = 8
N_EXPERTS = N_GROUPS * EXPERTS_PER_GROUP
EXPERT_FF = 256

LANES = 128
SUBLANES = 8
VMEM_LIMIT = 56 * 1024 * 1024
DMA_PRIORITIES = 2

DN_SEQS_PER_STEP = 2
SEQS_PER_STEP = 2
MOE_TILE = 512
MOE_TOKEN_TILE = 1024
MOE_MOVE_TILE = 2048
DEC_TB = 8
ROW_DMA_UNROLL = 8

NT_DIMS = (((1,), (1,)), ((), ()))
TN_DIMS = (((0,), (0,)), ((), ()))


def _cparams(sem, **kw):
    return pltpu.CompilerParams(dimension_semantics=sem, vmem_limit_bytes=VMEM_LIMIT, **kw)


def _const_spec(shape):
    nd = len(shape)
    return pl.BlockSpec(shape, lambda *_: (0,) * nd)


def _sigmoid(x):
    return 1.0 / (1.0 + jnp.exp(-x))


def _silu(x):
    return x * _sigmoid(x)


def _softplus(x):
    return jnp.maximum(x, 0.0) + jnp.log(1.0 + jnp.exp(-jnp.abs(x)))


def _rms(x, gain):
    return x * lax.rsqrt(jnp.mean(x * x, axis=-1, keepdims=True) + EPS) * gain


def _prenorm_proj_kernel(x_ref, g_ref, w_ref, *out_refs, segs):
    h = _rms(x_ref[...], g_ref[...]).astype(BF16)
    for o_ref, (a, b) in zip(out_refs, segs):
        o_ref[...] = jnp.dot(h, w_ref[:, a:b], preferred_element_type=F32)


def prenorm_proj(x, gain, w_bf16, segs, tm):
    t = x.shape[0]
    n_pad = w_bf16.shape[1]
    return pl.pallas_call(
        functools.partial(_prenorm_proj_kernel, segs=segs),
        grid=(t // tm,),
        in_specs=[pl.BlockSpec((tm, D_MODEL), lambda i: (i, 0)),
                  _const_spec((1, D_MODEL)),
                  _const_spec((D_MODEL, n_pad))],
        out_specs=[pl.BlockSpec((tm, b - a), lambda i: (i, 0)) for a, b in segs],
        out_shape=[jax.ShapeDtypeStruct((t, b - a), F32) for a, b in segs],
        compiler_params=_cparams(("parallel",)),
        name="prenorm_proj",
    )(x, gain.reshape(1, D_MODEL), w_bf16)


def _swa_kernel(sink_ref, q_ref, kc_ref, kp_ref, vc_ref, vp_ref, qn_ref, kn_ref, o_ref, kout_ref):
    n = pl.program_id(1)
    blk = q_ref.shape[0]
    q = q_ref[...]
    kc, kp, vc, vp = kc_ref[...], kp_ref[...], vc_ref[...], vp_ref[...]
    qn, kn = qn_ref[...], kn_ref[...]
    row = lax.broadcasted_iota(jnp.int32, (blk, 2 * blk), 0)
    col = lax.broadcasted_iota(jnp.int32, (blk, 2 * blk), 1)
    rel = row + blk - col
    mask = (rel >= 0) & (rel <= WINDOW) & ((n > 0) | (col >= blk))
    outs, kouts = [], []
    for j in range(SWA_KV_HEADS):
        sl = slice(j * HEAD_DIM, (j + 1) * HEAD_DIM)
        kcj = _rms(kc[:, sl], kn)
        kpj = _rms(kp[:, sl], kn)
        kouts.append(kcj)
        kcat = jnp.concatenate([kpj, kcj], axis=0).astype(BF16)
        vcat = jnp.concatenate([vp[:, sl], vc[:, sl]], axis=0).astype(BF16)
        for g in range(SWA_GROUP):
            h = j * SWA_GROUP + g
            qh = _rms(q[:, h * HEAD_DIM:(h + 1) * HEAD_DIM], qn) * (HEAD_DIM ** -0.5)
            s = lax.dot_general(qh.astype(BF16), kcat, NT_DIMS, preferred_element_type=F32)
            s = jnp.where(mask, s, -jnp.inf)
            sink = sink_ref[h]
            m = jnp.maximum(jnp.max(s, axis=-1, keepdims=True), sink)
            p = jnp.exp(s - m)
            p = p / (jnp.sum(p, axis=-1, keepdims=True) + jnp.exp(sink - m))
            outs.append(jnp.dot(p.astype(BF16), vcat, preferred_element_type=F32))
    o_ref[...] = jnp.concatenate(outs, axis=1)
    kout_ref[...] = jnp.concatenate(kouts, axis=1)


def swa_attention(q, k, v, q_norm, k_norm, sinks):
    bsz, length = q.shape[0], q.shape[1]
    nb = length // WINDOW
    prev = lambda b, n: (b, jnp.maximum(n - 1, 0), 0)
    cur = lambda b, n: (b, n, 0)
    kv_blk = (None, WINDOW, SWA_KV)
    return pl.pallas_call(
        _swa_kernel,
        grid=(bsz, nb),
        in_specs=[pl.BlockSpec(memory_space=pltpu.SMEM),
                  pl.BlockSpec((None, WINDOW, SWA_Q), cur),
                  pl.BlockSpec(kv_blk, cur), pl.BlockSpec(kv_blk, prev),
                  pl.BlockSpec(kv_blk, cur), pl.BlockSpec(kv_blk, prev),
                  _const_spec((1, HEAD_DIM)), _const_spec((1, HEAD_DIM))],
        out_specs=[pl.BlockSpec((None, WINDOW, SWA_Q), cur), pl.BlockSpec(kv_blk, cur)],
        out_shape=[jax.ShapeDtypeStruct((bsz, length, SWA_Q), F32),
                   jax.ShapeDtypeStruct((bsz, length, SWA_KV), F32)],
        compiler_params=_cparams(("parallel", "arbitrary")),
        name="swa_attention",
    )(sinks, q, k, k, v, v, q_norm.reshape(1, HEAD_DIM), k_norm.reshape(1, HEAD_DIM))


def _chunk_conv(x_ref, xx_scr, w_ref, rows):
    xx_scr[SUBLANES:SUBLANES + rows, :] = x_ref[...]
    acc = None
    for tap in range(CONV_WIDTH):
        off = SUBLANES - (CONV_WIDTH - 1) + tap
        term = w_ref[tap:tap + 1, :] * xx_scr[off:off + rows, :]
        acc = term if acc is None else acc + term
    return acc


def _carry_conv_tail(xx_scr, rows):
    xx_scr[0:SUBLANES, :] = xx_scr[rows:rows + SUBLANES, :]


def _ssd_kernel(xbc_ref, z_ref, dt_ref, cw_ref, cb_ref, dtb_ref, alog_ref, dd_ref, nrm_ref,
                y_ref, hout_ref, xx_scr, h_scr):
    for s in range(xbc_ref.shape[0]):
        _ssd_chunk(xbc_ref.at[s], z_ref.at[s], dt_ref.at[s], cw_ref, cb_ref, dtb_ref, alog_ref, dd_ref, nrm_ref,
                   y_ref.at[s], hout_ref.at[s], xx_scr.at[s], h_scr.at[s])


def _ssd_chunk(xbc_ref, z_ref, dt_ref, cw_ref, cb_ref, dtb_ref, alog_ref, dd_ref, nrm_ref,
               y_ref, hout_ref, xx_scr, h_scr):
    q_len = xbc_ref.shape[0]

    @pl.when(pl.program_id(1) == 0)
    def _():
        xx_scr[0:SUBLANES, :] = jnp.zeros((SUBLANES, xx_scr.shape[1]), F32)
        h_scr[...] = jnp.zeros_like(h_scr)

    act = _silu(_chunk_conv(xbc_ref, xx_scr, cw_ref, q_len) + cb_ref[...])
    _carry_conv_tail(xx_scr, q_len)
    xs = act[:, :SSM_INNER]
    bm = act[:, SSM_INNER:SSM_INNER + SSM_GROUPS * SSM_STATE]
    cm = act[:, SSM_INNER + SSM_GROUPS * SSM_STATE:]

    row = lax.broadcasted_iota(jnp.int32, (q_len, q_len), 0)
    col = lax.broadcasted_iota(jnp.int32, (q_len, q_len), 1)
    causal = row >= col
    dt = _softplus(dt_ref[...] + dtb_ref[...])
    da = dt * (-jnp.exp(alog_ref[...]))
    cum = jnp.dot(causal.astype(F32), da, preferred_element_type=F32, precision=lax.Precision.HIGHEST)
    cum_t = cum.T
    dt_t = dt.T
    e_cum = jnp.exp(cum)
    hpg = SSM_HEADS // SSM_GROUPS
    gw = hpg * SSM_HEAD_DIM
    ys = []
    for g in range(SSM_GROUPS):
        bm_g = bm[:, g * SSM_STATE:(g + 1) * SSM_STATE].astype(BF16)
        cm_g = cm[:, g * SSM_STATE:(g + 1) * SSM_STATE].astype(BF16)
        cb = lax.dot_general(cm_g, bm_g, NT_DIMS, preferred_element_type=F32)
        h_g = h_scr[g * gw:(g + 1) * gw, :]
        y_state = lax.dot_general(cm_g, h_g.astype(BF16), NT_DIMS, preferred_element_type=F32)
        xt_parts, dec_parts = [], []
        for hh in range(hpg):
            h = g * hpg + hh
            x_h = xs[:, h * SSM_HEAD_DIM:(h + 1) * SSM_HEAD_DIM]
            cum_c = cum[:, h:h + 1]
            seg = jnp.exp(jnp.where(causal, cum_c - cum_t[h:h + 1, :], -jnp.inf))
            wgt = cb * seg * dt_t[h:h + 1, :]
            y = jnp.dot(wgt.astype(BF16), x_h.astype(BF16), preferred_element_type=F32)
            y = y + y_state[:, hh * SSM_HEAD_DIM:(hh + 1) * SSM_HEAD_DIM] * e_cum[:, h:h + 1]
            ys.append(y + dd_ref[0, h] * x_h)
            c_last = cum[q_len - 1:q_len, h:h + 1]
            xt_parts.append(x_h * (jnp.exp(c_last - cum_c) * dt[:, h:h + 1]))
            dec_parts.append(jnp.broadcast_to(jnp.exp(c_last), (SSM_HEAD_DIM, SSM_STATE)))
        xt = jnp.concatenate(xt_parts, axis=1).astype(BF16)
        upd = lax.dot_general(xt, bm_g, TN_DIMS, preferred_element_type=F32)
        h_scr[g * gw:(g + 1) * gw, :] = h_g * jnp.concatenate(dec_parts, axis=0) + upd
    y_all = jnp.concatenate(ys, axis=1) * _silu(z_ref[...])
    nrm = nrm_ref[...]
    y_ref[...] = jnp.concatenate(
        [_rms(y_all[:, g * gw:(g + 1) * gw], nrm[:, g * gw:(g + 1) * gw]) for g in range(SSM_GROUPS)], axis=1)
    hout_ref[...] = h_scr[...]


def ssd_mixer(xbc, z, dt, conv_w, conv_b, dt_bias, a_log, d_skip, norm_g):
    bsz, length = xbc.shape[0], xbc.shape[1]
    nc = length // SSM_CHUNK
    cur = lambda b, c: (b, c, 0)
    per_b = lambda b, c: (b, 0, 0)
    pad8 = lambda v: jnp.pad(v.reshape(1, SSM_HEADS), ((0, 0), (0, LANES - SSM_HEADS)))
    return pl.pallas_call(
        _ssd_kernel,
        grid=(bsz // SEQS_PER_STEP, nc),
        in_specs=[pl.BlockSpec((SEQS_PER_STEP, SSM_CHUNK, SSM_CONV_CH), cur),
                  pl.BlockSpec((SEQS_PER_STEP, SSM_CHUNK, SSM_INNER), cur),
                  pl.BlockSpec((SEQS_PER_STEP, SSM_CHUNK, LANES), cur),
                  _const_spec((CONV_WIDTH, SSM_CONV_CH)), _const_spec((1, SSM_CONV_CH)),
                  _const_spec((1, LANES)), _const_spec((1, LANES)),
                  pl.BlockSpec(memory_space=pltpu.SMEM),
                  _const_spec((1, SSM_INNER))],
        out_specs=[pl.BlockSpec((SEQS_PER_STEP, SSM_CHUNK, SSM_INNER), cur),
                   pl.BlockSpec((SEQS_PER_STEP, SSM_INNER, SSM_STATE), per_b)],
        out_shape=[jax.ShapeDtypeStruct((bsz, length, SSM_INNER), F32),
                   jax.ShapeDtypeStruct((bsz, SSM_INNER, SSM_STATE), F32)],
        scratch_shapes=[pltpu.VMEM((SEQS_PER_STEP, SSM_CHUNK + SUBLANES, SSM_CONV_CH), F32),
                        pltpu.VMEM((SEQS_PER_STEP, SSM_INNER, SSM_STATE), F32)],
        compiler_params=_cparams(("parallel", "arbitrary")),
        name="ssd_mixer",
    )(xbc, z, dt, conv_w, conv_b.reshape(1, SSM_CONV_CH), pad8(dt_bias), pad8(a_log),
      d_skip.reshape(1, SSM_HEADS), norm_g.reshape(1, SSM_INNER))


def _unit_lower_inverses(lmats, row, col):
    c = lmats[0].shape[0]
    mm = lambda a, b: jnp.dot(a.astype(BF16), b.astype(BF16), preferred_element_type=F32)
    eye = (row == col).astype(F32)
    blk = SUBLANES
    same = (row // blk) == (col // blk)
    xs = [jnp.where(same, -l, 0.0) for l in lmats]
    invs = [eye + x for x in xs]
    p = blk
    while p > 2:
        xs = [mm(x, x) for x in xs]
        invs = [i + mm(i, x) for i, x in zip(invs, xs)]
        p //= 2
    while blk < c:
        outer = ((row // (2 * blk)) == (col // (2 * blk))) & ((row // blk) != (col // blk))
        ts = [mm(i, jnp.where(outer, l, 0.0)) for i, l in zip(invs, lmats)]
        invs = [i - mm(t, i) for i, t in zip(invs, ts)]
        blk *= 2
    return invs


def _dn_kernel(qkv_ref, z_ref, ba_ref, cw_ref, dtb_ref, alog_ref, nrm_ref, o_ref, sout_ref, xx_scr, s_scr):
    nseq, c_len = qkv_ref.shape[0], qkv_ref.shape[1]
    seqs = range(nseq)
    chains = [(s, h) for s in seqs for h in range(DN_HEADS)]
    ids = range(len(chains))

    @pl.when(pl.program_id(1) == 0)
    def _():
        xx_scr[:, 0:SUBLANES, :] = jnp.zeros((nseq, SUBLANES, xx_scr.shape[2]), F32)
        s_scr[...] = jnp.zeros_like(s_scr)

    row = lax.broadcasted_iota(jnp.int32, (c_len, c_len), 0)
    col = lax.broadcasted_iota(jnp.int32, (c_len, c_len), 1)
    incl = row >= col
    strict = row > col
    tri = incl.astype(F32)
    nrm = nrm_ref[...]
    bf = lambda t: t.astype(BF16)
    mm = lambda a, b: jnp.dot(bf(a), bf(b), preferred_element_type=F32)

    act, beta, cum, cum_t, e_cum, e_rest, e_last = [], [], [], [], [], [], []
    for s in seqs:
        act.append(_silu(_chunk_conv(qkv_ref.at[s], xx_scr.at[s], cw_ref, c_len)))
        _carry_conv_tail(xx_scr.at[s], c_len)
        ba = ba_ref[s]
        beta.append(_sigmoid(ba))
        gate = -jnp.exp(alog_ref[...]) * _softplus(ba + dtb_ref[...])
        cum_s = jnp.dot(tri, gate, preferred_element_type=F32, precision=lax.Precision.HIGHEST)
        c_last = cum_s[c_len - 1:c_len, :]
        cum.append(cum_s)
        cum_t.append(cum_s.T)
        e_cum.append(jnp.exp(cum_s))
        e_rest.append(jnp.exp(c_last - cum_s))
        e_last.append(jnp.exp(c_last))

    q, k, kb, rhs, decay = [], [], [], [], []
    for i in ids:
        s, h = chains[i]
        gl = DN_HEADS + h
        q_h = act[s][:, h * DN_DK:(h + 1) * DN_DK]
        k_h = act[s][:, DN_QK + h * DN_DK:DN_QK + (h + 1) * DN_DK]
        v_h = act[s][:, 2 * DN_QK + h * DN_DV:2 * DN_QK + (h + 1) * DN_DV]
        q.append(q_h * lax.rsqrt(jnp.sum(q_h * q_h, axis=-1, keepdims=True) + EPS) * (DN_DK ** -0.5))
        k.append(k_h * lax.rsqrt(jnp.sum(k_h * k_h, axis=-1, keepdims=True) + EPS))
        beta_c = beta[s][:, h:h + 1]
        kb.append(k[i] * beta_c)
        rhs.append(jnp.concatenate([v_h * beta_c, kb[i] * e_cum[s][:, gl:gl + 1]], axis=1))
        decay.append(jnp.exp(jnp.where(incl, cum[s][:, gl:gl + 1] - cum_t[s][gl:gl + 1, :], -jnp.inf)))
    kq = [lax.dot_general(bf(jnp.concatenate([kb[i], q[i]], axis=0)), bf(k[i]), NT_DIMS,
                          preferred_element_type=F32) for i in ids]
    lmat = [jnp.where(strict, kq[i][:c_len] * decay[i], 0.0) for i in ids]
    attn = [kq[i][c_len:] * decay[i] for i in ids]
    tinv = _unit_lower_inverses(lmat, row, col)
    uw = [mm(tinv[i], rhs[i]) for i in ids]
    s_old = [s_scr[s, h * DN_DK:(h + 1) * DN_DK, :] for s, h in chains]
    qe = [q[i] * e_cum[s][:, DN_HEADS + h:DN_HEADS + h + 1] for i, (s, h) in enumerate(chains)]
    wq = [mm(jnp.concatenate([uw[i][:, DN_DV:], qe[i]], axis=0), s_old[i]) for i in ids]
    v_new = [uw[i][:, :DN_DV] - wq[i][:c_len] for i in ids]
    o = [wq[i][c_len:] + mm(attn[i], v_new[i]) for i in ids]
    for i in ids:
        s, h = chains[i]
        gl = DN_HEADS + h
        s_scr[s, h * DN_DK:(h + 1) * DN_DK, :] = s_old[i] * e_last[s][:, gl:gl + 1] + lax.dot_general(
            bf(k[i] * e_rest[s][:, gl:gl + 1]), bf(v_new[i]), TN_DIMS, preferred_element_type=F32)
    for s in seqs:
        z = z_ref[s]
        o_ref[s] = jnp.concatenate(
            [_rms(o[s * DN_HEADS + h], nrm) * _silu(z[:, h * DN_DV:(h + 1) * DN_DV]) for h in range(DN_HEADS)],
            axis=1)
    sout_ref[...] = s_scr[...]


def dn_mixer(qkv, z, ba, conv_w, dt_bias, a_log, norm_g, nseq):
    bsz, length = qkv.shape[0], qkv.shape[1]
    nc = length // DN_CHUNK
    cur = lambda b, c: (b, c, 0)
    per_b = lambda b, c: (b, 0, 0)
    pad_a = lambda v: jnp.pad(v.reshape(1, DN_HEADS), ((0, 0), (DN_HEADS, LANES - 2 * DN_HEADS)))
    return pl.pallas_call(
        _dn_kernel,
        grid=(bsz // nseq, nc),
        in_specs=[pl.BlockSpec((nseq, DN_CHUNK, DN_CONV_CH), cur),
                  pl.BlockSpec((nseq, DN_CHUNK, DN_VW), cur),
                  pl.BlockSpec((nseq, DN_CHUNK, LANES), cur),
                  _const_spec((CONV_WIDTH, DN_CONV_CH)),
                  _const_spec((1, LANES)), _const_spec((1, LANES)), _const_spec((1, DN_DV))],
        out_specs=[pl.BlockSpec((nseq, DN_CHUNK, DN_VW), cur),
                   pl.BlockSpec((nseq, DN_QK, DN_DV), per_b)],
        out_shape=[jax.ShapeDtypeStruct((bsz, length, DN_VW), F32),
                   jax.ShapeDtypeStruct((bsz, DN_QK, DN_DV), F32)],
        scratch_shapes=[pltpu.VMEM((nseq, DN_CHUNK + SUBLANES, DN_CONV_CH), F32),
                        pltpu.VMEM((nseq, DN_QK, DN_DV), F32)],
        compiler_params=_cparams(("parallel", "arbitrary")),
        name="dn_mixer",
    )(qkv, z, ba, conv_w, pad_a(dt_bias), pad_a(a_log), norm_g.reshape(1, DN_DV))


X_SUB = D_MODEL // LANES


def _rows_to_slabs(x):
    parts = jnp.stack([x[:, s * LANES:(s + 1) * LANES] for s in range(X_SUB)], axis=0)
    return jnp.transpose(parts, (1, 0, 2))


def _slabs_to_rows(slab):
    parts = jnp.transpose(slab, (1, 0, 2))
    return jnp.concatenate([parts[s] for s in range(X_SUB)], axis=1)


def _resid_proj_kernel(x_ref, *refs, n_in):
    acc = x_ref[...]
    for a_ref, w_ref in zip(refs[:n_in], refs[n_in:2 * n_in]):
        acc = acc + jnp.dot(a_ref[...].astype(BF16), w_ref[...], preferred_element_type=F32)
    refs[2 * n_in][...] = acc


def resid_proj(x, acts, ws, tm):
    t = x.shape[0]
    n_in = len(acts)
    tok = lambda i: (i, 0)
    return pl.pallas_call(
        functools.partial(_resid_proj_kernel, n_in=n_in),
        grid=(t // tm,),
        in_specs=([pl.BlockSpec((tm, D_MODEL), tok)] + [pl.BlockSpec((tm, a.shape[1]), tok) for a in acts]
                  + [_const_spec(w.shape) for w in ws]),
        out_specs=pl.BlockSpec((tm, D_MODEL), tok),
        out_shape=jax.ShapeDtypeStruct((t, D_MODEL), F32),
        compiler_params=_cparams(("parallel",)),
        name="resid_proj",
    )(x, *acts, *ws)


def _split_bf16(v):
    hi = v.astype(BF16)
    return hi, (v - hi.astype(F32)).astype(BF16)


def _router_kernel(x_ref, g_ref, wr_hi_ref, wr_lo_ref, br_ref, xg_ref, gates_ref, grp_ref, rank_ref, cnt_ref,
                   carry_scr):
    tm = x_ref.shape[0]

    @pl.when(pl.program_id(0) == 0)
    def _():
        carry_scr[...] = jnp.zeros_like(carry_scr)

    x = x_ref[...]
    h_hi, h_lo = _split_bf16(_rms(x, g_ref[...]))
    w_hi, w_lo = wr_hi_ref[...], wr_lo_ref[...]
    logits = (jnp.dot(h_hi, w_hi, preferred_element_type=F32) + jnp.dot(h_lo, w_hi, preferred_element_type=F32)
              + jnp.dot(h_hi, w_lo, preferred_element_type=F32)) + br_ref[...]
    lt = logits.T
    sub = lax.broadcasted_iota(jnp.int32, (SUBLANES, tm), 0)
    neg = -jnp.inf
    glog = jnp.where(sub < N_GROUPS, lt[N_EXPERTS:N_EXPERTS + SUBLANES], neg)
    gmax = jnp.max(glog, axis=0, keepdims=True)
    g_p = 1.0 / jnp.sum(jnp.exp(glog - gmax), axis=0, keepdims=True)
    g_i = jnp.min(jnp.where(glog == gmax, sub, SUBLANES), axis=0, keepdims=True)
    sel = lt[0:EXPERTS_PER_GROUP]
    for g in range(1, N_GROUPS):
        sel = jnp.where(g_i == g, lt[g * EXPERTS_PER_GROUP:(g + 1) * EXPERTS_PER_GROUP], sel)
    m1 = jnp.max(sel, axis=0, keepdims=True)
    zsum = jnp.sum(jnp.exp(sel - m1), axis=0, keepdims=True)
    i1 = jnp.min(jnp.where(sel == m1, sub, SUBLANES), axis=0, keepdims=True)
    sel2 = jnp.where(sub == i1, neg, sel)
    m2 = jnp.max(sel2, axis=0, keepdims=True)
    i2 = jnp.min(jnp.where(sel2 == m2, sub, SUBLANES), axis=0, keepdims=True)
    p1 = 1.0 / zsum
    p2 = jnp.exp(m2 - m1) / zsum
    gate1 = g_p * p1 / (p1 + p2)
    gate2 = g_p * p2 / (p1 + p2)
    gates = jnp.where(sub == i1, gate1, 0.0) + jnp.where(sub == i2, gate2, 0.0)

    onehot = (sub == g_i).astype(F32)
    before = (lax.broadcasted_iota(jnp.int32, (tm, tm), 0) < lax.broadcasted_iota(jnp.int32, (tm, tm), 1))
    rank_in_tile = jnp.dot(onehot.astype(BF16), before.astype(BF16), preferred_element_type=F32)
    carry = carry_scr[...]
    rank = jnp.sum(onehot * (rank_in_tile + carry[:, 0:1]), axis=0, keepdims=True)
    grp_ref[...] = g_i
    rank_ref[...] = rank.astype(jnp.int32)
    new_carry = carry + jnp.sum(onehot, axis=1, keepdims=True)
    carry_scr[...] = new_carry
    cnt_ref[...] = new_carry

    gates_ref[...] = jnp.concatenate([gates, jnp.zeros((LANES - SUBLANES, tm), F32)], axis=0).T
    xg_ref[...] = _rows_to_slabs(x)


def moe_router(x, gain, w_router_pad, b_router_pad, tm):
    t = x.shape[0]
    w_hi = w_router_pad.astype(BF16)
    w_lo = (w_router_pad - w_hi.astype(F32)).astype(BF16)
    return pl.pallas_call(
        _router_kernel,
        grid=(t // tm,),
        in_specs=[pl.BlockSpec((tm, D_MODEL), lambda i: (i, 0)), _const_spec((1, D_MODEL)),
                  _const_spec((D_MODEL, LANES)), _const_spec((D_MODEL, LANES)), _const_spec((1, LANES))],
        out_specs=[pl.BlockSpec((tm, X_SUB, LANES), lambda i: (i, 0, 0)),
                   pl.BlockSpec((tm, LANES), lambda i: (i, 0)),
                   pl.BlockSpec((None, 1, tm), lambda i: (i, 0, 0)),
                   pl.BlockSpec((None, 1, tm), lambda i: (i, 0, 0)),
                   _const_spec((SUBLANES, LANES))],
        out_shape=[jax.ShapeDtypeStruct((t, X_SUB, LANES), F32),
                   jax.ShapeDtypeStruct((t, LANES), F32),
                   jax.ShapeDtypeStruct((t // tm, 1, tm), jnp.int32),
                   jax.ShapeDtypeStruct((t // tm, 1, tm), jnp.int32),
                   jax.ShapeDtypeStruct((SUBLANES, LANES), F32)],
        scratch_shapes=[pltpu.VMEM((SUBLANES, LANES), F32)],
        compiler_params=_cparams(("arbitrary",)),
        name="moe_router",
    )(x, gain.reshape(1, D_MODEL), w_hi, w_lo, b_router_pad)


def _issue_row_copies(n_rows, make_copy):
    def trip(i, carry):
        for u in range(ROW_DMA_UNROLL):
            make_copy(i * ROW_DMA_UNROLL + u).start(priority=u % DMA_PRIORITIES)
        return carry

    lax.fori_loop(0, n_rows // ROW_DMA_UNROLL, trip, 0)


def _dispatch_kernel(pos_ref, pad_lo_ref, pad_hi_ref, xg_ref, gates_ref, xs_hbm, gs_hbm, zero_scr,
                     sem, gsem, zsem, zgsem):
    tm = xg_ref.shape[0]
    gate_row = lambda ref, r: ref.at[pl.ds(r, 1)]
    _issue_row_copies(tm, lambda r: pltpu.make_async_copy(xg_ref.at[r], xs_hbm.at[pos_ref[0, r]], sem))
    _issue_row_copies(tm, lambda r: pltpu.make_async_copy(gate_row(gates_ref, r),
                                                          gate_row(gs_hbm, pos_ref[0, r]), gsem))
    pltpu.make_async_copy(xg_ref, xs_hbm.at[pl.ds(0, tm)], sem).wait()
    pltpu.make_async_copy(gates_ref, gs_hbm.at[pl.ds(0, tm)], gsem).wait()

    @pl.when(pl.program_id(0) == pl.num_programs(0) - 1)
    def _():
        zero_scr[...] = jnp.zeros_like(zero_scr)
        zero_gate = gate_row(zero_scr, 0)
        for k in range(pad_lo_ref.shape[0]):
            def fill(j, carry):
                pltpu.make_async_copy(zero_scr, xs_hbm.at[j], zsem).start()
                pltpu.make_async_copy(zero_gate, gate_row(gs_hbm, j), zgsem).start()
                return carry

            def drain(j, carry):
                pltpu.make_async_copy(zero_scr, xs_hbm.at[j], zsem).wait()
                pltpu.make_async_copy(zero_gate, gate_row(gs_hbm, j), zgsem).wait()
                return carry

            lax.fori_loop(pad_lo_ref[k], pad_hi_ref[k], fill, 0)
            lax.fori_loop(pad_lo_ref[k], pad_hi_ref[k], drain, 0)


def moe_dispatch(pos, pad_lo, pad_hi, xg, gates, tm, n_slots):
    t = xg.shape[0]
    dma = pltpu.SemaphoreType.DMA
    return pl.pallas_call(
        _dispatch_kernel,
        grid=(t // tm,),
        in_specs=[pl.BlockSpec((None, 1, tm), lambda i: (i, 0, 0), memory_space=pltpu.SMEM),
                  pl.BlockSpec(memory_space=pltpu.SMEM), pl.BlockSpec(memory_space=pltpu.SMEM),
                  pl.BlockSpec((tm, X_SUB, LANES), lambda i: (i, 0, 0)),
                  pl.BlockSpec((tm, LANES), lambda i: (i, 0))],
        out_specs=[pl.BlockSpec(memory_space=pl.ANY), pl.BlockSpec(memory_space=pl.ANY)],
        out_shape=[jax.ShapeDtypeStruct((n_slots, X_SUB, LANES), F32),
                   jax.ShapeDtypeStruct((n_slots, LANES), F32)],
        scratch_shapes=[pltpu.VMEM((X_SUB, LANES), F32), dma, dma, dma, dma],
        compiler_params=_cparams(("arbitrary",), has_side_effects=True),
        name="moe_dispatch",
    )(pos, pad_lo, pad_hi, xg, gates)


def _collect_kernel(pos_ref, os_hbm, out_ref, buf, sem):
    tm = out_ref.shape[0]
    _issue_row_copies(tm, lambda r: pltpu.make_async_copy(os_hbm.at[pos_ref[0, r]], buf.at[r], sem))
    pltpu.make_async_copy(os_hbm.at[pl.ds(0, tm)], buf, sem).wait()
    out_ref[...] = _slabs_to_rows(buf[...])


def moe_collect(pos, os_sorted, t, tm):
    return pl.pallas_call(
        _collect_kernel,
        grid=(t // tm,),
        in_specs=[pl.BlockSpec((None, 1, tm), lambda i: (i, 0, 0), memory_space=pltpu.SMEM),
                  pl.BlockSpec(memory_space=pl.ANY)],
        out_specs=pl.BlockSpec((tm, D_MODEL), lambda i: (i, 0)),
        out_shape=jax.ShapeDtypeStruct((t, D_MODEL), F32),
        scratch_shapes=[pltpu.VMEM((tm, X_SUB, LANES), F32), pltpu.SemaphoreType.DMA],
        compiler_params=_cparams(("arbitrary",), has_side_effects=True),
        name="moe_collect",
    )(pos, os_sorted)


def _group_experts_kernel(grp_ref, xs_ref, gates_ref, g_ref, wg_ref, wu_ref, wd_ref, o_ref):
    del grp_ref
    x = _slabs_to_rows(xs_ref[...])
    gates = gates_ref[...]
    h = _rms(x, g_ref[...]).astype(BF16)
    acc = x
    for e in range(EXPERTS_PER_GROUP):
        gate = jnp.dot(h, wg_ref[e], preferred_element_type=F32)
        up = jnp.dot(h, wu_ref[e], preferred_element_type=F32)
        act = (_silu(gate) * up).astype(BF16)
        acc = acc + gates[:, e:e + 1] * jnp.dot(act, wd_ref[e], preferred_element_type=F32)
    o_ref[...] = _rows_to_slabs(acc)


def moe_group_experts(tile_grp, xs, gs, gain, wg, wu, wd, tile):
    n_steps = xs.shape[0] // tile
    rows = lambda i, grp: (i, 0, 0)
    wts = lambda i, grp: (grp[i], 0, 0, 0)
    return pl.pallas_call(
        _group_experts_kernel,
        grid_spec=pltpu.PrefetchScalarGridSpec(
            num_scalar_prefetch=1,
            grid=(n_steps,),
            in_specs=[pl.BlockSpec((tile, X_SUB, LANES), rows),
                      pl.BlockSpec((tile, LANES), lambda i, grp: (i, 0)),
                      pl.BlockSpec((1, D_MODEL), lambda i, grp: (0, 0)),
                      pl.BlockSpec((None, EXPERTS_PER_GROUP, D_MODEL, EXPERT_FF), wts),
                      pl.BlockSpec((None, EXPERTS_PER_GROUP, D_MODEL, EXPERT_FF), wts),
                      pl.BlockSpec((None, EXPERTS_PER_GROUP, EXPERT_FF, D_MODEL), wts)],
            out_specs=pl.BlockSpec((tile, X_SUB, LANES), rows)),
        out_shape=jax.ShapeDtypeStruct((xs.shape[0], X_SUB, LANES), F32),
        compiler_params=_cparams(("arbitrary",)),
        name="moe_group_experts",
    )(tile_grp, xs, gs, gain.reshape(1, D_MODEL), wg, wu, wd)


def mix_out_and_moe(x, acts, ws, gain, w_route, b_route, wg, wu, wd):
    t = x.shape[0]
    tile = MOE_TILE if t % MOE_TILE == 0 else t
    tm = MOE_TOKEN_TILE if t % MOE_TOKEN_TILE == 0 else tile
    tmove = MOE_MOVE_TILE if t % MOE_MOVE_TILE == 0 else tm
    n_tiles = t // tile + N_GROUPS
    x = resid_proj(x, acts, ws, tm)
    xg, gates, grp, rank, cnt = moe_router(x, gain, w_route, b_route, tm)
    counts = cnt[:N_GROUPS, 0].astype(jnp.int32)
    tile_ends = jnp.cumsum((counts + tile - 1) // tile)
    offs = (tile_ends - (counts + tile - 1) // tile) * tile
    pos = rank + sum(jnp.where(grp == g, offs[g], 0) for g in range(N_GROUPS))
    pad_lo = jnp.concatenate([offs + counts, tile_ends[-1:] * tile])
    pad_hi = jnp.concatenate([tile_ends * tile, jnp.full((1,), n_tiles * tile, jnp.int32)])
    tile_grp = jnp.minimum(jnp.sum((jnp.arange(n_tiles, dtype=jnp.int32)[:, None] >= tile_ends[None, :])
                                   .astype(jnp.int32), axis=1), N_GROUPS - 1)
    pos = pos.reshape(t // tmove, 1, tmove)
    xs, gs = moe_dispatch(pos, pad_lo, pad_hi, xg, gates, tmove, n_tiles * tile)
    os_sorted = moe_group_experts(tile_grp, xs, gs, gain, wg, wu, wd, tile)
    return moe_collect(pos, os_sorted, t, tmove)


BATCH_NT = (((2,), (2,)), ((0,), (0,)))
BATCH_NN = (((2,), (1,)), ((0,), (0,)))


def _token_major(parts):
    return jnp.transpose(jnp.stack(parts, axis=0), (1, 0, 2))


def _pad_rows(parts, tb, width):
    return parts + [jnp.zeros((tb, width), F32)] * (SUBLANES - len(parts))


def _spread(v, first, n, width):
    tb = v.shape[0]
    return jnp.concatenate([jnp.broadcast_to(v[:, first + i:first + i + 1], (tb, width)) for i in range(n)], axis=1)


def _columns(rows):
    tb, m = rows.shape
    return jnp.concatenate([rows, jnp.zeros((LANES - tb, m), F32)], axis=0).T


def _one_step_conv(x_ref, buf_ref, w_ref):
    return (w_ref[0:1, :] * buf_ref[:, 0, :] + w_ref[1:2, :] * buf_ref[:, 1, :] + w_ref[2:3, :] * buf_ref[:, 2, :]
            + w_ref[3:4, :] * x_ref[...])


def _swa_decode_kernel(sink_ref, q_ref, k_ref, v_ref, kc_ref, vc_ref, qn_ref, kn_ref, o_ref, kout_ref):
    tb = q_ref.shape[0]
    lo = lax.broadcasted_iota(jnp.int32, (tb, LANES), 1) < HEAD_DIM

    def pair_rms(t, gain):
        sq = t * t
        s_lo = jnp.sum(jnp.where(lo, sq, 0.0), axis=-1, keepdims=True)
        s_hi = jnp.sum(jnp.where(lo, 0.0, sq), axis=-1, keepdims=True)
        return t * lax.rsqrt(jnp.where(lo, s_lo, s_hi) * (1.0 / HEAD_DIM) + EPS) * gain

    qn2 = jnp.concatenate([qn_ref[...], qn_ref[...]], axis=1)
    kn2 = jnp.concatenate([kn_ref[...], kn_ref[...]], axis=1)
    k_new = pair_rms(k_ref[...], kn2)
    kout_ref[...] = k_new
    v_new = v_ref[...]
    rows = []
    for h in range(SWA_HEADS):
        t = pair_rms(q_ref[:, (h // 2) * LANES:(h // 2 + 1) * LANES], qn2) * (HEAD_DIM ** -0.5)
        j = h // SWA_GROUP
        if h % 2 != j:
            t = pltpu.roll(t, HEAD_DIM, axis=1)
        rows.append(jnp.where(lo if j == 0 else jnp.logical_not(lo), t, 0.0))
    q8 = _token_major(rows)
    s = lax.dot_general(q8.astype(BF16), kc_ref[...].astype(BF16), BATCH_NT, preferred_element_type=F32)
    s_new = jnp.sum(q8 * k_new[:, None, :], axis=-1, keepdims=True)
    sub = lax.broadcasted_iota(jnp.int32, (1, SWA_HEADS, 1), 1)
    sink = jnp.zeros((1, SWA_HEADS, 1), F32)
    for h in range(SWA_HEADS):
        sink = jnp.where(sub == h, sink_ref[h], sink)
    m = jnp.maximum(jnp.maximum(jnp.max(s, axis=-1, keepdims=True), s_new), sink)
    p = jnp.exp(s - m)
    p_new = jnp.exp(s_new - m)
    den = jnp.sum(p, axis=-1, keepdims=True) + p_new + jnp.exp(sink - m)
    o8 = lax.dot_general((p / den).astype(BF16), vc_ref[...].astype(BF16), BATCH_NN,
                         preferred_element_type=F32) + (p_new / den) * v_new[:, None, :]
    o_h = jnp.transpose(o8, (1, 0, 2))
    tiles = []
    for t in range(SWA_HEADS // 2):
        halves = []
        for h in (2 * t, 2 * t + 1):
            piece = o_h[h]
            if h % 2 != h // SWA_GROUP:
                piece = pltpu.roll(piece, HEAD_DIM, axis=1)
            halves.append(piece)
        tiles.append(jnp.where(lo, halves[0], halves[1]))
    o_ref[...] = jnp.concatenate(tiles, axis=1)


def swa_decode(q, k, v, k_cache, v_cache, q_norm, k_norm, sinks):
    bsz = q.shape[0]
    tok = lambda i: (i, 0)
    tok3 = lambda i: (i, 0, 0)
    return pl.pallas_call(
        _swa_decode_kernel,
        grid=(bsz // DEC_TB,),
        in_specs=[pl.BlockSpec(memory_space=pltpu.SMEM),
                  pl.BlockSpec((DEC_TB, SWA_Q), tok), pl.BlockSpec((DEC_TB, SWA_KV), tok),
                  pl.BlockSpec((DEC_TB, SWA_KV), tok),
                  pl.BlockSpec((DEC_TB, WINDOW, SWA_KV), tok3), pl.BlockSpec((DEC_TB, WINDOW, SWA_KV), tok3),
                  _const_spec((1, HEAD_DIM)), _const_spec((1, HEAD_DIM))],
        out_specs=[pl.BlockSpec((DEC_TB, SWA_Q), tok), pl.BlockSpec((DEC_TB, SWA_KV), tok)],
        out_shape=[jax.ShapeDtypeStruct((bsz, SWA_Q), F32), jax.ShapeDtypeStruct((bsz, SWA_KV), F32)],
        compiler_params=_cparams(("parallel",)),
        name="swa_decode",
    )(sinks, q, k, v, k_cache, v_cache, q_norm.reshape(1, HEAD_DIM), k_norm.reshape(1, HEAD_DIM))


def _ssd_decode_kernel(x_ref, buf_ref, z_ref, dt_ref, cw_ref, cb_ref, dtb_ref, alog_ref, dd_ref,
                       nrm_ref, h0_ref, y_ref, hout_ref):
    tb = x_ref.shape[0]
    act = _silu(_one_step_conv(x_ref, buf_ref, cw_ref) + cb_ref[...])
    xs = act[:, :SSM_INNER]
    bm = act[:, SSM_INNER:SSM_INNER + SSM_GROUPS * SSM_STATE]
    cm = act[:, SSM_INNER + SSM_GROUPS * SSM_STATE:]
    dt = _softplus(dt_ref[...] + dtb_ref[...])
    dec = jnp.exp(dt * (-jnp.exp(alog_ref[...])))
    dt_x = _spread(dt, 0, SSM_HEADS, SSM_HEAD_DIM)
    dec_x = _spread(dec, 0, SSM_HEADS, SSM_HEAD_DIM)
    gw = SSM_INNER // SSM_GROUPS
    grp = lambda t, g: t[:, g * SSM_STATE:(g + 1) * SSM_STATE]
    cb_x = jnp.concatenate([jnp.broadcast_to(jnp.sum(grp(cm, g) * grp(bm, g), axis=-1, keepdims=True), (tb, gw))
                            for g in range(SSM_GROUPS)], axis=1)
    h0 = h0_ref[...]
    c8 = _token_major(_pad_rows([grp(cm, g) for g in range(SSM_GROUPS)], tb, SSM_STATE))
    ys = jnp.transpose(lax.dot_general(c8.astype(BF16), h0.astype(BF16), BATCH_NT, preferred_element_type=F32),
                       (1, 0, 2))
    y_state = jnp.concatenate([ys[g][:, g * gw:(g + 1) * gw] for g in range(SSM_GROUPS)], axis=1)
    y = cb_x * dt_x * xs + y_state * dec_x + dd_ref[...] * xs
    y = y * _silu(z_ref[...])
    nrm = nrm_ref[...]
    y_ref[...] = jnp.concatenate(
        [_rms(y[:, g * gw:(g + 1) * gw], nrm[:, g * gw:(g + 1) * gw]) for g in range(SSM_GROUPS)], axis=1)
    u_cols = _columns(dt_x * xs)
    hpg = SSM_HEADS // SSM_GROUPS
    for b in range(tb):
        for g in range(SSM_GROUPS):
            rows = slice(g * gw, (g + 1) * gw)
            decay = jnp.concatenate([jnp.broadcast_to(dec[b:b + 1, h:h + 1], (SSM_HEAD_DIM, SSM_STATE))
                                     for h in range(g * hpg, (g + 1) * hpg)], axis=0)
            hout_ref[b, rows, :] = h0[b, rows, :] * decay + u_cols[rows, b:b + 1] * grp(bm, g)[b:b + 1, :]


def ssd_decode(xbc, bufs, z, dt, conv_w, conv_b, dt_bias, a_log, d_skip, norm_g, h0):
    bsz = xbc.shape[0]
    tok = lambda i: (i, 0)
    tok3 = lambda i: (i, 0, 0)
    pad8 = lambda v: jnp.pad(v.reshape(1, SSM_HEADS), ((0, 0), (0, LANES - SSM_HEADS)))
    row = lambda w: pl.BlockSpec((DEC_TB, w), tok)
    return pl.pallas_call(
        _ssd_decode_kernel,
        grid=(bsz // DEC_TB,),
        in_specs=[row(SSM_CONV_CH), pl.BlockSpec((DEC_TB, CONV_WIDTH - 1, SSM_CONV_CH), tok3),
                  row(SSM_INNER), row(LANES),
                  _const_spec((CONV_WIDTH, SSM_CONV_CH)), _const_spec((1, SSM_CONV_CH)),
                  _const_spec((1, LANES)), _const_spec((1, LANES)), _const_spec((1, SSM_INNER)),
                  _const_spec((1, SSM_INNER)),
                  pl.BlockSpec((DEC_TB, SSM_INNER, SSM_STATE), tok3)],
        out_specs=[row(SSM_INNER), pl.BlockSpec((DEC_TB, SSM_INNER, SSM_STATE), tok3)],
        out_shape=[jax.ShapeDtypeStruct((bsz, SSM_INNER), F32),
                   jax.ShapeDtypeStruct((bsz, SSM_INNER, SSM_STATE), F32)],
        compiler_params=_cparams(("parallel",)),
        name="ssd_decode",
    )(xbc, bufs, z, dt, conv_w, conv_b.reshape(1, SSM_CONV_CH), pad8(dt_bias), pad8(a_log),
      jnp.repeat(d_skip, SSM_HEAD_DIM).reshape(1, SSM_INNER), norm_g.reshape(1, SSM_INNER), h0)


def _dn_decode_kernel(x_ref, buf_ref, z_ref, ba_ref, cw_ref, dtb_ref, alog_ref, nrm_ref, s0_ref,
                      o_ref, sout_ref):
    tb = x_ref.shape[0]
    act = _silu(_one_step_conv(x_ref, buf_ref, cw_ref))
    ba = ba_ref[...]
    beta = _sigmoid(ba)
    eg = jnp.exp(-jnp.exp(alog_ref[...]) * _softplus(ba + dtb_ref[...]))
    z = z_ref[...]
    nrm = nrm_ref[...]
    outs = []
    for h in range(DN_HEADS):
        q_h = act[:, h * DN_DK:(h + 1) * DN_DK]
        k_h = act[:, DN_QK + h * DN_DK:DN_QK + (h + 1) * DN_DK]
        v_h = act[:, 2 * DN_QK + h * DN_DV:2 * DN_QK + (h + 1) * DN_DV]
        q_h = q_h * lax.rsqrt(jnp.sum(q_h * q_h, axis=-1, keepdims=True) + EPS) * (DN_DK ** -0.5)
        k_h = k_h * lax.rsqrt(jnp.sum(k_h * k_h, axis=-1, keepdims=True) + EPS)
        beta_c = beta[:, h:h + 1]
        eg_c = eg[:, DN_HEADS + h:DN_HEADS + h + 1]
        s_h = s0_ref[:, h * DN_DK:(h + 1) * DN_DK, :]
        kq8 = _token_major(_pad_rows([k_h, q_h], tb, DN_DK))
        r = jnp.transpose(lax.dot_general(kq8.astype(BF16), s_h.astype(BF16), BATCH_NN,
                                          preferred_element_type=F32), (1, 0, 2))
        v_new = beta_c * v_h - (beta_c * eg_c) * r[0]
        o_h = eg_c * r[1] + jnp.sum(q_h * k_h, axis=-1, keepdims=True) * v_new
        outs.append(_rms(o_h, nrm) * _silu(z[:, h * DN_DV:(h + 1) * DN_DV]))
        k_cols = _columns(k_h)
        for b in range(tb):
            sout_ref[b, h * DN_DK:(h + 1) * DN_DK, :] = (
                s_h[b] * jnp.broadcast_to(eg_c[b:b + 1, :], (DN_DK, DN_DV)) + k_cols[:, b:b + 1] * v_new[b:b + 1, :])
    o_ref[...] = jnp.concatenate(outs, axis=1)


def dn_decode(qkv, bufs, z, ba, conv_w, dt_bias, a_log, norm_g, s0):
    bsz = qkv.shape[0]
    tok = lambda i: (i, 0)
    tok3 = lambda i: (i, 0, 0)
    pad_a = lambda v: jnp.pad(v.reshape(1, DN_HEADS), ((0, 0), (DN_HEADS, LANES - 2 * DN_HEADS)))
    row = lambda w: pl.BlockSpec((DEC_TB, w), tok)
    return pl.pallas_call(
        _dn_decode_kernel,
        grid=(bsz // DEC_TB,),
        in_specs=[row(DN_CONV_CH), pl.BlockSpec((DEC_TB, CONV_WIDTH - 1, DN_CONV_CH), tok3),
                  row(DN_VW), row(LANES),
                  _const_spec((CONV_WIDTH, DN_CONV_CH)),
                  _const_spec((1, LANES)), _const_spec((1, LANES)), _const_spec((1, DN_DV)),
                  pl.BlockSpec((DEC_TB, DN_QK, DN_DV), tok3)],
        out_specs=[row(DN_VW), pl.BlockSpec((DEC_TB, DN_QK, DN_DV), tok3)],
        out_shape=[jax.ShapeDtypeStruct((bsz, DN_VW), F32), jax.ShapeDtypeStruct((bsz, DN_QK, DN_DV), F32)],
        compiler_params=_cparams(("parallel",)),
        name="dn_decode",
    )(qkv, bufs, z, ba, conv_w, pad_a(dt_bias), pad_a(a_log), norm_g.reshape(1, DN_DV), s0)


def _pad_cols(w, n_pad):
    return jnp.pad(w, ((0, 0), (0, n_pad - w.shape[1])))


EVEN_SEGS = ((0, 512), (512, 640), (640, 768), (768, 1280), (1280, 2304), (2304, 2432))
ODD_SEGS = ((0, 3072), (3072, 4096), (4096, 4224))


def _trunk(x_seq, p, states):
    prompt = states is None
    bsz, length = x_seq.shape[0], x_seq.shape[1]
    t = bsz * length
    tm = 512 if t % 512 == 0 else t
    x = x_seq.reshape(t, D_MODEL)
    seq = lambda u: u.reshape(bsz, length, u.shape[-1])

    q, k, v, z, xbc, dt = prenorm_proj(x, p['ln_mix'][0], p['w_in_even'], EVEN_SEGS, tm)
    if prompt:
        v3, xbc3 = seq(v), seq(xbc)
        att, k_normed = swa_attention(seq(q), seq(k), v3, p['q_norm'], p['k_norm'], p['attn_sinks'])
        new_k = k_normed[:, -WINDOW:].reshape(bsz, WINDOW, SWA_KV_HEADS, HEAD_DIM)
        new_v = v3[:, -WINDOW:].reshape(bsz, WINDOW, SWA_KV_HEADS, HEAD_DIM)
        new_ssm_conv = xbc3[:, -(CONV_WIDTH - 1):]
        y_ssm, new_h = ssd_mixer(xbc3, seq(z), seq(dt), p['ssm_conv_w'], p['ssm_conv_b'], p['ssm_dt_bias'],
                                 p['ssm_A_log'], p['ssm_D'], p['ssm_norm'])
    else:
        k_win, v_win, ssm_h, ssm_conv, dn_s, dn_conv = states
        kb = k_win.reshape(bsz, WINDOW, SWA_KV)
        vb = v_win.reshape(bsz, WINDOW, SWA_KV)
        att, k_normed = swa_decode(q, k, v, kb, vb, p['q_norm'], p['k_norm'], p['attn_sinks'])
        new_k = jnp.concatenate([kb[:, 1:], k_normed[:, None]], axis=1).reshape(bsz, WINDOW, SWA_KV_HEADS, HEAD_DIM)
        new_v = jnp.concatenate([vb[:, 1:], v[:, None]], axis=1).reshape(bsz, WINDOW, SWA_KV_HEADS, HEAD_DIM)
        new_ssm_conv = jnp.concatenate([ssm_conv[:, 1:], xbc[:, None, :]], axis=1)
        y_ssm, new_h = ssd_decode(xbc, ssm_conv, z, dt, p['ssm_conv_w'],
                                  p['ssm_conv_b'], p['ssm_dt_bias'], p['ssm_A_log'], p['ssm_D'], p['ssm_norm'],
                                  ssm_h.reshape(bsz, SSM_INNER, SSM_STATE))
    x = mix_out_and_moe(x, [att.reshape(t, SWA_Q), y_ssm.reshape(t, SSM_INNER)],
                        [p['w_out_even'][:SWA_Q], p['w_out_even'][SWA_Q:]],
                        p['ln_ffn'][0], p['w_route'][0], p['b_route'][0],
                        p['moe_w_gate'][0], p['moe_w_up'][0], p['moe_w_down'][0])

    qkv, zz, ba = prenorm_proj(x, p['ln_mix'][1], p['w_in_odd'], ODD_SEGS, tm)
    if prompt:
        qkv3 = seq(qkv)
        new_dn_conv = qkv3[:, -(CONV_WIDTH - 1):]
        o_dn, new_s = dn_mixer(qkv3, seq(zz), seq(ba), p['dn_conv_w'], p['dn_dt_bias'], p['dn_A_log'],
                               p['dn_norm'], DN_SEQS_PER_STEP)
    else:
        new_dn_conv = jnp.concatenate([dn_conv[:, 1:], qkv[:, None, :]], axis=1)
        o_dn, new_s = dn_decode(qkv, dn_conv, zz, ba, p['dn_conv_w'],
                                p['dn_dt_bias'], p['dn_A_log'], p['dn_norm'], dn_s.reshape(bsz, DN_QK, DN_DV))
    x = mix_out_and_moe(x, [o_dn.reshape(t, DN_VW)], [p['w_out_odd']],
                        p['ln_ffn'][1], p['w_route'][1], p['b_route'][1],
                        p['moe_w_gate'][1], p['moe_w_up'][1], p['moe_w_down'][1])

    return (x.reshape(bsz, length, D_MODEL), new_k[None], new_v[None],
            new_h.reshape(1, bsz, SSM_HEADS, SSM_HEAD_DIM, SSM_STATE), new_ssm_conv[None],
            new_s.reshape(1, bsz, DN_HEADS, DN_DK, DN_DV), new_dn_conv[None])


def kernel(x_prompt, x_sample, cache_k_win, cache_v_win, state_ssm, state_ssm_conv, state_dn, state_dn_conv,
           ln_mix, ln_ffn, w_in_even, q_norm, k_norm, attn_sinks, ssm_conv_w, ssm_conv_b, ssm_dt_bias,
           ssm_A_log, ssm_D, ssm_norm, w_out_even, w_in_odd, dn_conv_w, dn_dt_bias, dn_A_log, dn_norm,
           w_out_odd, moe_w_group, moe_b_group, moe_w_router, moe_b_router, moe_w_gate, moe_w_up, moe_w_down):
    w_route = _pad_cols(jnp.concatenate([moe_w_router, moe_w_group], axis=-1).reshape(-1, N_EXPERTS + N_GROUPS),
                        LANES).reshape(2, D_MODEL, LANES)
    b_route = _pad_cols(jnp.concatenate([moe_b_router, moe_b_group], axis=-1), LANES).reshape(2, 1, LANES)
    p = {
        'ln_mix': ln_mix, 'ln_ffn': ln_ffn,
        'w_in_even': _pad_cols(w_in_even[0], EVEN_SEGS[-1][1]).astype(BF16),
        'q_norm': q_norm[0], 'k_norm': k_norm[0], 'attn_sinks': attn_sinks[0],
        'ssm_conv_w': ssm_conv_w[0], 'ssm_conv_b': ssm_conv_b[0], 'ssm_dt_bias': ssm_dt_bias[0],
        'ssm_A_log': ssm_A_log[0], 'ssm_D': ssm_D[0], 'ssm_norm': ssm_norm[0],
        'w_out_even': w_out_even[0].astype(BF16),
        'w_in_odd': _pad_cols(w_in_odd[0], ODD_SEGS[-1][1]).astype(BF16),
        'dn_conv_w': dn_conv_w[0], 'dn_dt_bias': dn_dt_bias[0], 'dn_A_log': dn_A_log[0], 'dn_norm': dn_norm[0],
        'w_out_odd': w_out_odd[0].astype(BF16),
        'w_route': w_route, 'b_route': b_route,
        'moe_w_gate': moe_w_gate.astype(BF16).reshape(2, N_GROUPS, EXPERTS_PER_GROUP, D_MODEL, EXPERT_FF),
        'moe_w_up': moe_w_up.astype(BF16).reshape(2, N_GROUPS, EXPERTS_PER_GROUP, D_MODEL, EXPERT_FF),
        'moe_w_down': moe_w_down.astype(BF16).reshape(2, N_GROUPS, EXPERTS_PER_GROUP, EXPERT_FF, D_MODEL),
    }
    y_p, kp, vp, sp, scp, dnp, dncp = _trunk(x_prompt, p, None)
    sample_states = (cache_k_win[0], cache_v_win[0], state_ssm[0], state_ssm_conv[0], state_dn[0],
                     state_dn_conv[0])
    y_s, ks, vs, ss, scs, dns, dncs = _trunk(x_sample, p, sample_states)
    return (y_p, y_s, kp, ks, vp, vs, sp, ss, scp, scs, dnp, dns, dncp, dncs)
```
